```python
import math
import jax, jax.numpy as jnp
from jax import lax
import numpy as np

D_MODEL = 1024
BATCH = 4
SEQ = 4096
DEPTH = 4
DEC_BATCH = 128
DEC_SEQ = 4
PAST_LEN = 8192
PAGE_SIZE = 128

N_MIXERS = 4
N_CONV_LAYERS = (DEPTH + 3) // 4
N_MLA_LAYERS = (DEPTH + 2) // 4
N_SSD_LAYERS = (DEPTH + 1) // 4
N_DIL_LAYERS = DEPTH // 4

D_FF = 4 * D_MODEL
PLE_DIM = 256
EPS = 1e-6

CONV_WIDTH = 31

MLA_HEADS = 8
MLA_NOPE = 128
MLA_ROPE = 64
MLA_V = 128
MLA_Q_RANK = 384
MLA_KV_RANK = 256
MLA_SCALE = (MLA_NOPE + MLA_ROPE) ** -0.5
MLA_BLOCK = 128
ROPE_THETA = 10000.0

SSD_INNER = 2 * D_MODEL
SSD_HEAD_DIM = 64
SSD_HEADS = SSD_INNER // SSD_HEAD_DIM
SSD_GROUPS = 4
SSD_STATE = 128
SSD_CONV = 4
SSD_CHUNK = 128
SSD_CONV_DIM = SSD_INNER + 2 * SSD_GROUPS * SSD_STATE
SSD_IN_DIM = SSD_INNER + SSD_CONV_DIM + SSD_HEADS

DIL_PATTERNS = ((128, 1), (512, 4), (2048, 16))
DIL_GROUPS = len(DIL_PATTERNS)
DIL_HEADS_PER_GROUP = 8
DIL_HEAD_DIM = 64
DIL_HEADS = DIL_GROUPS * DIL_HEADS_PER_GROUP
DIL_SCALE = DIL_HEAD_DIM ** -0.5

REL_BUCKETS = 32
REL_MAX_DIST = 2048

kernel_name = "hybrid_conv_mla_ssd_dilated_decoder_step"


def rms_norm(x, g):
    xf = x.astype(jnp.float32)
    y = xf * lax.rsqrt(jnp.mean(xf * xf, axis=-1, keepdims=True) + EPS)
    return (y * g.astype(jnp.float32)).astype(x.dtype)


def layer_norm(x, g, b):
    xf = x.astype(jnp.float32)
    xc = xf - jnp.mean(xf, axis=-1, keepdims=True)
    y = xc * lax.rsqrt(jnp.mean(xc * xc, axis=-1, keepdims=True) + EPS)
    return (y * g.astype(jnp.float32) + b.astype(jnp.float32)).astype(x.dtype)


def causal_dwconv(hist, x, w, b):
    width = w.shape[0]
    full = jnp.concatenate([hist.astype(x.dtype), x], axis=1)
    y = lax.conv_general_dilated(full, w.astype(x.dtype)[:, None, :], window_strides=(1,), padding="VALID",
                                 dimension_numbers=("NWC", "WIO", "NWC"), feature_group_count=x.shape[-1])
    return y + b.astype(x.dtype), full[:, full.shape[1] - (width - 1):]


def conformer_conv(u, hist, w_in, w_dw, b_dw, ln_g, ln_b, w_out):
    a, gate = jnp.split(u @ w_in, 2, axis=-1)
    glu = a * jax.nn.sigmoid(gate)
    c, new_hist = causal_dwconv(hist, glu, w_dw, b_dw)
    c = jax.nn.silu(layer_norm(c, ln_g, ln_b))
    return c @ w_out, new_hist


def rope_angles(pos):
    inv = ROPE_THETA ** (-jnp.arange(0, MLA_ROPE, 2, dtype=jnp.float32) / MLA_ROPE)
    ang = pos.astype(jnp.float32)[:, None] * inv[None, :]
    return jnp.cos(ang), jnp.sin(ang)


def apply_rope(x, cos, sin):
    x1, x2 = jnp.split(x.astype(jnp.float32), 2, axis=-1)
    return jnp.concatenate([x1 * cos - x2 * sin, x2 * cos + x1 * sin], axis=-1).astype(x.dtype)


def mla_project(u, pos, w_dq, g_q, w_uq, w_dkv, g_kv, w_uk):
    b, l, _ = u.shape
    cq = rms_norm(u @ w_dq, g_q)
    q = (cq @ w_uq).reshape(b, l, MLA_HEADS, MLA_NOPE + MLA_ROPE)
    q_nope, q_pe = q[..., :MLA_NOPE], q[..., MLA_NOPE:]
    kv = u @ w_dkv
    ckv = rms_norm(kv[..., :MLA_KV_RANK], g_kv)
    cos, sin = rope_angles(pos)
    q_pe = apply_rope(q_pe, cos[None, :, None], sin[None, :, None])
    kpe = apply_rope(kv[..., MLA_KV_RANK:], cos[None], sin[None])
    q_lat = jnp.einsum("blhn,hnc->blhc", q_nope, w_uk)
    return q_lat, q_pe, ckv, kpe


def mla_attend(q_lat, q_pe, ckv, kpe, mask):
    s = (jnp.einsum("bqhc,bkc->bhqk", q_lat, ckv) + jnp.einsum("bqhr,bkr->bhqk", q_pe, kpe)).astype(jnp.float32) * MLA_SCALE
    s = jnp.where(mask[None, None], s, -jnp.inf)
    p = jax.nn.softmax(s, axis=-1).astype(ckv.dtype)
    return jnp.einsum("bhqk,bkc->bqhc", p, ckv)


def mla_out(o_lat, w_uv, w_o):
    b, l = o_lat.shape[:2]
    return jnp.einsum("blhc,hcv->blhv", o_lat, w_uv).reshape(b, l, MLA_HEADS * MLA_V) @ w_o


def mla_mixer_prompt(u, pos, w_dq, g_q, w_uq, w_dkv, g_kv, w_uk, w_uv, w_o):
    q_lat, q_pe, ckv, kpe = mla_project(u, pos, w_dq, g_q, w_uq, w_dkv, g_kv, w_uk)
    b, s = u.shape[:2]
    nb = s // MLA_BLOCK
    kpos = jnp.arange(s)

    def to_blocks(t):
        return jnp.moveaxis(t.reshape((b, nb, MLA_BLOCK) + t.shape[2:]), 1, 0)

    def one_block(args):
        i, ql, qp = args
        qpos = i * MLA_BLOCK + jnp.arange(MLA_BLOCK)
        return mla_attend(ql, qp, ckv, kpe, kpos[None, :] <= qpos[:, None])

    o = lax.map(one_block, (jnp.arange(nb), to_blocks(q_lat), to_blocks(q_pe)))
    o = jnp.moveaxis(o, 0, 1).reshape(b, s, MLA_HEADS, MLA_KV_RANK)
    return mla_out(o, w_uv, w_o), ckv, kpe


def gather_pages(pool, page_table):
    return pool[page_table].reshape(page_table.shape[0], -1, pool.shape[-1])


def mla_mixer_sample(u, pos, ckv_pool, kpe_pool, page_table, w_dq, g_q, w_uq, w_dkv, g_kv, w_uk, w_uv, w_o):
    q_lat, q_pe, ckv, kpe = mla_project(u, pos, w_dq, g_q, w_uq, w_dkv, g_kv, w_uk)
    keys_c = jnp.concatenate([gather_pages(ckv_pool, page_table).astype(ckv.dtype), ckv], axis=1)
    keys_r = jnp.concatenate([gather_pages(kpe_pool, page_table).astype(kpe.dtype), kpe], axis=1)
    kpos = jnp.arange(keys_c.shape[1])
    o = mla_attend(q_lat, q_pe, keys_c, keys_r, kpos[None, :] <= pos[:, None])
    return mla_out(o, w_uv, w_o), ckv, kpe


def ssd_scan(x, dt, a, bm, cm, h0):
    f32 = jnp.float32
    bsz, l = x.shape[:2]
    q = min(SSD_CHUNK, l)
    lp = -(-l // q) * q
    xf, bf, cf = x.astype(f32), bm.astype(f32), cm.astype(f32)
    if lp != l:
        padw = ((0, 0), (0, lp - l), (0, 0), (0, 0))
        xf, bf, cf = jnp.pad(xf, padw), jnp.pad(bf, padw), jnp.pad(cf, padw)
        dt = jnp.pad(dt, ((0, 0), (0, lp - l), (0, 0)))
    nc, r = lp // q, SSD_HEADS // SSD_GROUPS
    xdt = (xf * dt[..., None]).reshape(bsz, nc, q, SSD_GROUPS, r, SSD_HEAD_DIM)
    la = (dt * a.astype(f32)).reshape(bsz, nc, q, SSD_GROUPS, r)
    bq = bf.reshape(bsz, nc, q, SSD_GROUPS, SSD_STATE)
    cq = cf.reshape(bsz, nc, q, SSD_GROUPS, SSD_STATE)
    cs = jnp.cumsum(la, axis=2)
    causal = jnp.tril(jnp.ones((q, q), bool))
    seg = cs[:, :, :, None] - cs[:, :, None, :]
    decay = jnp.exp(jnp.where(causal[None, None, :, :, None, None], seg, -jnp.inf))
    cb = jnp.einsum("bcign,bcjgn->bcijg", cq, bq)
    y_diag = jnp.einsum("bcijgr,bcjgrp->bcigrp", cb[..., None] * decay, xdt)
    xw = xdt * jnp.exp(cs[:, :, -1:] - cs)[..., None]
    chunk_states = jnp.einsum("bcjgn,bcjgrp->bcgrpn", bq, xw)
    chunk_decay = jnp.exp(cs[:, :, -1])

    def step(h, inp):
        st, dec = inp
        return dec[..., None, None] * h + st, h

    h0g = h0.astype(f32).reshape(bsz, SSD_GROUPS, r, SSD_HEAD_DIM, SSD_STATE)
    h_fin, h_in = lax.scan(step, h0g, (jnp.moveaxis(chunk_states, 1, 0), jnp.moveaxis(chunk_decay, 1, 0)))
    h_in = jnp.moveaxis(h_in, 0, 1)
    y_off = jnp.einsum("bcign,bcgrpn->bcigrp", cq, h_in) * jnp.exp(cs)[..., None]
    y = (y_diag + y_off).reshape(bsz, lp, SSD_HEADS, SSD_HEAD_DIM)[:, :l]
    return y.astype(x.dtype), h_fin.reshape(bsz, SSD_HEADS, SSD_HEAD_DIM, SSD_STATE)


def ssd_mixer(u, conv_hist, h0, w_in, w_conv, b_conv, dt_bias, a_log, d_skip, g_norm, w_out):
    b, l, _ = u.shape
    zxbcdt = u @ w_in
    z = zxbcdt[..., :SSD_INNER]
    xbc = zxbcdt[..., SSD_INNER:SSD_INNER + SSD_CONV_DIM]
    dt = zxbcdt[..., SSD_INNER + SSD_CONV_DIM:]
    xbc, new_hist = causal_dwconv(conv_hist, xbc, w_conv, b_conv)
    xbc = jax.nn.silu(xbc)
    gn = SSD_GROUPS * SSD_STATE
    xs = xbc[..., :SSD_INNER].reshape(b, l, SSD_HEADS, SSD_HEAD_DIM)
    bm = xbc[..., SSD_INNER:SSD_INNER + gn].reshape(b, l, SSD_GROUPS, SSD_STATE)
    cm = xbc[..., SSD_INNER + gn:].reshape(b, l, SSD_GROUPS, SSD_STATE)
    dt = jax.nn.softplus(dt.astype(jnp.float32) + dt_bias.astype(jnp.float32))
    a = -jnp.exp(a_log.astype(jnp.float32))
    y, h_fin = ssd_scan(xs, dt, a, bm, cm, h0)
    y = (y + xs * d_skip[:, None]).reshape(b, l, SSD_INNER)
    gated = (y * jax.nn.silu(z)).astype(jnp.float32).reshape(b, l, SSD_GROUPS, -1)
    gated = gated * lax.rsqrt(jnp.mean(gated * gated, axis=-1, keepdims=True) + EPS)
    y = (gated.reshape(b, l, SSD_INNER) * g_norm.astype(jnp.float32)).astype(u.dtype)
    return y @ w_out, new_hist, h_fin.astype(u.dtype)


def t5_bucket(dist):
    max_exact = REL_BUCKETS // 2
    d = jnp.maximum(dist, 1).astype(jnp.float32)
    large = max_exact + (jnp.log(d / max_exact) / math.log(REL_MAX_DIST / max_exact) * (REL_BUCKETS - max_exact)).astype(jnp.int32)
    return jnp.where(dist < max_exact, dist, jnp.minimum(large, REL_BUCKETS - 1))


def dil_group_bias(rel_bias, g, r, nk):
    tab = rel_bias[t5_bucket(r * jnp.arange(nk + 1))]
    return tab[:, g * DIL_HEADS_PER_GROUP:(g + 1) * DIL_HEADS_PER_GROUP].T.astype(jnp.float32)


def dilated_band_attend(q, k, v, bias, r, nk):
    b, s, h, d = q.shape
    m = s // r
    blk = nk
    nb = -(-m // blk)
    mp = nb * blk

    def split(t):
        t = jnp.transpose(t.reshape(b, m, r, h, d), (0, 2, 1, 3, 4))
        t = jnp.pad(t, ((0, 0), (0, 0), (0, mp - m), (0, 0), (0, 0)))
        return t.reshape(b, r, nb, blk, h, d)

    def with_prev(t):
        prev = jnp.pad(t, ((0, 0), (0, 0), (1, 0), (0, 0), (0, 0), (0, 0)))[:, :, :-1]
        return jnp.concatenate([prev, t], axis=3)

    qb, kk, vv = split(q), with_prev(split(k)), with_prev(split(v))
    qi, ki = jnp.arange(blk), jnp.arange(2 * blk)
    dm = qi[:, None] + blk - ki[None, :]
    k_abs = jnp.arange(nb)[:, None] * blk + ki[None, :] - blk
    mask = ((dm >= 0) & (dm <= nk))[None] & (k_abs[:, None, :] >= 0)
    s_ = jnp.einsum("brnqhd,brnkhd->brnhqk", qb, kk).astype(jnp.float32) * DIL_SCALE
    s_ = s_ + bias[:, jnp.clip(dm, 0, nk)][None, None, None]
    s_ = jnp.where(mask[None, None, :, None], s_, -jnp.inf)
    lse = jax.nn.logsumexp(s_, axis=-1)
    p = jnp.exp(s_ - lse[..., None]).astype(v.dtype)
    o = jnp.einsum("brnhqk,brnkhd->brnqhd", p, vv).reshape(b, r, mp, h, d)[:, :, :m]
    o = jnp.transpose(o, (0, 2, 1, 3, 4)).reshape(b, s, h, d)
    lse = jnp.swapaxes(lse, 3, 4).reshape(b, r, mp, h)[:, :, :m]
    lse = jnp.transpose(lse, (0, 2, 1, 3)).reshape(b, s, h)
    return o, lse


def dilated_gather_attend(q, k, v, buf, bias, r, nk):
    wb, t = buf.shape[1], q.shape[1]
    full = jnp.concatenate([buf.astype(k.dtype), jnp.stack([k, v], axis=2)], axis=1)
    idx = wb + jnp.arange(t)[:, None] - r * jnp.arange(nk + 1)[None, :]
    valid = idx >= 0
    kv = full[:, jnp.maximum(idx, 0)]
    s_ = jnp.einsum("bthd,btkhd->bhtk", q, kv[:, :, :, 0]).astype(jnp.float32) * DIL_SCALE + bias[:, None, :]
    s_ = jnp.where(valid[None, None], s_, -jnp.inf)
    lse = jax.nn.logsumexp(s_, axis=-1)
    p = jnp.exp(s_ - lse[..., None]).astype(v.dtype)
    o = jnp.einsum("bhtk,btkhd->bthd", p, kv[:, :, :, 1])
    return o, jnp.swapaxes(lse, 1, 2), full[:, t:]


def dil_qkv(u, w_qkv):
    b, l, _ = u.shape
    return (u @ w_qkv).reshape(b, l, 3, DIL_GROUPS, DIL_HEADS_PER_GROUP, DIL_HEAD_DIM)


def dil_combine(outs, lses, w_o):
    o, lse = jnp.stack(outs), jnp.stack(lses)
    wts = jax.nn.softmax(lse, axis=0).astype(o.dtype)
    o = jnp.sum(wts[..., None] * o, axis=0)
    b, l = o.shape[:2]
    return o.reshape(b, l, DIL_HEADS_PER_GROUP * DIL_HEAD_DIM) @ w_o


def dil_mixer_prompt(u, w_qkv, w_o, rel_bias):
    qkv = dil_qkv(u, w_qkv)
    s = u.shape[1]
    outs, lses, bufs = [], [], []
    for g, (w, r) in enumerate(DIL_PATTERNS):
        k, v = qkv[:, :, 1, g], qkv[:, :, 2, g]
        o, lse = dilated_band_attend(qkv[:, :, 0, g], k, v, dil_group_bias(rel_bias, g, r, w // r), r, w // r)
        outs.append(o)
        lses.append(lse)
        bufs.append(jnp.stack([k, v], axis=2)[:, s - min(w, s):])
    return dil_combine(outs, lses, w_o), bufs


def dil_mixer_sample(u, bufs_in, w_qkv, w_o, rel_bias):
    qkv = dil_qkv(u, w_qkv)
    outs, lses, bufs = [], [], []
    for g, (w, r) in enumerate(DIL_PATTERNS):
        o, lse, nbuf = dilated_gather_attend(qkv[:, :, 0, g], qkv[:, :, 1, g], qkv[:, :, 2, g], bufs_in[g],
                                             dil_group_bias(rel_bias, g, r, w // r), r, w // r)
        outs.append(o)
        lses.append(lse)
        bufs.append(nbuf)
    return dil_combine(outs, lses, w_o), bufs


def channel_and_ple(h, p, g_ffn, w1, w2, g_ple, w_gate, w_proj):
    h = h + jnp.square(jax.nn.relu(rms_norm(h, g_ffn) @ w1)) @ w2
    return h + jax.nn.sigmoid(rms_norm(h, g_ple) @ w_gate) * (p @ w_proj)


def setup_inputs(seed: int = 0) -> dict:
    key = jax.random.key(seed)
    ks = iter(jax.random.split(key, 64))
    f32 = jnp.float32

    def nrm(shape, scale):
        return jax.random.normal(next(ks), shape, f32) * scale

    def gain(shape):
        return 1.0 + nrm(shape, 0.02)

    n_pages = PAST_LEN // PAGE_SIZE
    n_used = DEC_BATCH * n_pages
    n_pool = n_used + n_used // 4
    x_prompt = nrm((BATCH, SEQ, D_MODEL), 1.0)
    x_sample = nrm((DEC_BATCH, DEC_SEQ, D_MODEL), 1.0)
    state_conv = nrm((N_CONV_LAYERS, DEC_BATCH, CONV_WIDTH - 1, D_MODEL), 0.5)
    cache_mla_ckv = nrm((N_MLA_LAYERS, n_pool, PAGE_SIZE, MLA_KV_RANK), 1.0)
    cache_mla_kpe = nrm((N_MLA_LAYERS, n_pool, PAGE_SIZE, MLA_ROPE), 1.0)
    state_ssd_conv = nrm((N_SSD_LAYERS, DEC_BATCH, SSD_CONV - 1, SSD_CONV_DIM), 1.0)
    state_ssd = nrm((N_SSD_LAYERS, DEC_BATCH, SSD_HEADS, SSD_HEAD_DIM, SSD_STATE), 0.1)
    dil_states = [nrm((N_DIL_LAYERS, DEC_BATCH, min(w, PAST_LEN), 2, DIL_HEADS_PER_GROUP, DIL_HEAD_DIM), 1.0)
                  for (w, r) in DIL_PATTERNS]
    page_table = jax.random.permutation(next(ks), n_pool)[:n_used].reshape(DEC_BATCH, n_pages).astype(jnp.int32)
    p_prompt = nrm((DEPTH, BATCH, SEQ, PLE_DIM), 1.0)
    p_sample = nrm((DEPTH, DEC_BATCH, DEC_SEQ, PLE_DIM), 1.0)
    nA, nB, nC, nD = N_CONV_LAYERS, N_MLA_LAYERS, N_SSD_LAYERS, N_DIL_LAYERS
    dt0 = jnp.exp(jax.random.uniform(next(ks), (nC, SSD_HEADS), f32, math.log(1e-3), math.log(1e-1)))
    return {
        "x_prompt": x_prompt, "x_sample": x_sample,
        "state_conv": state_conv, "cache_mla_ckv": cache_mla_ckv, "cache_mla_kpe": cache_mla_kpe,
        "state_ssd_conv": state_ssd_conv, "state_ssd": state_ssd,
        "state_dil0_kv": dil_states[0], "state_dil1_kv": dil_states[1], "state_dil2_kv": dil_states[2],
        "page_table": page_table, "p_prompt": p_prompt, "p_sample": p_sample,
        "norm_mix": gain((DEPTH, D_MODEL)), "norm_ffn": gain((DEPTH, D_MODEL)),
        "norm_ple": gain((DEPTH, D_MODEL)), "norm_final": gain((D_MODEL,)),
        "conv_w_in": nrm((nA, D_MODEL, 2 * D_MODEL), D_MODEL ** -0.5),
        "conv_w_dw": nrm((nA, CONV_WIDTH, D_MODEL), CONV_WIDTH ** -0.5),
        "conv_b_dw": nrm((nA, D_MODEL), 0.02),
        "conv_ln_g": gain((nA, D_MODEL)), "conv_ln_b": nrm((nA, D_MODEL), 0.02),
        "conv_w_out": nrm((nA, D_MODEL, D_MODEL), D_MODEL ** -0.5),
        "mla_w_dq": nrm((nB, D_MODEL, MLA_Q_RANK), D_MODEL ** -0.5),
        "mla_g_q": gain((nB, MLA_Q_RANK)),
        "mla_w_uq": nrm((nB, MLA_Q_RANK, MLA_HEADS * (MLA_NOPE + MLA_ROPE)), MLA_Q_RANK ** -0.5),
        "mla_w_dkv": nrm((nB, D_MODEL, MLA_KV_RANK + MLA_ROPE), D_MODEL ** -0.5),
        "mla_g_kv": gain((nB, MLA_KV_RANK)),
        "mla_w_uk": nrm((nB, MLA_HEADS, MLA_NOPE, MLA_KV_RANK), MLA_KV_RANK ** -0.5),
        "mla_w_uv": nrm((nB, MLA_HEADS, MLA_KV_RANK, MLA_V), MLA_KV_RANK ** -0.5),
        "mla_w_o": nrm((nB, MLA_HEADS * MLA_V, D_MODEL), (MLA_HEADS * MLA_V) ** -0.5),
        "ssd_w_in": nrm((nC, D_MODEL, SSD_IN_DIM), D_MODEL ** -0.5),
        "ssd_w_conv": nrm((nC, SSD_CONV, SSD_CONV_DIM), SSD_CONV ** -0.5),
        "ssd_b_conv": nrm((nC, SSD_CONV_DIM), 0.02),
        "ssd_dt_bias": dt0 + jnp.log(-jnp.expm1(-dt0)),
        "ssd_a_log": jnp.log(jax.random.uniform(next(ks), (nC, SSD_HEADS), f32, 1.0, 16.0)),
        "ssd_d": gain((nC, SSD_HEADS)),
        "ssd_g_norm": gain((nC, SSD_INNER)),
        "ssd_w_out": nrm((nC, SSD_INNER, D_MODEL), SSD_INNER ** -0.5),
        "dil_w_qkv": nrm((nD, D_MODEL, 3 * DIL_HEADS * DIL_HEAD_DIM), D_MODEL ** -0.5),
        "dil_w_o": nrm((nD, DIL_HEADS_PER_GROUP * DIL_HEAD_DIM, D_MODEL), (DIL_HEADS_PER_GROUP * DIL_HEAD_DIM) ** -0.5),
        "rel_bias": nrm((REL_BUCKETS, DIL_HEADS), 0.1),
        "ffn_w1": nrm((DEPTH, D_MODEL, D_FF), D_MODEL ** -0.5),
        "ffn_w2": nrm((DEPTH, D_FF, D_MODEL), D_FF ** -0.5),
        "ple_w_gate": nrm((DEPTH, D_MODEL, D_MODEL), D_MODEL ** -0.5),
        "ple_w_proj": nrm((DEPTH, PLE_DIM, D_MODEL), PLE_DIM ** -0.5),
    }


def reference(x_prompt, x_sample, state_conv, cache_mla_ckv, cache_mla_kpe, state_ssd_conv, state_ssd,
              state_dil0_kv, state_dil1_kv, state_dil2_kv, page_table, p_prompt, p_sample,
              norm_mix, norm_ffn, norm_ple, norm_final,
              conv_w_in, conv_w_dw, conv_b_dw, conv_ln_g, conv_ln_b, conv_w_out,
              mla_w_dq, mla_g_q, mla_w_uq, mla_w_dkv, mla_g_kv, mla_w_uk, mla_w_uv, mla_w_o,
              ssd_w_in, ssd_w_conv, ssd_b_conv, ssd_dt_bias, ssd_a_log, ssd_d, ssd_g_norm, ssd_w_out,
              dil_w_qkv, dil_w_o, rel_bias, ffn_w1, ffn_w2, ple_w_gate, ple_w_proj):
    hp, hs = x_prompt, x_sample
    bp, s = hp.shape[:2]
    t = hs.shape[1]
    pos_p = jnp.arange(s)
    pos_s = PAST_LEN + jnp.arange(t)
    conv_p, conv_s = [], []
    ckv_p, kpe_p, ckv_s, kpe_s = [], [], [], []
    ssdc_p, ssdh_p, ssdc_s, ssdh_s = [], [], [], []
    dil_p = [[], [], []]
    dil_s = [[], [], []]
    dil_in = (state_dil0_kv, state_dil1_kv, state_dil2_kv)
    for i in range(DEPTH):
        kind, j = i % N_MIXERS, i // N_MIXERS
        up, us = rms_norm(hp, norm_mix[i]), rms_norm(hs, norm_mix[i])
        if kind == 0:
            prm = (conv_w_in[j], conv_w_dw[j], conv_b_dw[j], conv_ln_g[j], conv_ln_b[j], conv_w_out[j])
            mp, st_p = conformer_conv(up, jnp.zeros((bp, CONV_WIDTH - 1, D_MODEL), up.dtype), *prm)
            ms, st_s = conformer_conv(us, state_conv[j], *prm)
            conv_p.append(st_p)
            conv_s.append(st_s)
        elif kind == 1:
            prm = (mla_w_dq[j], mla_g_q[j], mla_w_uq[j], mla_w_dkv[j], mla_g_kv[j], mla_w_uk[j], mla_w_uv[j], mla_w_o[j])
            mp, c_p, r_p = mla_mixer_prompt(up, pos_p, *prm)
            ms, c_s, r_s = mla_mixer_sample(us, pos_s, cache_mla_ckv[j], cache_mla_kpe[j], page_table, *prm)
            ckv_p.append(c_p)
            kpe_p.append(r_p)
            ckv_s.append(c_s)
            kpe_s.append(r_s)
        elif kind == 2:
            prm = (ssd_w_in[j], ssd_w_conv[j], ssd_b_conv[j], ssd_dt_bias[j], ssd_a_log[j], ssd_d[j], ssd_g_norm[j], ssd_w_out[j])
            mp, hc_p, hh_p = ssd_mixer(up, jnp.zeros((bp, SSD_CONV - 1, SSD_CONV_DIM), up.dtype),
                                       jnp.zeros((bp, SSD_HEADS, SSD_HEAD_DIM, SSD_STATE), jnp.float32), *prm)
            ms, hc_s, hh_s = ssd_mixer(us, state_ssd_conv[j], state_ssd[j], *prm)
            ssdc_p.append(hc_p)
            ssdh_p.append(hh_p)
            ssdc_s.append(hc_s)
            ssdh_s.append(hh_s)
        else:
            mp, bufs_p = dil_mixer_prompt(up, dil_w_qkv[j], dil_w_o[j], rel_bias)
            ms, bufs_s = dil_mixer_sample(us, [d[j] for d in dil_in], dil_w_qkv[j], dil_w_o[j], rel_bias)
            for g in range(DIL_GROUPS):
                dil_p[g].append(bufs_p[g])
                dil_s[g].append(bufs_s[g])
        hp = hp + mp
        hs = hs + ms
        hp = channel_and_ple(hp, p_prompt[i], norm_ffn[i], ffn_w1[i], ffn_w2[i], norm_ple[i], ple_w_gate[i], ple_w_proj[i])
        hs = channel_and_ple(hs, p_sample[i], norm_ffn[i], ffn_w1[i], ffn_w2[i], norm_ple[i], ple_w_gate[i], ple_w_proj[i])
    y_prompt = rms_norm(hp, norm_final)
    y_sample = rms_norm(hs, norm_final)
    return (y_prompt, y_sample,
            jnp.stack(conv_p), jnp.stack(conv_s),
            jnp.stack(ckv_p), jnp.stack(kpe_p), jnp.stack(ckv_s), jnp.stack(kpe_s),
            jnp.stack(ssdc_p), jnp.stack(ssdh_p), jnp.stack(ssdc_s), jnp.stack(ssdh_s),
            jnp.stack(dil_p[0]), jnp.stack(dil_p[1]), jnp.stack(dil_p[2]),
            jnp.stack(dil_s[0]), jnp.stack(dil_s[1]), jnp.stack(dil_s[2]))
```

```python
import functools
import math

import jax
import jax.numpy as jnp
from jax import lax
from jax.experimental import pallas as pl
from jax.experimental.pallas import tpu as pltpu

F32 = jnp.float32
BF16 = jnp.bfloat16
EPS = 1e-6

LANES = 128
SUBLANES = 8
VMEM_LIMIT_BYTES = 56 * 1024 * 1024

PAGE_SIZE = 128
CONV_WIDTH = 31
MLA_HEADS = 8
MLA_NOPE = 128
MLA_ROPE = 64
MLA_KV_RANK = 256
MLA_SCALE = (MLA_NOPE + MLA_ROPE) ** -0.5
ROPE_THETA = 10000.0
SSD_HEAD_DIM = 64
SSD_GROUPS = 4
SSD_STATE = 128
SSD_CONV = 4
SSD_CHUNK = 128
DIL_PATTERNS = ((128, 1), (512, 4), (2048, 16))
DIL_HEADS_PER_GROUP = 8
DIL_HEAD_DIM = 64
DIL_SCALE = DIL_HEAD_DIM ** -0.5
REL_BUCKETS = 32
REL_MAX_DIST = 2048


def _params(*sem):
    return pltpu.CompilerParams(dimension_semantics=sem, vmem_limit_bytes=VMEM_LIMIT_BYTES)


def _resident(shape):
    zeros = (0,) * len(shape)
    return pl.BlockSpec(shape, lambda *_: zeros)


def _rms(x, g):
    return x * lax.rsqrt(jnp.mean(x * x, axis=-1, keepdims=True) + EPS) * g


def _silu(x):
    return x * jax.nn.sigmoid(x)


def _dot(a, b):
    return jnp.dot(a, b, preferred_element_type=F32)


def _dot_nt(a, b):
    return lax.dot_general(a, b, (((1,), (1,)), ((), ())), preferred_element_type=F32)


def _split3(x):
    p1 = x.astype(BF16)
    r1 = x - p1.astype(F32)
    p2 = r1.astype(BF16)
    p3 = (r1 - p2.astype(F32)).astype(BF16)
    return p1, p2, p3


def _dot_sel_lhs(sel, x):
    p1, p2, p3 = _split3(x)
    return _dot(sel, p1) + _dot(sel, p2) + _dot(sel, p3)


def _dot_sel_rhs(x, sel):
    p1, p2, p3 = _split3(x)
    return _dot(p1, sel) + _dot(p2, sel) + _dot(p3, sel)


def _row_tile(m, want):
    t = min(m, want)
    assert m % t == 0, (m, t)
    return t


def _norm_matmul_kernel(x_ref, g_ref, *refs, n_w, chunk):
    w_refs, o_refs = refs[:n_w], refs[n_w:]
    xn = _rms(x_ref[...], g_ref[...]).astype(BF16)
    for w_ref, o_ref in zip(w_refs, o_refs):
        n = w_ref.shape[1]
        for c0 in range(0, n, chunk):
            c1 = min(c0 + chunk, n)
            o_ref[:, c0:c1] = _dot(xn, w_ref[:, c0:c1]).astype(o_ref.dtype)


def norm_matmul(x, g, ws, tm=256, chunk=512):
    m, k = x.shape
    tm = _row_tile(m, tm)
    return pl.pallas_call(
        functools.partial(_norm_matmul_kernel, n_w=len(ws), chunk=chunk),
        grid=(m // tm,),
        in_specs=[pl.BlockSpec((tm, k), lambda i: (i, 0)), _resident((1, k))]
        + [_resident(w.shape) for w in ws],
        out_specs=[pl.BlockSpec((tm, w.shape[1]), lambda i: (i, 0)) for w in ws],
        out_shape=[jax.ShapeDtypeStruct((m, w.shape[1]), F32) for w in ws],
        compiler_params=_params("parallel"),
        name="norm_matmul",
    )(x, g.reshape(1, k), *ws)


def _matmul_residual_kernel(a_ref, w_ref, h_ref, o_ref):
    o_ref[...] = h_ref[...] + _dot(a_ref[...].astype(BF16), w_ref[...])


def matmul_residual(a, w, h, tm=512):
    m, k = a.shape
    n = w.shape[1]
    tm = _row_tile(m, tm)
    return pl.pallas_call(
        _matmul_residual_kernel,
        grid=(m // tm,),
        in_specs=[pl.BlockSpec((tm, k), lambda i: (i, 0)), _resident(w.shape),
                  pl.BlockSpec((tm, n), lambda i: (i, 0))],
        out_specs=pl.BlockSpec((tm, n), lambda i: (i, 0)),
        out_shape=jax.ShapeDtypeStruct((m, n), F32),
        compiler_params=_params("parallel"),
        name="matmul_residual",
    )(a, w, h)


def _ffn_kernel(x_ref, g_ref, w1_ref, w2_ref, o_ref, xn_scr, acc_scr):
    j = pl.program_id(1)

    @pl.when(j == 0)
    def _():
        xn_scr[...] = _rms(x_ref[...], g_ref[...]).astype(BF16)
        acc_scr[...] = jnp.zeros_like(acc_scr)

    a = _dot(xn_scr[...], w1_ref[...])
    a = jnp.square(jnp.maximum(a, 0.0)).astype(BF16)
    acc_scr[...] += _dot(a, w2_ref[...])

    @pl.when(j == pl.num_programs(1) - 1)
    def _():
        o_ref[...] = x_ref[...] + acc_scr[...]


def ffn(x, g, w1, w2, tm=512, tf=1024):
    m, d = x.shape
    f = w1.shape[1]
    tm = _row_tile(m, tm)
    return pl.pallas_call(
        _ffn_kernel,
        grid=(m // tm, f // tf),
        in_specs=[pl.BlockSpec((tm, d), lambda i, j: (i, 0)), _resident((1, d)),
                  pl.BlockSpec((d, tf), lambda i, j: (0, j)),
                  pl.BlockSpec((tf, d), lambda i, j: (j, 0))],
        out_specs=pl.BlockSpec((tm, d), lambda i, j: (i, 0)),
        out_shape=jax.ShapeDtypeStruct((m, d), F32),
        scratch_shapes=[pltpu.VMEM((tm, d), BF16), pltpu.VMEM((tm, d), F32)],
        compiler_params=_params("parallel", "arbitrary"),
        name="ffn",
    )(x, g.reshape(1, d), w1, w2)


def _ple_kernel(x_ref, p_ref, g_ref, wg_ref, wp_ref, gf_ref, o_ref, *, final):
    x = x_ref[...]
    xn = _rms(x, g_ref[...]).astype(BF16)
    gate = jax.nn.sigmoid(_dot(xn, wg_ref[...]))
    y = x + gate * _dot(p_ref[...].astype(BF16), wp_ref[...])
    if final:
        y = _rms(y, gf_ref[...])
    o_ref[...] = y


def ple(x, p, g, wg, wp, g_final, final, tm=512):
    m, d = x.shape
    pd = p.shape[1]
    tm = _row_tile(m, tm)
    return pl.pallas_call(
        functools.partial(_ple_kernel, final=final),
        grid=(m // tm,),
        in_specs=[pl.BlockSpec((tm, d), lambda i: (i, 0)), pl.BlockSpec((tm, pd), lambda i: (i, 0)),
                  _resident((1, d)), _resident(wg.shape), _resident(wp.shape), _resident((1, d))],
        out_specs=pl.BlockSpec((tm, d), lambda i: (i, 0)),
        out_shape=jax.ShapeDtypeStruct((m, d), F32),
        compiler_params=_params("parallel"),
        name="ple",
    )(x, p, g.reshape(1, d), wg, wp, g_final.reshape(1, d))


def _conv_in_kernel(x_ref, g_ref, w_ref, o_ref):
    xn = _rms(x_ref[...], g_ref[...]).astype(BF16)
    d = o_ref.shape[1]
    o_ref[...] = _dot(xn, w_ref[:, :d]) * jax.nn.sigmoid(_dot(xn, w_ref[:, d:]))


def conv_in(x, g, w_in, tm=512):
    m, d = x.shape
    tm = _row_tile(m, tm)
    return pl.pallas_call(
        _conv_in_kernel,
        grid=(m // tm,),
        in_specs=[pl.BlockSpec((tm, d), lambda i: (i, 0)), _resident((1, d)), _resident(w_in.shape)],
        out_specs=pl.BlockSpec((tm, d), lambda i: (i, 0)),
        out_shape=jax.ShapeDtypeStruct((m, d), F32),
        compiler_params=_params("parallel"),
        name="conv_in",
    )(x, g.reshape(1, d), w_in)


def _ln_silu_out(c, h, lng_ref, lnb_ref, wo_ref):
    xc = c - jnp.mean(c, axis=-1, keepdims=True)
    y = xc * lax.rsqrt(jnp.mean(xc * xc, axis=-1, keepdims=True) + EPS) * lng_ref[...] + lnb_ref[...]
    return h + _dot(_silu(y).astype(BF16), wo_ref[...])


CONV_HALO = 32


def _conv_prompt_kernel(cur_ref, halo_ref, h_ref, wdw_ref, bdw_ref, lng_ref, lnb_ref, wo_ref, o_ref,
                        full_scr, c_scr, *, ts, width):
    i = pl.program_id(1)
    d = cur_ref.shape[2]
    full_scr[0:CONV_HALO] = jnp.where(i > 0, halo_ref[0], 0.0)
    full_scr[CONV_HALO:CONV_HALO + ts] = cur_ref[0]
    off = CONV_HALO - (width - 1)
    for c in range(d // LANES):
        cs = slice(c * LANES, (c + 1) * LANES)
        acc = jnp.broadcast_to(bdw_ref[:, cs], (ts, LANES))
        for k in range(width):
            acc = acc + full_scr[off + k:off + k + ts, cs] * wdw_ref[k:k + 1, cs]
        c_scr[:, cs] = acc
    o_ref[0] = _ln_silu_out(c_scr[...], h_ref[0], lng_ref, lnb_ref, wo_ref)


def conv_core_prompt(glu, h, w_dw, b_dw, ln_g, ln_b, w_out, ts=256):
    b, s, d = glu.shape
    ts = _row_tile(s, ts)
    width = w_dw.shape[0]
    assert width - 1 <= CONV_HALO and ts % CONV_HALO == 0
    per = ts // CONV_HALO
    vec = lambda v: v.reshape(1, d)
    return pl.pallas_call(
        functools.partial(_conv_prompt_kernel, ts=ts, width=width),
        grid=(b, s // ts),
        in_specs=[pl.BlockSpec((1, ts, d), lambda bi, i: (bi, i, 0)),
                  pl.BlockSpec((1, CONV_HALO, d), lambda bi, i: (bi, jnp.maximum(i * per - 1, 0), 0)),
                  pl.BlockSpec((1, ts, d), lambda bi, i: (bi, i, 0)),
                  _resident(w_dw.shape), _resident((1, d)), _resident((1, d)), _resident((1, d)),
                  _resident(w_out.shape)],
        out_specs=pl.BlockSpec((1, ts, d), lambda bi, i: (bi, i, 0)),
        out_shape=jax.ShapeDtypeStruct((b, s, d), F32),
        scratch_shapes=[pltpu.VMEM((CONV_HALO + ts, d), F32), pltpu.VMEM((ts, d), F32)],
        compiler_params=_params("parallel", "arbitrary"),
        name="conv_core_prompt",
    )(glu, glu, h, w_dw, vec(b_dw), vec(ln_g), vec(ln_b), w_out)


def _conv_sample_kernel(glu_ref, st_ref, h_ref, wdw_ref, bdw_ref, lng_ref, lnb_ref, wo_ref, o_ref, nst_ref,
                        full_scr, c_scr, *, bb, t, width):
    hist = width - 1
    d = glu_ref.shape[1]
    for bi in range(bb):
        full_scr[0:hist] = st_ref[bi]
        full_scr[hist:hist + t] = glu_ref[bi * t:(bi + 1) * t, :]
        acc = jnp.broadcast_to(bdw_ref[...], (t, d))
        for k in range(width):
            acc = acc + full_scr[k:k + t, :] * wdw_ref[k:k + 1, :]
        c_scr[bi * t:(bi + 1) * t, :] = acc
        nst_ref[bi] = full_scr[t:t + hist]
    o_ref[...] = _ln_silu_out(c_scr[...], h_ref[...], lng_ref, lnb_ref, wo_ref)


def conv_core_sample(glu, state, h, w_dw, b_dw, ln_g, ln_b, w_out, bb=8):
    nb, hist, d = state.shape
    t = glu.shape[0] // nb
    width = w_dw.shape[0]
    assert hist == width - 1 and nb % bb == 0
    vec = lambda v: v.reshape(1, d)
    return pl.pallas_call(
        functools.partial(_conv_sample_kernel, bb=bb, t=t, width=width),
        grid=(nb // bb,),
        in_specs=[pl.BlockSpec((bb * t, d), lambda i: (i, 0)),
                  pl.BlockSpec((bb, hist, d), lambda i: (i, 0, 0)),
                  pl.BlockSpec((bb * t, d), lambda i: (i, 0)),
                  _resident(w_dw.shape), _resident((1, d)), _resident((1, d)), _resident((1, d)),
                  _resident(w_out.shape)],
        out_specs=[pl.BlockSpec((bb * t, d), lambda i: (i, 0)),
                   pl.BlockSpec((bb, hist, d), lambda i: (i, 0, 0))],
        out_shape=[jax.ShapeDtypeStruct((nb * t, d), F32), jax.ShapeDtypeStruct((nb, hist, d), F32)],
        scratch_shapes=[pltpu.VMEM((hist + t + SUBLANES, d), F32), pltpu.VMEM((bb * t, d), F32)],
        compiler_params=_params("parallel"),
        name="conv_core_sample",
    )(glu, state, h, w_dw, vec(b_dw), vec(ln_g), vec(ln_b), w_out)


MLA_QK = MLA_KV_RANK + MLA_ROPE


def _mla_in_kernel(x_ref, g_ref, wdq_ref, gq_ref, wuq_ref, wdkv_ref, gkv_ref, wuk_ref, cos_ref, sin_ref,
                   q_ref, kcat_ref, ckv_ref, kpe_ref):
    nope_w = MLA_HEADS * MLA_NOPE
    rope_w = MLA_HEADS * MLA_ROPE
    xn = _rms(x_ref[...], g_ref[...]).astype(BF16)
    cq = _rms(_dot(xn, wdq_ref[...]), gq_ref[...]).astype(BF16)
    kv = _dot(xn, wdkv_ref[...])
    ckv = _rms(kv[:, :MLA_KV_RANK], gkv_ref[...])
    cos, sin = cos_ref[...], sin_ref[...]
    kpe = (kv[:, MLA_KV_RANK:MLA_QK] * cos[:, :MLA_ROPE]
           + kv[:, MLA_QK:MLA_QK + MLA_ROPE] * sin[:, :MLA_ROPE])
    ckv_ref[...] = ckv
    kpe_ref[...] = kpe
    kcat_ref[:, :MLA_KV_RANK] = ckv.astype(BF16)
    kcat_ref[:, MLA_KV_RANK:] = kpe.astype(BF16)
    qp = _dot(cq, wuq_ref[:, nope_w:nope_w + rope_w])
    qps = _dot(cq, wuq_ref[:, nope_w + rope_w:])
    per = LANES // MLA_ROPE
    for c in range(rope_w // LANES):
        cs = slice(c * LANES, (c + 1) * LANES)
        roped = ((qp[:, cs] * cos + qps[:, cs] * sin) * MLA_SCALE).astype(BF16)
        for hh in range(per):
            q_ref[c * per + hh, :, MLA_KV_RANK:] = roped[:, hh * MLA_ROPE:(hh + 1) * MLA_ROPE]
    for h in range(MLA_HEADS):
        qn = _dot(cq, wuq_ref[:, h * MLA_NOPE:(h + 1) * MLA_NOPE]).astype(BF16)
        q_ref[h, :, :MLA_KV_RANK] = (_dot(qn, wuk_ref[h]) * MLA_SCALE).astype(BF16)


def mla_in(x, g, w, cos_tab, sin_tab, tab_index, tm=256):
    m, d = x.shape
    tm = _row_tile(m, tm)
    assert cos_tab.shape[0] % tm == 0 or cos_tab.shape[0] == tm
    return pl.pallas_call(
        _mla_in_kernel,
        grid=(m // tm,),
        in_specs=[pl.BlockSpec((tm, d), lambda i: (i, 0)), _resident((1, d)),
                  _resident(w["dq"].shape), _resident((1, w["dq"].shape[1])), _resident(w["uq"].shape),
                  _resident(w["dkv"].shape), _resident((1, MLA_KV_RANK)), _resident(w["uk"].shape),
                  pl.BlockSpec((tm, LANES), lambda i: (tab_index(i), 0)),
                  pl.BlockSpec((tm, LANES), lambda i: (tab_index(i), 0))],
        out_specs=[pl.BlockSpec((MLA_HEADS, tm, MLA_QK), lambda i: (0, i, 0)),
                   pl.BlockSpec((tm, MLA_QK), lambda i: (i, 0)),
                   pl.BlockSpec((tm, MLA_KV_RANK), lambda i: (i, 0)),
                   pl.BlockSpec((tm, MLA_ROPE), lambda i: (i, 0))],
        out_shape=[jax.ShapeDtypeStruct((MLA_HEADS, m, MLA_QK), BF16),
                   jax.ShapeDtypeStruct((m, MLA_QK), BF16),
                   jax.ShapeDtypeStruct((m, MLA_KV_RANK), F32),
                   jax.ShapeDtypeStruct((m, MLA_ROPE), F32)],
        compiler_params=_params("parallel"),
        name="mla_in",
    )(x, g.reshape(1, d), w["dq"], w["gq"].reshape(1, -1), w["uq"], w["dkv"], w["gkv"].reshape(1, -1),
      w["uk"], cos_tab, sin_tab)


def _mla_attn_kernel(q_ref, k_ref, o_ref, m_scr, l_scr, acc_scr, *, tq, tk):
    i, j = pl.program_id(1), pl.program_id(2)
    nh = q_ref.shape[0]
    last_j = ((i + 1) * tq - 1) // tk

    @pl.when(j == 0)
    def _():
        m_scr[...] = jnp.full_like(m_scr, -jnp.inf)
        l_scr[...] = jnp.zeros_like(l_scr)
        acc_scr[...] = jnp.zeros_like(acc_scr)

    def step(masked):
        q = q_ref[...].reshape(nh * tq, MLA_QK)
        k = k_ref[...]
        s = _dot_nt(q, k)
        if masked:
            row = lax.broadcasted_iota(jnp.int32, s.shape, 0) & (tq - 1)
            col = lax.broadcasted_iota(jnp.int32, s.shape, 1)
            s = jnp.where(col + j * tk <= row + i * tq, s, -jnp.inf)
        m_prev = m_scr[...]
        m_new = jnp.maximum(m_prev, jnp.max(s, axis=-1, keepdims=True))
        alpha = jnp.exp(m_prev - m_new)
        p = jnp.exp(s - m_new)
        l_scr[...] = alpha * l_scr[...] + jnp.sum(p, axis=-1, keepdims=True)
        acc_scr[...] = alpha * acc_scr[...] + _dot(p.astype(BF16), k[:, :MLA_KV_RANK])
        m_scr[...] = m_new

    crosses_diagonal = j * tk + tk - 1 > i * tq

    @pl.when((j <= last_j) & jnp.logical_not(crosses_diagonal))
    def _():
        step(False)

    @pl.when((j <= last_j) & crosses_diagonal)
    def _():
        step(True)

    @pl.when(j == last_j)
    def _():
        o = acc_scr[...] / l_scr[...]
        o_ref[...] = o.reshape(nh, tq, MLA_KV_RANK).astype(o_ref.dtype)


def mla_attn_prompt(q, kcat, b, s, tq=256, tk=512):
    nh = q.shape[0]
    tq, tk = _row_tile(s, tq), _row_tile(s, tk)
    assert tq & (tq - 1) == 0
    nq, nk = s // tq, s // tk

    def k_index(bi, i, j):
        return (bi * nk + jnp.minimum(j, ((i + 1) * tq - 1) // tk), 0)

    return pl.pallas_call(
        functools.partial(_mla_attn_kernel, tq=tq, tk=tk),
        grid=(b, nq, nk),
        in_specs=[pl.BlockSpec((nh, tq, MLA_QK), lambda bi, i, j: (0, bi * nq + i, 0)),
                  pl.BlockSpec((tk, MLA_QK), k_index)],
        out_specs=pl.BlockSpec((nh, tq, MLA_KV_RANK), lambda bi, i, j: (0, bi * nq + i, 0)),
        out_shape=jax.ShapeDtypeStruct((nh, b * s, MLA_KV_RANK), BF16),
        scratch_shapes=[pltpu.VMEM((nh * tq, 1), F32), pltpu.VMEM((nh * tq, 1), F32),
                        pltpu.VMEM((nh * tq, MLA_KV_RANK), F32)],
        compiler_params=_params("parallel", "parallel", "arbitrary"),
        name="mla_attn_prompt",
    )(q, kcat)


def _mla_sample_kernel(pt_ref, q_ref, knew_ref, *refs, pp, t):
    del pt_ref
    ckv_refs, kpe_refs = refs[:pp], refs[pp:2 * pp]
    o_ref, kc_scr, m_scr, l_scr, acc_scr = refs[2 * pp:]
    j = pl.program_id(1)
    page = ckv_refs[0].shape[1]

    @pl.when(j == 0)
    def _():
        m_scr[...] = jnp.full_like(m_scr, -jnp.inf)
        l_scr[...] = jnp.zeros_like(l_scr)
        acc_scr[...] = jnp.zeros_like(acc_scr)

    for p in range(pp):
        kc_scr[p * page:(p + 1) * page, :MLA_KV_RANK] = ckv_refs[p][0].astype(BF16)
        kc_scr[p * page:(p + 1) * page, MLA_KV_RANK:] = kpe_refs[p][0].astype(BF16)
    q = q_ref[0]
    k = kc_scr[...]
    s = _dot_nt(q, k)
    m_prev = m_scr[...]
    m_new = jnp.maximum(m_prev, jnp.max(s, axis=-1, keepdims=True))
    alpha = jnp.exp(m_prev - m_new)
    p = jnp.exp(s - m_new)
    l_scr[...] = alpha * l_scr[...] + jnp.sum(p, axis=-1, keepdims=True)
    acc_scr[...] = alpha * acc_scr[...] + _dot(p.astype(BF16), k[:, :MLA_KV_RANK])
    m_scr[...] = m_new

    @pl.when(j == pl.num_programs(1) - 1)
    def _():
        qf = q.astype(F32)
        kn = knew_ref[0].astype(F32)
        row_t = lax.broadcasted_iota(jnp.int32, (q.shape[0], 1), 0) % t
        m, l, acc = m_scr[...], l_scr[...], acc_scr[...]
        for tk in range(t):
            krow = kn[tk:tk + 1, :]
            s_t = jnp.where(row_t >= tk, jnp.sum(qf * krow, axis=-1, keepdims=True), -jnp.inf)
            m_new = jnp.maximum(m, s_t)
            alpha = jnp.exp(m - m_new)
            p_t = jnp.exp(s_t - m_new)
            l = alpha * l + p_t
            acc = alpha * acc + p_t * krow[:, :MLA_KV_RANK]
            m = m_new
        o_ref[0] = (acc / l).astype(o_ref.dtype)


def mla_attn_sample(q, knew, ckv_pool, kpe_pool, page_table, pp=16):
    nb, rows, _ = q.shape
    t = knew.shape[1]
    n_pages = page_table.shape[1]
    page = ckv_pool.shape[1]
    pp = min(pp, n_pages)
    assert n_pages % pp == 0

    def pool_spec(width, p):
        return pl.BlockSpec((1, page, width), lambda bi, j, pt: (pt[bi * n_pages + j * pp + p], 0, 0))

    grid_spec = pltpu.PrefetchScalarGridSpec(
        num_scalar_prefetch=1,
        grid=(nb, n_pages // pp),
        in_specs=[pl.BlockSpec((1, rows, MLA_QK), lambda bi, j, pt: (bi, 0, 0)),
                  pl.BlockSpec((1, t, MLA_QK), lambda bi, j, pt: (bi, 0, 0))]
        + [pool_spec(MLA_KV_RANK, p) for p in range(pp)]
        + [pool_spec(MLA_ROPE, p) for p in range(pp)],
        out_specs=pl.BlockSpec((1, rows, MLA_KV_RANK), lambda bi, j, pt: (bi, 0, 0)),
        scratch_shapes=[pltpu.VMEM((pp * page, MLA_QK), BF16), pltpu.VMEM((rows, 1), F32),
                        pltpu.VMEM((rows, 1), F32), pltpu.VMEM((rows, MLA_KV_RANK), F32)],
    )
    return pl.pallas_call(
        functools.partial(_mla_sample_kernel, pp=pp, t=t),
        grid_spec=grid_spec,
        out_shape=jax.ShapeDtypeStruct((nb, rows, MLA_KV_RANK), BF16),
        compiler_params=_params("parallel", "arbitrary"),
        name="mla_attn_sample",
    )(page_table.reshape(-1), q, knew, *([ckv_pool] * pp), *([kpe_pool] * pp))


def _mla_out_kernel(o_ref, wuv_ref, wo_ref, h_ref, y_ref):
    parts = [_dot(o_ref[h], wuv_ref[h]).astype(BF16) for h in range(o_ref.shape[0])]
    y_ref[...] = h_ref[...] + _dot(jnp.concatenate(parts, axis=-1), wo_ref[...])


def mla_out(o, w_uv, w_o, h, tm=512):
    nh, m, c = o.shape
    d = h.shape[1]
    tm = _row_tile(m, tm)
    return pl.pallas_call(
        _mla_out_kernel,
        grid=(m // tm,),
        in_specs=[pl.BlockSpec((nh, tm, c), lambda i: (0, i, 0)), _resident(w_uv.shape), _resident(w_o.shape),
                  pl.BlockSpec((tm, d), lambda i: (i, 0))],
        out_specs=pl.BlockSpec((tm, d), lambda i: (i, 0)),
        out_shape=jax.ShapeDtypeStruct((m, d), F32),
        compiler_params=_params("parallel"),
        name="mla_out",
    )(o, w_uv, w_o, h)


SSD_TAIL = SUBLANES


def _ssd_kernel(xbc_ref, z_ref, dt_ref, h0_ref, hist_ref, wc_ref, bc_ref, dtb_ref, alog_ref, dexp_ref, gn_ref,
                e_ref, et_ref, y_ref, hfin_ref, state_scr, tail_scr, full_scr, z_scr, dt_scr, y_scr,
                *, q, qb, nh, hd, ng, ns):
    c = pl.program_id(1)
    inner = nh * hd
    hpg = nh // ng
    gw = hpg * hd

    @pl.when(c == 0)
    def _():
        state_scr[...] = h0_ref[0]
        tail_scr[...] = hist_ref[0]

    full_scr[0:SSD_TAIL] = tail_scr[...]
    full_scr[SSD_TAIL:SSD_TAIL + qb] = xbc_ref[0]
    if qb < q:
        full_scr[SSD_TAIL + qb:SSD_TAIL + q] = jnp.zeros((q - qb, full_scr.shape[1]), F32)
        z_scr[0:qb] = z_ref[0]
        z_scr[qb:q] = jnp.zeros((q - qb, inner), F32)
        dt_scr[0:qb] = dt_ref[0]
        dt_scr[qb:q] = jnp.zeros((q - qb, LANES), F32)
        z, dt_raw = z_scr[...], dt_scr[...]
    else:
        tail_scr[...] = full_scr[q:q + SSD_TAIL]
        z, dt_raw = z_ref[0], dt_ref[0]

    off = SSD_TAIL - (SSD_CONV - 1)
    conv = bc_ref[...] + full_scr[off:off + q] * wc_ref[0:1]
    for k in range(1, SSD_CONV):
        conv = conv + full_scr[off + k:off + k + q] * wc_ref[k:k + 1]
    xc = _silu(conv)
    dt = jnp.maximum(dt_raw + dtb_ref[...], 0.0) + jnp.log1p(jnp.exp(-jnp.abs(dt_raw + dtb_ref[...])))
    if qb < q:
        live = lax.broadcasted_iota(jnp.int32, (q, 1), 0) < qb
        xc = jnp.where(live, xc, 0.0)
        dt = jnp.where(live, dt, 0.0)

    la = dt * (-jnp.exp(alog_ref[...]))
    ri = lax.broadcasted_iota(jnp.int32, (q, q), 0)
    ci = lax.broadcasted_iota(jnp.int32, (q, q), 1)
    causal = ri >= ci
    cs = _dot_sel_lhs(causal.astype(BF16), la)
    cs_t = cs.T
    cs_last = cs[q - 1:q, :]
    per_head = jnp.concatenate([dt, jnp.exp(cs), jnp.exp(cs_last - cs)], axis=0)
    spread = _dot_sel_rhs(per_head, e_ref[...])
    dt_x, ecs_x, edec_x = spread[0:q], spread[q:2 * q], spread[2 * q:3 * q]
    chunk_decay = jnp.broadcast_to(jnp.exp(cs_t[:, q - 1:q]), (LANES, LANES))
    decay_rows = _dot_sel_lhs(et_ref[...], chunk_decay)

    xs = xc[:, :inner]
    xdt = xs * dt_x
    xw = xdt * edec_x
    for g in range(ng):
        bg = xc[:, inner + g * ns:inner + (g + 1) * ns].astype(BF16)
        cg = xc[:, inner + ng * ns + g * ns:inner + ng * ns + (g + 1) * ns].astype(BF16)
        cb = _dot_nt(cg, bg)
        rows = slice(g * gw, (g + 1) * gw)
        st = state_scr[rows, :]
        y_scr[:, rows] = _dot_nt(cg, st.astype(BF16)) * ecs_x[:, rows]
        state_scr[rows, :] = decay_rows[rows, :] * st + _dot(xw[:, rows].T.astype(BF16), bg)
        for r in range(hpg):
            h = g * hpg + r
            hs = slice(h * hd, (h + 1) * hd)
            seg = cs[:, h:h + 1] - cs_t[h:h + 1, :]
            mat = (cb * jnp.exp(jnp.where(causal, seg, -jnp.inf))).astype(BF16)
            y_scr[:, hs] += _dot(mat, xdt[:, hs].astype(BF16))

    gated = (y_scr[...] + xs * dexp_ref[...]) * _silu(z)
    normed = []
    for g in range(ng):
        grp = gated[:, g * gw:(g + 1) * gw]
        normed.append(grp * lax.rsqrt(jnp.mean(grp * grp, axis=-1, keepdims=True) + EPS))
    out = jnp.concatenate(normed, axis=-1) * gn_ref[...]
    y_ref[0] = out[0:qb]

    @pl.when(c == pl.num_programs(1) - 1)
    def _():
        hfin_ref[0] = state_scr[...]


def ssd_core(xbc, z, dt, h0, hist, w_conv, b_conv, dt_bias, a_log, d_skip, g_norm):
    b, l, conv_dim = xbc.shape
    inner = z.shape[2]
    nh = inner // SSD_HEAD_DIM
    q = SSD_CHUNK
    qb = min(l, q)
    assert l % qb == 0 and (qb == q or l == qb)
    nc = l // qb
    hp = nh * SSD_HEAD_DIM
    head_of_lane = jnp.arange(inner) // SSD_HEAD_DIM
    e = (jnp.arange(LANES)[:, None] == head_of_lane[None, :]).astype(BF16)
    pad = lambda v: jnp.pad(v.astype(F32), (0, LANES - nh)).reshape(1, LANES)
    row = lambda v: v.astype(F32).reshape(1, -1)
    blk = lambda w: pl.BlockSpec((1, qb, w), lambda bi, ci: (bi, ci, 0))
    per_b = lambda r, w: pl.BlockSpec((1, r, w), lambda bi, ci: (bi, 0, 0))
    return pl.pallas_call(
        functools.partial(_ssd_kernel, q=q, qb=qb, nh=nh, hd=SSD_HEAD_DIM, ng=SSD_GROUPS, ns=SSD_STATE),
        grid=(b, nc),
        in_specs=[blk(conv_dim), blk(inner), blk(LANES), per_b(hp, SSD_STATE), per_b(SSD_TAIL, conv_dim),
                  _resident((SSD_CONV, conv_dim)), _resident((1, conv_dim)), _resident((1, LANES)),
                  _resident((1, LANES)), _resident((1, inner)), _resident((1, inner)),
                  _resident((LANES, inner)), _resident((inner, LANES))],
        out_specs=[blk(inner), per_b(hp, SSD_STATE)],
        out_shape=[jax.ShapeDtypeStruct((b, l, inner), F32), jax.ShapeDtypeStruct((b, hp, SSD_STATE), F32)],
        scratch_shapes=[pltpu.VMEM((hp, SSD_STATE), F32), pltpu.VMEM((SSD_TAIL, conv_dim), F32),
                        pltpu.VMEM((SSD_TAIL + q, conv_dim), F32), pltpu.VMEM((q, inner), F32),
                        pltpu.VMEM((q, LANES), F32), pltpu.VMEM((q, inner), F32)],
        compiler_params=_params("parallel", "arbitrary"),
        name="ssd_core",
    )(xbc, z, dt, h0, hist, w_conv.astype(F32), row(b_conv), pad(dt_bias), pad(a_log),
      row(jnp.repeat(d_skip, SSD_HEAD_DIM)), row(g_norm), e, e.T)


def _t5_bucket(dist):
    max_exact = REL_BUCKETS // 2
    d = jnp.maximum(dist, 1).astype(F32)
    large = max_exact + (jnp.log(d / max_exact) / math.log(REL_MAX_DIST / max_exact)
                         * (REL_BUCKETS - max_exact)).astype(jnp.int32)
    return jnp.where(dist < max_exact, dist, jnp.minimum(large, REL_BUCKETS - 1))


def _group_bias(rel_bias, g, r, nk):
    tab = rel_bias[_t5_bucket(r * jnp.arange(nk + 1))]
    return tab[:, g * DIL_HEADS_PER_GROUP:(g + 1) * DIL_HEADS_PER_GROUP].T.astype(F32)


def _dil_prompt_kernel(q_ref, kc_ref, kp_ref, vc_ref, vp_ref, bias_ref, o_ref, lse_ref, *, blk, nh, hd):
    i = pl.program_id(1)
    q = q_ref[0].astype(BF16)
    k = jnp.concatenate([kp_ref[0], kc_ref[0]], axis=0).astype(BF16)
    v = jnp.concatenate([vp_ref[0], vc_ref[0]], axis=0).astype(BF16)
    qi = lax.broadcasted_iota(jnp.int32, (blk, 2 * blk), 0)
    ki = lax.broadcasted_iota(jnp.int32, (blk, 2 * blk), 1)
    dm = qi + blk - ki
    valid = (dm >= 0) & (dm <= blk) & ((ki >= blk) | (i > 0))
    for h in range(nh):
        hs = slice(h * hd, (h + 1) * hd)
        s = _dot_nt(q[:, hs], k[:, hs]) * DIL_SCALE + bias_ref[h]
        s = jnp.where(valid, s, -jnp.inf)
        m = jnp.max(s, axis=-1, keepdims=True)
        p = jnp.exp(s - m)
        l = jnp.sum(p, axis=-1, keepdims=True)
        o_ref[0, :, hs] = _dot(p.astype(BF16), v[:, hs]) / l
        lse_ref[0, :, hs] = jnp.broadcast_to(m + jnp.log(l), (blk, hd))


def dil_attn_prompt(q, k, v, bias_mat, blk):
    nseq, l, w = q.shape
    nh = bias_mat.shape[0]
    assert l % blk == 0
    cur = pl.BlockSpec((1, blk, w), lambda si, i: (si, i, 0))
    prev = pl.BlockSpec((1, blk, w), lambda si, i: (si, jnp.maximum(i - 1, 0), 0))
    return pl.pallas_call(
        functools.partial(_dil_prompt_kernel, blk=blk, nh=nh, hd=w // nh),
        grid=(nseq, l // blk),
        in_specs=[cur, cur, prev, cur, prev, _resident(bias_mat.shape)],
        out_specs=[cur, cur],
        out_shape=[jax.ShapeDtypeStruct((nseq, l, w), F32)] * 2,
        compiler_params=_params("parallel", "arbitrary"),
        name="dil_attn_prompt",
    )(q, k, k, v, v, bias_mat)


DIL_PAD_ROWS = SUBLANES


def _dil_sample_kernel(buf_ref, q_ref, k_ref, v_ref, bias_ref, seg_ref, nbuf_ref, o_ref, lse_ref, keys_scr,
                       *, r, nk, t):
    n_rows, kvw = keys_scr.shape
    half = kvw // 2

    def new_row(n):
        return jnp.concatenate([k_ref[0, n:n + 1, :], v_ref[0, n:n + 1, :]], axis=-1)

    def rows_from(first, dst_ref, dst_cols, n_new):
        shift, block = divmod(first, r)
        src_cols = slice(block * kvw, (block + 1) * kvw)
        dst_ref[0:nk - shift, dst_cols] = buf_ref[0, shift:nk, src_cols]
        for i in range(n_new):
            dst_ref[nk - shift + i:nk - shift + i + 1, dst_cols] = new_row(r * i + block)
        return shift

    for c in range(r):
        rows_from(c + t, nbuf_ref.at[0], slice(c * kvw, (c + 1) * kvw), (c + t) // r)

    q = q_ref[0]
    e = lax.broadcasted_iota(jnp.int32, (n_rows, 1), 0)
    keys_scr[nk:, :] = jnp.zeros((n_rows - nk, kvw), F32)
    for tt in range(t):
        rows_from(tt, keys_scr, slice(0, kvw), tt // r + 1)
        rows = keys_scr[...]
        prod = (rows[:, :half] * q[tt:tt + 1, :]).astype(BF16)
        s = _dot(prod, seg_ref[...]) * DIL_SCALE + bias_ref[...]
        s = jnp.where(e <= nk, s, -jnp.inf)
        m = jnp.max(s, axis=0, keepdims=True)
        p = jnp.exp(s - m)
        l = jnp.sum(p, axis=0, keepdims=True)
        o_ref[0, tt:tt + 1, :] = jnp.sum(p * rows[:, half:], axis=0, keepdims=True) / l
        lse_ref[0, tt:tt + 1, :] = m + jnp.log(l)


def dil_attn_sample(buf, qkv, g, bias, r, nk):
    nb, w, kvw = buf.shape
    half = kvw // 2
    t = qkv.shape[1]
    n_groups = qkv.shape[2] // (3 * half)
    assert w == r * nk, "the buffer holds exactly one window"
    n_rows = nk + DIL_PAD_ROWS
    hd = half // bias.shape[0]
    bias_rows = jnp.repeat(bias[:, ::-1].T, hd, axis=1)
    bias_rows = jnp.pad(bias_rows, ((0, n_rows - nk - 1), (0, 0)))
    lane_head = jnp.arange(half) // hd
    seg = (lane_head[:, None] == lane_head[None, :]).astype(BF16)
    col = lambda c: pl.BlockSpec((1, t, half), lambda bi: (bi, 0, c))
    view = pl.BlockSpec((1, nk, r * kvw), lambda bi: (bi, 0, 0))
    nbuf, o, lse = pl.pallas_call(
        functools.partial(_dil_sample_kernel, r=r, nk=nk, t=t),
        grid=(nb,),
        in_specs=[view, col(g), col(n_groups + g), col(2 * n_groups + g),
                  _resident(bias_rows.shape), _resident(seg.shape)],
        out_specs=[view, pl.BlockSpec((1, t, half), lambda bi: (bi, 0, 0)),
                   pl.BlockSpec((1, t, half), lambda bi: (bi, 0, 0))],
        out_shape=[jax.ShapeDtypeStruct((nb, nk, r * kvw), F32), jax.ShapeDtypeStruct((nb, t, half), F32),
                   jax.ShapeDtypeStruct((nb, t, half), F32)],
        scratch_shapes=[pltpu.VMEM((n_rows, kvw), F32)],
        compiler_params=_params("parallel"),
        name="dil_attn_sample",
    )(buf.reshape(nb, nk, r * kvw), qkv, qkv, qkv, bias_rows, seg)
    return nbuf.reshape(nb, w, kvw), o, lse


def _dil_combine_kernel(*refs, ng):
    o_refs, l_refs = refs[:ng], refs[ng:2 * ng]
    wo_ref, h_ref, y_ref = refs[2 * ng:]
    lses = [l_ref[...] for l_ref in l_refs]
    m = functools.reduce(jnp.maximum, lses)
    es = [jnp.exp(l - m) for l in lses]
    tot = functools.reduce(lambda a, b: a + b, es)
    o = functools.reduce(lambda a, b: a + b, [(e / tot) * o_ref[...] for e, o_ref in zip(es, o_refs)])
    y_ref[...] = h_ref[...] + _dot(o.astype(BF16), wo_ref[...])


def dil_combine_out(outs, lses, w_o, h, tm=512):
    m, w = outs[0].shape
    d = h.shape[1]
    tm = _row_tile(m, tm)
    ng = len(outs)
    rows = lambda width: pl.BlockSpec((tm, width), lambda i: (i, 0))
    return pl.pallas_call(
        functools.partial(_dil_combine_kernel, ng=ng),
        grid=(m // tm,),
        in_specs=[rows(w)] * (2 * ng) + [_resident(w_o.shape), rows(d)],
        out_specs=rows(d),
        out_shape=jax.ShapeDtypeStruct((m, d), F32),
        compiler_params=_params("parallel"),
        name="dil_combine_out",
    )(*outs, *lses, w_o, h)


def _conv_layer(hp, hs, g_mix, state, w_in, w_dw, b_dw, ln_g, ln_b, w_out, bp, s):
    d = hp.shape[1]
    w_in, w_out = w_in.astype(BF16), w_out.astype(BF16)
    hist = w_dw.shape[0] - 1
    glu_p = conv_in(hp, g_mix, w_in)
    glu_s = conv_in(hs, g_mix, w_in)
    glu_p3 = glu_p.reshape(bp, s, d)
    new_hp = conv_core_prompt(glu_p3, hp.reshape(bp, s, d), w_dw, b_dw, ln_g, ln_b, w_out).reshape(bp * s, d)
    new_hs, st_s = conv_core_sample(glu_s, state, hs, w_dw, b_dw, ln_g, ln_b, w_out)
    st_p = jnp.concatenate([jnp.zeros((bp, hist, d), F32), glu_p3], axis=1)[:, s:]
    return new_hp, new_hs, st_p, st_s


def _rope_tables(pos):
    inv = ROPE_THETA ** (-jnp.arange(0, MLA_ROPE, 2, dtype=F32) / MLA_ROPE)
    ang = pos.astype(F32)[:, None] * inv[None, :]
    cos, sin = jnp.cos(ang), jnp.sin(ang)
    reps = LANES // MLA_ROPE
    return jnp.tile(jnp.concatenate([cos, cos], axis=1), (1, reps)), jnp.tile(jnp.concatenate([-sin, sin], axis=1), (1, reps))


def _swap_halves(w):
    k = w.shape[0]
    w4 = w.reshape(k, -1, 2, MLA_ROPE // 2)
    return w4[:, :, ::-1, :].reshape(k, -1)


def _mla_layer(hp, hs, g_mix, ckv_pool, kpe_pool, page_table, w_dq, g_q, w_uq, w_dkv, g_kv, w_uk, w_uv, w_o, bp, s):
    nb = page_table.shape[0]
    t = hs.shape[0] // nb
    past = page_table.shape[1] * ckv_pool.shape[1]
    qr = w_dq.shape[1]
    uq = w_uq.reshape(qr, MLA_HEADS, MLA_NOPE + MLA_ROPE)
    uq_nope = uq[:, :, :MLA_NOPE].reshape(qr, -1)
    uq_rope = uq[:, :, MLA_NOPE:].reshape(qr, -1)
    dkv_rope = w_dkv[:, MLA_KV_RANK:]
    w = {"dq": w_dq.astype(BF16), "gq": g_q,
         "uq": jnp.concatenate([uq_nope, uq_rope, _swap_halves(uq_rope)], axis=1).astype(BF16),
         "dkv": jnp.concatenate([w_dkv[:, :MLA_KV_RANK], dkv_rope, _swap_halves(dkv_rope)], axis=1).astype(BF16),
         "gkv": g_kv, "uk": w_uk.astype(BF16)}
    w_uv, w_o = w_uv.astype(BF16), w_o.astype(BF16)

    tm_p = _row_tile(s, 256)
    cos_p, sin_p = _rope_tables(jnp.arange(s))
    q_p, kcat_p, ckv_p, kpe_p = mla_in(hp, g_mix, w, cos_p, sin_p, lambda i: i % (s // tm_p), tm=tm_p)
    o_p = mla_attn_prompt(q_p, kcat_p, bp, s)
    new_hp = mla_out(o_p, w_uv, w_o, hp)

    ms = hs.shape[0]
    tm_s = _row_tile(ms, 256)
    assert tm_s % t == 0
    cos_s, sin_s = _rope_tables(past + jnp.arange(t))
    cos_s, sin_s = jnp.tile(cos_s, (tm_s // t, 1)), jnp.tile(sin_s, (tm_s // t, 1))
    q_s, kcat_s, ckv_s, kpe_s = mla_in(hs, g_mix, w, cos_s, sin_s, lambda i: 0, tm=tm_s)
    q_rows = q_s.reshape(MLA_HEADS, nb, t, MLA_QK).transpose(1, 0, 2, 3).reshape(nb, MLA_HEADS * t, MLA_QK)
    o_s = mla_attn_sample(q_rows, kcat_s.reshape(nb, t, MLA_QK), ckv_pool, kpe_pool, page_table)
    o_s = o_s.reshape(nb, MLA_HEADS, t, MLA_KV_RANK).transpose(1, 0, 2, 3).reshape(MLA_HEADS, ms, MLA_KV_RANK)
    new_hs = mla_out(o_s, w_uv, w_o, hs)
    return (new_hp, new_hs, ckv_p.reshape(bp, s, -1), kpe_p.reshape(bp, s, -1),
            ckv_s.reshape(nb, t, -1), kpe_s.reshape(nb, t, -1))


def _ssd_layer(hp, hs, g_mix, conv_state, ssm_state, w_in, w_conv, b_conv, dt_bias, a_log, d_skip, g_norm, w_out,
               bp, s):
    nb = conv_state.shape[0]
    t = hs.shape[0] // nb
    inner = g_norm.shape[0]
    conv_dim = w_conv.shape[1]
    nh = dt_bias.shape[0]
    hist = SSD_CONV - 1
    ws = [w_in[:, :inner].astype(BF16), w_in[:, inner:inner + conv_dim].astype(BF16),
          jnp.pad(w_in[:, inner + conv_dim:], ((0, 0), (0, LANES - nh))).astype(BF16)]
    w_out = w_out.astype(BF16)
    prm = (w_conv, b_conv, dt_bias, a_log, d_skip, g_norm)

    def run(h, b, l, h0, hist_rows):
        z, xbc, dt = norm_matmul(h, g_mix, ws)
        xbc3 = xbc.reshape(b, l, conv_dim)
        hist8 = jnp.pad(hist_rows, ((0, 0), (SSD_TAIL - hist, 0), (0, 0)))
        y, h_fin = ssd_core(xbc3, z.reshape(b, l, inner), dt.reshape(b, l, LANES), h0, hist8, *prm)
        new_h = matmul_residual(y.reshape(b * l, inner), w_out, h)
        new_hist = jnp.concatenate([hist_rows, xbc3], axis=1)[:, l:]
        return new_h, new_hist, h_fin.reshape(b, nh, SSD_HEAD_DIM, SSD_STATE)

    zero_state = jnp.zeros((bp, nh * SSD_HEAD_DIM, SSD_STATE), F32)
    new_hp, hc_p, hh_p = run(hp, bp, s, zero_state, jnp.zeros((bp, hist, conv_dim), F32))
    new_hs, hc_s, hh_s = run(hs, nb, t, ssm_state.reshape(nb, nh * SSD_HEAD_DIM, SSD_STATE), conv_state)
    return new_hp, new_hs, hc_p, hh_p, hc_s, hh_s


def _dil_layer(hp, hs, g_mix, bufs_in, w_qkv, w_o, rel_bias, bp, s):
    nb = bufs_in[0].shape[0]
    t = hs.shape[0] // nb
    ng = len(DIL_PATTERNS)
    half = DIL_HEADS_PER_GROUP * DIL_HEAD_DIM
    w_qkv, w_o = w_qkv.astype(BF16), w_o.astype(BF16)
    (qkv_p,) = norm_matmul(hp, g_mix, [w_qkv])
    (qkv_s,) = norm_matmul(hs, g_mix, [w_qkv])
    qkv_p5 = qkv_p.reshape(bp, s, 3, ng, half)
    qkv_s3 = qkv_s.reshape(nb, t, 3 * ng * half)

    outs_p, lses_p, bufs_p, outs_s, lses_s, bufs_s = [], [], [], [], [], []
    for g, (win, r) in enumerate(DIL_PATTERNS):
        nk = win // r
        bias = _group_bias(rel_bias, g, r, nk)
        assert s % r == 0 and (s // r) % nk == 0
        to_seq = lambda a: a.reshape(bp, s // r, r, half).transpose(0, 2, 1, 3).reshape(bp * r, s // r, half)
        from_seq = lambda a: a.reshape(bp, r, s // r, half).transpose(0, 2, 1, 3).reshape(bp * s, half)
        qi, ki = jnp.arange(nk)[:, None], jnp.arange(2 * nk)[None, :]
        bias_mat = bias[:, jnp.clip(qi + nk - ki, 0, nk)]
        k_p, v_p = qkv_p5[:, :, 1, g], qkv_p5[:, :, 2, g]
        o, lse = dil_attn_prompt(to_seq(qkv_p5[:, :, 0, g]), to_seq(k_p), to_seq(v_p), bias_mat, nk)
        outs_p.append(from_seq(o))
        lses_p.append(from_seq(lse))
        keep = min(win, s)
        bufs_p.append(jnp.stack([k_p, v_p], axis=2)[:, s - keep:].reshape(bp, keep, 2, DIL_HEADS_PER_GROUP, DIL_HEAD_DIM))
        buf = bufs_in[g]
        wb = buf.shape[1]
        nbuf, o, lse = dil_attn_sample(buf.reshape(nb, wb, 2 * half), qkv_s3, g, bias, r, nk)
        outs_s.append(o.reshape(nb * t, half))
        lses_s.append(lse.reshape(nb * t, half))
        bufs_s.append(nbuf.reshape(buf.shape))
    new_hp = dil_combine_out(outs_p, lses_p, w_o, hp)
    new_hs = dil_combine_out(outs_s, lses_s, w_o, hs)
    return new_hp, new_hs, bufs_p, bufs_s


def kernel(x_prompt, x_sample, state_conv, cache_mla_ckv, cache_mla_kpe, state_ssd_conv, state_ssd, state_dil0_kv, state_dil1_kv, state_dil2_kv, page_table, p_prompt, p_sample, norm_mix, norm_ffn, norm_ple, norm_final, conv_w_in, conv_w_dw, conv_b_dw, conv_ln_g, conv_ln_b, conv_w_out, mla_w_dq, mla_g_q, mla_w_uq, mla_w_dkv, mla_g_kv, mla_w_uk, mla_w_uv, mla_w_o, ssd_w_in, ssd_w_conv, ssd_b_conv, ssd_dt_bias, ssd_a_log, ssd_d, ssd_g_norm, ssd_w_out, dil_w_qkv, dil_w_o, rel_bias, ffn_w1, ffn_w2, ple_w_gate, ple_w_proj):
    bp, s, d = x_prompt.shape
    nb, t, _ = x_sample.shape
    depth = norm_mix.shape[0]
    hp = x_prompt.reshape(bp * s, d)
    hs = x_sample.reshape(nb * t, d)
    conv_p, conv_s = [], []
    ckv_p, kpe_p, ckv_s, kpe_s = [], [], [], []
    ssdc_p, ssdh_p, ssdc_s, ssdh_s = [], [], [], []
    dil_p, dil_s = [[], [], []], [[], [], []]
    dil_in = (state_dil0_kv, state_dil1_kv, state_dil2_kv)
    for i in range(depth):
        kind, j = i % 4, i // 4
        if kind == 0:
            hp, hs, st_p, st_s = _conv_layer(hp, hs, norm_mix[i], state_conv[j], conv_w_in[j], conv_w_dw[j],
                                             conv_b_dw[j], conv_ln_g[j], conv_ln_b[j], conv_w_out[j], bp, s)
            conv_p.append(st_p)
            conv_s.append(st_s)
        elif kind == 1:
            hp, hs, c_p, r_p, c_s, r_s = _mla_layer(hp, hs, norm_mix[i], cache_mla_ckv[j], cache_mla_kpe[j], page_table,
                                                    mla_w_dq[j], mla_g_q[j], mla_w_uq[j], mla_w_dkv[j], mla_g_kv[j],
                                                    mla_w_uk[j], mla_w_uv[j], mla_w_o[j], bp, s)
            ckv_p.append(c_p)
            kpe_p.append(r_p)
            ckv_s.append(c_s)
            kpe_s.append(r_s)
        elif kind == 2:
            hp, hs, hc_p, hh_p, hc_s, hh_s = _ssd_layer(hp, hs, norm_mix[i], state_ssd_conv[j], state_ssd[j], ssd_w_in[j],
                                                        ssd_w_conv[j], ssd_b_conv[j], ssd_dt_bias[j], ssd_a_log[j],
                                                        ssd_d[j], ssd_g_norm[j], ssd_w_out[j], bp, s)
            ssdc_p.append(hc_p)
            ssdh_p.append(hh_p)
            ssdc_s.append(hc_s)
            ssdh_s.append(hh_s)
        else:
            hp, hs, bufs_p, bufs_s = _dil_layer(hp, hs, norm_mix[i], [b[j] for b in dil_in], dil_w_qkv[j], dil_w_o[j],
                                                rel_bias, bp, s)
            for g in range(len(DIL_PATTERNS)):
                dil_p[g].append(bufs_p[g])
                dil_s[g].append(bufs_s[g])
        final = i == depth - 1
        w1, w2 = ffn_w1[i].astype(BF16), ffn_w2[i].astype(BF16)
        wg, wp = ple_w_gate[i].astype(BF16), ple_w_proj[i].astype(BF16)
        hp = ffn(hp, norm_ffn[i], w1, w2)
        hs = ffn(hs, norm_ffn[i], w1, w2)
        hp = ple(hp, p_prompt[i].reshape(bp * s, -1), norm_ple[i], wg, wp, norm_final, final)
        hs = ple(hs, p_sample[i].reshape(nb * t, -1), norm_ple[i], wg, wp, norm_final, final)
    return (hp.reshape(bp, s, d), hs.reshape(nb, t, d),
            jnp.stack(conv_p), jnp.stack(conv_s),
            jnp.stack(ckv_p), jnp.stack(kpe_p), jnp.stack(ckv_s), jnp.stack(kpe_s),
            jnp.stack(ssdc_p), jnp.stack(ssdh_p), jnp.stack(ssdc_s), jnp.stack(ssdh_s),
            jnp.stack(dil_p[0]), jnp.stack(dil_p[1]), jnp.stack(dil_p[2]),
            jnp.stack(dil_s[0]), jnp.stack(dil_s[1]), jnp.stack(dil_s[2]))
```

```python
import functools
import math

import jax
import jax.numpy as jnp
from jax import lax
from jax.experimental import pallas as pl
from jax.experimental.pallas import tpu as pltpu

F32 = jnp.float32
BF16 = jnp.bfloat16
EPS = 1e-6

LANES = 128
SUBLANES = 8
VMEM_LIMIT_BYTES = 56 * 1024 * 1024

PAGE_SIZE = 128
CONV_WIDTH = 31
MLA_HEADS = 8
MLA_NOPE = 128
MLA_ROPE = 64
MLA_KV_RANK = 256
MLA_SCALE = (MLA_NOPE + MLA_ROPE) ** -0.5
ROPE_THETA = 10000.0
SSD_HEAD_DIM = 64
SSD_GROUPS = 4
SSD_STATE = 128
SSD_CONV = 4
SSD_CHUNK = 128
DIL_PATTERNS = ((128, 1), (512, 4), (2048, 16))
DIL_HEADS_PER_GROUP = 8
DIL_HEAD_DIM = 64
DIL_SCALE = DIL_HEAD_DIM ** -0.5
REL_BUCKETS = 32
REL_MAX_DIST = 2048


def _params(*sem):
    return pltpu.CompilerParams(dimension_semantics=sem, vmem_limit_bytes=VMEM_LIMIT_BYTES)


def _resident(shape):
    zeros = (0,) * len(shape)
    return pl.BlockSpec(shape, lambda *_: zeros)


def _rms(x, g):
    return x * lax.rsqrt(jnp.mean(x * x, axis=-1, keepdims=True) + EPS) * g


def _silu(x):
    return x * jax.nn.sigmoid(x)


def _dot(a, b):
    return jnp.dot(a, b, preferred_element_type=F32)


def _dot_nt(a, b):
    return lax.dot_general(a, b, (((1,), (1,)), ((), ())), preferred_element_type=F32)


def _split3(x):
    p1 = x.astype(BF16)
    r1 = x - p1.astype(F32)
    p2 = r1.astype(BF16)
    p3 = (r1 - p2.astype(F32)).astype(BF16)
    return p1, p2, p3


def _dot_sel_lhs(sel, x):
    p1, p2, p3 = _split3(x)
    return _dot(sel, p1) + _dot(sel, p2) + _dot(sel, p3)


def _dot_sel_rhs(x, sel):
    p1, p2, p3 = _split3(x)
    return _dot(p1, sel) + _dot(p2, sel) + _dot(p3, sel)


def _row_tile(m, want):
    t = min(m, want)
    assert m % t == 0, (m, t)
    return t


def _norm_matmul_kernel(x_ref, g_ref, *refs, n_w, chunk):
    w_refs, o_refs = refs[:n_w], refs[n_w:]
    xn = _rms(x_ref[...], g_ref[...]).astype(BF16)
    for w_ref, o_ref in zip(w_refs, o_refs):
        n = w_ref.shape[1]
        for c0 in range(0, n, chunk):
            c1 = min(c0 + chunk, n)
            o_ref[:, c0:c1] = _dot(xn, w_ref[:, c0:c1]).astype(o_ref.dtype)


def norm_matmul(x, g, ws, tm=256, chunk=512):
    m, k = x.shape
    tm = _row_tile(m, tm)
    return pl.pallas_call(
        functools.partial(_norm_matmul_kernel, n_w=len(ws), chunk=chunk),
        grid=(m // tm,),
        in_specs=[pl.BlockSpec((tm, k), lambda i: (i, 0)), _resident((1, k))]
        + [_resident(w.shape) for w in ws],
        out_specs=[pl.BlockSpec((tm, w.shape[1]), lambda i: (i, 0)) for w in ws],
        out_shape=[jax.ShapeDtypeStruct((m, w.shape[1]), F32) for w in ws],
        compiler_params=_params("parallel"),
        name="norm_matmul",
    )(x, g.reshape(1, k), *ws)


def _matmul_residual_kernel(a_ref, w_ref, h_ref, o_ref):
    o_ref[...] = h_ref[...] + _dot(a_ref[...].astype(BF16), w_ref[...])


def matmul_residual(a, w, h, tm=512):
    m, k = a.shape
    n = w.shape[1]
    tm = _row_tile(m, tm)
    return pl.pallas_call(
        _matmul_residual_kernel,
        grid=(m // tm,),
        in_specs=[pl.BlockSpec((tm, k), lambda i: (i, 0)), _resident(w.shape),
                  pl.BlockSpec((tm, n), lambda i: (i, 0))],
        out_specs=pl.BlockSpec((tm, n), lambda i: (i, 0)),
        out_shape=jax.ShapeDtypeStruct((m, n), F32),
        compiler_params=_params("parallel"),
        name="matmul_residual",
    )(a, w, h)


def _ffn_kernel(x_ref, g_ref, w1_ref, w2_ref, o_ref, xn_scr, acc_scr):
    j = pl.program_id(1)

    @pl.when(j == 0)
    def _():
        xn_scr[...] = _rms(x_ref[...], g_ref[...]).astype(BF16)
        acc_scr[...] = jnp.zeros_like(acc_scr)

    a = _dot(xn_scr[...], w1_ref[...])
    a = jnp.square(jnp.maximum(a, 0.0)).astype(BF16)
    acc_scr[...] += _dot(a, w2_ref[...])

    @pl.when(j == pl.num_programs(1) - 1)
    def _():
        o_ref[...] = x_ref[...] + acc_scr[...]


def ffn(x, g, w1, w2, tm=512, tf=1024):
    m, d = x.shape
    f = w1.shape[1]
    tm = _row_tile(m, tm)
    return pl.pallas_call(
        _ffn_kernel,
        grid=(m // tm, f // tf),
        in_specs=[pl.BlockSpec((tm, d), lambda i, j: (i, 0)), _resident((1, d)),
                  pl.BlockSpec((d, tf), lambda i, j: (0, j)),
                  pl.BlockSpec((tf, d), lambda i, j: (j, 0))],
        out_specs=pl.BlockSpec((tm, d), lambda i, j: (i, 0)),
        out_shape=jax.ShapeDtypeStruct((m, d), F32),
        scratch_shapes=[pltpu.VMEM((tm, d), BF16), pltpu.VMEM((tm, d), F32)],
        compiler_params=_params("parallel", "arbitrary"),
        name="ffn",
    )(x, g.reshape(1, d), w1, w2)


def _ple_kernel(x_ref, p_ref, g_ref, wg_ref, wp_ref, gf_ref, o_ref, *, final):
    x = x_ref[...]
    xn = _rms(x, g_ref[...]).astype(BF16)
    gate = jax.nn.sigmoid(_dot(xn, wg_ref[...]))
    y = x + gate * _dot(p_ref[...].astype(BF16), wp_ref[...])
    if final:
        y = _rms(y, gf_ref[...])
    o_ref[...] = y


def ple(x, p, g, wg, wp, g_final, final, tm=512):
    m, d = x.shape
    pd = p.shape[1]
    tm = _row_tile(m, tm)
    return pl.pallas_call(
        functools.partial(_ple_kernel, final=final),
        grid=(m // tm,),
        in_specs=[pl.BlockSpec((tm, d), lambda i: (i, 0)), pl.BlockSpec((tm, pd), lambda i: (i, 0)),
                  _resident((1, d)), _resident(wg.shape), _resident(wp.shape), _resident((1, d))],
        out_specs=pl.BlockSpec((tm, d), lambda i: (i, 0)),
        out_shape=jax.ShapeDtypeStruct((m, d), F32),
        compiler_params=_params("parallel"),
        name="ple",
    )(x, p, g.reshape(1, d), wg, wp, g_final.reshape(1, d))


def _conv_in_kernel(x_ref, g_ref, w_ref, o_ref):
    xn = _rms(x_ref[...], g_ref[...]).astype(BF16)
    d = o_ref.shape[1]
    o_ref[...] = _dot(xn, w_ref[:, :d]) * jax.nn.sigmoid(_dot(xn, w_ref[:, d:]))


def conv_in(x, g, w_in, tm=512):
    m, d = x.shape
    tm = _row_tile(m, tm)
    return pl.pallas_call(
        _conv_in_kernel,
        grid=(m // tm,),
        in_specs=[pl.BlockSpec((tm, d), lambda i: (i, 0)), _resident((1, d)), _resident(w_in.shape)],
        out_specs=pl.BlockSpec((tm, d), lambda i: (i, 0)),
        out_shape=jax.ShapeDtypeStruct((m, d), F32),
        compiler_params=_params("parallel"),
        name="conv_in",
    )(x, g.reshape(1, d), w_in)


def _ln_silu_out(c, h, lng_ref, lnb_ref, wo_ref):
    xc = c - jnp.mean(c, axis=-1, keepdims=True)
    y = xc * lax.rsqrt(jnp.mean(xc * xc, axis=-1, keepdims=True) + EPS) * lng_ref[...] + lnb_ref[...]
    return h + _dot(_silu(y).astype(BF16), wo_ref[...])


CONV_HALO = 32


def _conv_prompt_kernel(cur_ref, halo_ref, h_ref, wdw_ref, bdw_ref, lng_ref, lnb_ref, wo_ref, o_ref,
                        full_scr, c_scr, *, ts, width):
    i = pl.program_id(1)
    d = cur_ref.shape[2]
    full_scr[0:CONV_HALO] = jnp.where(i > 0, halo_ref[0], 0.0)
    full_scr[CONV_HALO:CONV_HALO + ts] = cur_ref[0]
    off = CONV_HALO - (width - 1)
    for c in range(d // LANES):
        cs = slice(c * LANES, (c + 1) * LANES)
        acc = jnp.broadcast_to(bdw_ref[:, cs], (ts, LANES))
        for k in range(width):
            acc = acc + full_scr[off + k:off + k + ts, cs] * wdw_ref[k:k + 1, cs]
        c_scr[:, cs] = acc
    o_ref[0] = _ln_silu_out(c_scr[...], h_ref[0], lng_ref, lnb_ref, wo_ref)


def conv_core_prompt(glu, h, w_dw, b_dw, ln_g, ln_b, w_out, ts=256):
    b, s, d = glu.shape
    ts = _row_tile(s, ts)
    width = w_dw.shape[0]
    assert width - 1 <= CONV_HALO and ts % CONV_HALO == 0
    per = ts // CONV_HALO
    vec = lambda v: v.reshape(1, d)
    return pl.pallas_call(
        functools.partial(_conv_prompt_kernel, ts=ts, width=width),
        grid=(b, s // ts),
        in_specs=[pl.BlockSpec((1, ts, d), lambda bi, i: (bi, i, 0)),
                  pl.BlockSpec((1, CONV_HALO, d), lambda bi, i: (bi, jnp.maximum(i * per - 1, 0), 0)),
                  pl.BlockSpec((1, ts, d), lambda bi, i: (bi, i, 0)),
                  _resident(w_dw.shape), _resident((1, d)), _resident((1, d)), _resident((1, d)),
                  _resident(w_out.shape)],
        out_specs=pl.BlockSpec((1, ts, d), lambda bi, i: (bi, i, 0)),
        out_shape=jax.ShapeDtypeStruct((b, s, d), F32),
        scratch_shapes=[pltpu.VMEM((CONV_HALO + ts, d), F32), pltpu.VMEM((ts, d), F32)],
        compiler_params=_params("parallel", "arbitrary"),
        name="conv_core_prompt",
    )(glu, glu, h, w_dw, vec(b_dw), vec(ln_g), vec(ln_b), w_out)


def _conv_sample_kernel(glu_ref, st_ref, h_ref, wdw_ref, bdw_ref, lng_ref, lnb_ref, wo_ref, o_ref, nst_ref,
                        full_scr, c_scr, *, bb, t, width):
    hist = width - 1
    d = glu_ref.shape[1]
    for bi in range(bb):
        full_scr[0:hist] = st_ref[bi]
        full_scr[hist:hist + t] = glu_ref[bi * t:(bi + 1) * t, :]
        acc = jnp.broadcast_to(bdw_ref[...], (t, d))
        for k in range(width):
            acc = acc + full_scr[k:k + t, :] * wdw_ref[k:k + 1, :]
        c_scr[bi * t:(bi + 1) * t, :] = acc
        nst_ref[bi] = full_scr[t:t + hist]
    o_ref[...] = _ln_silu_out(c_scr[...], h_ref[...], lng_ref, lnb_ref, wo_ref)


def conv_core_sample(glu, state, h, w_dw, b_dw, ln_g, ln_b, w_out, bb=8):
    nb, hist, d = state.shape
    t = glu.shape[0] // nb
    width = w_dw.shape[0]
    assert hist == width - 1 and nb % bb == 0
    vec = lambda v: v.reshape(1, d)
    return pl.pallas_call(
        functools.partial(_conv_sample_kernel, bb=bb, t=t, width=width),
        grid=(nb // bb,),
        in_specs=[pl.BlockSpec((bb * t, d), lambda i: (i, 0)),
                  pl.BlockSpec((bb, hist, d), lambda i: (i, 0, 0)),
                  pl.BlockSpec((bb * t, d), lambda i: (i, 0)),
                  _resident(w_dw.shape), _resident((1, d)), _resident((1, d)), _resident((1, d)),
                  _resident(w_out.shape)],
        out_specs=[pl.BlockSpec((bb * t, d), lambda i: (i, 0)),
                   pl.BlockSpec((bb, hist, d), lambda i: (i, 0, 0))],
        out_shape=[jax.ShapeDtypeStruct((nb * t, d), F32), jax.ShapeDtypeStruct((nb, hist, d), F32)],
        scratch_shapes=[pltpu.VMEM((hist + t + SUBLANES, d), F32), pltpu.VMEM((bb * t, d), F32)],
        compiler_params=_params("parallel"),
        name="conv_core_sample",
    )(glu, state, h, w_dw, vec(b_dw), vec(ln_g), vec(ln_b), w_out)


MLA_QK = MLA_KV_RANK + MLA_ROPE


def _mla_in_kernel(x_ref, g_ref, wdq_ref, gq_ref, wuq_ref, wdkv_ref, gkv_ref, wuk_ref, cos_ref, sin_ref,
                   q_ref, kcat_ref, ckv_ref, kpe_ref):
    nope_w = MLA_HEADS * MLA_NOPE
    rope_w = MLA_HEADS * MLA_ROPE
    xn = _rms(x_ref[...], g_ref[...]).astype(BF16)
    cq = _rms(_dot(xn, wdq_ref[...]), gq_ref[...]).astype(BF16)
    kv = _dot(xn, wdkv_ref[...])
    ckv = _rms(kv[:, :MLA_KV_RANK], gkv_ref[...])
    cos, sin = cos_ref[...], sin_ref[...]
    kpe = (kv[:, MLA_KV_RANK:MLA_QK] * cos[:, :MLA_ROPE]
           + kv[:, MLA_QK:MLA_QK + MLA_ROPE] * sin[:, :MLA_ROPE])
    ckv_ref[...] = ckv
    kpe_ref[...] = kpe
    kcat_ref[:, :MLA_KV_RANK] = ckv.astype(BF16)
    kcat_ref[:, MLA_KV_RANK:] = kpe.astype(BF16)
    qp = _dot(cq, wuq_ref[:, nope_w:nope_w + rope_w])
    qps = _dot(cq, wuq_ref[:, nope_w + rope_w:])
    per = LANES // MLA_ROPE
    for c in range(rope_w // LANES):
        cs = slice(c * LANES, (c + 1) * LANES)
        roped = ((qp[:, cs] * cos + qps[:, cs] * sin) * MLA_SCALE).astype(BF16)
        for hh in range(per):
            q_ref[c * per + hh, :, MLA_KV_RANK:] = roped[:, hh * MLA_ROPE:(hh + 1) * MLA_ROPE]
    for h in range(MLA_HEADS):
        qn = _dot(cq, wuq_ref[:, h * MLA_NOPE:(h + 1) * MLA_NOPE]).astype(BF16)
        q_ref[h, :, :MLA_KV_RANK] = (_dot(qn, wuk_ref[h]) * MLA_SCALE).astype(BF16)


def mla_in(x, g, w, cos_tab, sin_tab, tab_index, tm=256):
    m, d = x.shape
    tm = _row_tile(m, tm)
    assert cos_tab.shape[0] % tm == 0 or cos_tab.shape[0] == tm
    return pl.pallas_call(
        _mla_in_kernel,
        grid=(m // tm,),
        in_specs=[pl.BlockSpec((tm, d), lambda i: (i, 0)), _resident((1, d)),
                  _resident(w["dq"].shape), _resident((1, w["dq"].shape[1])), _resident(w["uq"].shape),
                  _resident(w["dkv"].shape), _resident((1, MLA_KV_RANK)), _resident(w["uk"].shape),
                  pl.BlockSpec((tm, LANES), lambda i: (tab_index(i), 0)),
                  pl.BlockSpec((tm, LANES), lambda i: (tab_index(i), 0))],
        out_specs=[pl.BlockSpec((MLA_HEADS, tm, MLA_QK), lambda i: (0, i, 0)),
                   pl.BlockSpec((tm, MLA_QK), lambda i: (i, 0)),
                   pl.BlockSpec((tm, MLA_KV_RANK), lambda i: (i, 0)),
                   pl.BlockSpec((tm, MLA_ROPE), lambda i: (i, 0))],
        out_shape=[jax.ShapeDtypeStruct((MLA_HEADS, m, MLA_QK), BF16),
                   jax.ShapeDtypeStruct((m, MLA_QK), BF16),
                   jax.ShapeDtypeStruct((m, MLA_KV_RANK), F32),
                   jax.ShapeDtypeStruct((m, MLA_ROPE), F32)],
        compiler_params=_params("parallel"),
        name="mla_in",
    )(x, g.reshape(1, d), w["dq"], w["gq"].reshape(1, -1), w["uq"], w["dkv"], w["gkv"].reshape(1, -1),
      w["uk"], cos_tab, sin_tab)


def _mla_attn_kernel(q_ref, k_ref, o_ref, m_scr, l_scr, acc_scr, *, tq, tk):
    i, j = pl.program_id(1), pl.program_id(2)
    nh = q_ref.shape[0]
    last_j = ((i + 1) * tq - 1) // tk

    @pl.when(j == 0)
    def _():
        m_scr[...] = jnp.full_like(m_scr, -jnp.inf)
        l_scr[...] = jnp.zeros_like(l_scr)
        acc_scr[...] = jnp.zeros_like(acc_scr)

    def step(masked):
        q = q_ref[...].reshape(nh * tq, MLA_QK)
        k = k_ref[...]
        s = _dot_nt(q, k)
        if masked:
            row = lax.broadcasted_iota(jnp.int32, s.shape, 0) & (tq - 1)
            col = lax.broadcasted_iota(jnp.int32, s.shape, 1)
            s = jnp.where(col + j * tk <= row + i * tq, s, -jnp.inf)
        m_prev = m_scr[...]
        m_new = jnp.maximum(m_prev, jnp.max(s, axis=-1, keepdims=True))
        alpha = jnp.exp(m_prev - m_new)
        p = jnp.exp(s - m_new)
        l_scr[...] = alpha * l_scr[...] + jnp.sum(p, axis=-1, keepdims=True)
        acc_scr[...] = alpha * acc_scr[...] + _dot(p.astype(BF16), k[:, :MLA_KV_RANK])
        m_scr[...] = m_new

    crosses_diagonal = j * tk + tk - 1 > i * tq

    @pl.when((j <= last_j) & jnp.logical_not(crosses_diagonal))
    def _():
        step(False)

    @pl.when((j <= last_j) & crosses_diagonal)
    def _():
        step(True)

    @pl.when(j == last_j)
    def _():
        o = acc_scr[...] / l_scr[...]
        o_ref[...] = o.reshape(nh, tq, MLA_KV_RANK).astype(o_ref.dtype)


def mla_attn_prompt(q, kcat, b, s, tq=256, tk=512):
    nh = q.shape[0]
    tq, tk = _row_tile(s, tq), _row_tile(s, tk)
    assert tq & (tq - 1) == 0
    nq, nk = s // tq, s // tk

    def k_index(bi, i, j):
        return (bi * nk + jnp.minimum(j, ((i + 1) * tq - 1) // tk), 0)

    return pl.pallas_call(
        functools.partial(_mla_attn_kernel, tq=tq, tk=tk),
        grid=(b, nq, nk),
        in_specs=[pl.BlockSpec((nh, tq, MLA_QK), lambda bi, i, j: (0, bi * nq + i, 0)),
                  pl.BlockSpec((tk, MLA_QK), k_index)],
        out_specs=pl.BlockSpec((nh, tq, MLA_KV_RANK), lambda bi, i, j: (0, bi * nq + i, 0)),
        out_shape=jax.ShapeDtypeStruct((nh, b * s, MLA_KV_RANK), BF16),
        scratch_shapes=[pltpu.VMEM((nh * tq, 1), F32), pltpu.VMEM((nh * tq, 1), F32),
                        pltpu.VMEM((nh * tq, MLA_KV_RANK), F32)],
        compiler_params=_params("parallel", "parallel", "arbitrary"),
        name="mla_attn_prompt",
    )(q, kcat)


def _mla_sample_kernel(pt_ref, q_ref, knew_ref, *refs, pp, t):
    del pt_ref
    ckv_refs, kpe_refs = refs[:pp], refs[pp:2 * pp]
    o_ref, kc_scr, kr_scr, m_scr, l_scr, acc_scr = refs[2 * pp:]
    j = pl.program_id(1)
    page = ckv_refs[0].shape[1]

    @pl.when(j == 0)
    def _():
        m_scr[...] = jnp.full_like(m_scr, -jnp.inf)
        l_scr[...] = jnp.zeros_like(l_scr)
        acc_scr[...] = jnp.zeros_like(acc_scr)

    for p in range(pp):
        kc_scr[p * page:(p + 1) * page, :] = ckv_refs[p][0].astype(BF16)
        kr_scr[:, p * page:(p + 1) * page] = kpe_refs[p][0].astype(BF16)
    q = q_ref[0]
    kc = kc_scr[...]
    s = _dot_nt(q[:, :MLA_KV_RANK], kc) + _dot(q[:, MLA_KV_RANK:], kr_scr[...])
    m_prev = m_scr[...]
    m_new = jnp.maximum(m_prev, jnp.max(s, axis=-1, keepdims=True))
    alpha = jnp.exp(m_prev - m_new)
    p = jnp.exp(s - m_new)
    l_scr[...] = alpha * l_scr[...] + jnp.sum(p, axis=-1, keepdims=True)
    acc_scr[...] = alpha * acc_scr[...] + _dot(p.astype(BF16), kc)
    m_scr[...] = m_new

    @pl.when(j == pl.num_programs(1) - 1)
    def _():
        qf = q.astype(F32)
        kn = knew_ref[0].astype(F32)
        row_t = lax.broadcasted_iota(jnp.int32, (q.shape[0], 1), 0) % t
        m, l, acc = m_scr[...], l_scr[...], acc_scr[...]
        for tk in range(t):
            krow = kn[tk:tk + 1, :]
            s_t = jnp.where(row_t >= tk, jnp.sum(qf * krow, axis=-1, keepdims=True), -jnp.inf)
            m_new = jnp.maximum(m, s_t)
            alpha = jnp.exp(m - m_new)
            p_t = jnp.exp(s_t - m_new)
            l = alpha * l + p_t
            acc = alpha * acc + p_t * krow[:, :MLA_KV_RANK]
            m = m_new
        o_ref[0] = (acc / l).astype(o_ref.dtype)


def mla_attn_sample(q, knew, ckv_pool, kpe_pool_t, page_table, pp=16):
    nb, rows, _ = q.shape
    t = knew.shape[1]
    n_pages = page_table.shape[1]
    page = ckv_pool.shape[1]
    pp = min(pp, n_pages)
    assert n_pages % pp == 0

    def pool_spec(shape, p):
        return pl.BlockSpec((1,) + shape, lambda bi, j, pt: (pt[bi * n_pages + j * pp + p], 0, 0))

    grid_spec = pltpu.PrefetchScalarGridSpec(
        num_scalar_prefetch=1,
        grid=(nb, n_pages // pp),
        in_specs=[pl.BlockSpec((1, rows, MLA_QK), lambda bi, j, pt: (bi, 0, 0)),
                  pl.BlockSpec((1, t, MLA_QK), lambda bi, j, pt: (bi, 0, 0))]
        + [pool_spec((page, MLA_KV_RANK), p) for p in range(pp)]
        + [pool_spec((MLA_ROPE, page), p) for p in range(pp)],
        out_specs=pl.BlockSpec((1, rows, MLA_KV_RANK), lambda bi, j, pt: (bi, 0, 0)),
        scratch_shapes=[pltpu.VMEM((pp * page, MLA_KV_RANK), BF16), pltpu.VMEM((MLA_ROPE, pp * page), BF16),
                        pltpu.VMEM((rows, 1), F32), pltpu.VMEM((rows, 1), F32),
                        pltpu.VMEM((rows, MLA_KV_RANK), F32)],
    )
    return pl.pallas_call(
        functools.partial(_mla_sample_kernel, pp=pp, t=t),
        grid_spec=grid_spec,
        out_shape=jax.ShapeDtypeStruct((nb, rows, MLA_KV_RANK), BF16),
        compiler_params=_params("parallel", "arbitrary"),
        name="mla_attn_sample",
    )(page_table.reshape(-1), q, knew, *([ckv_pool] * pp), *([kpe_pool_t] * pp))


def _mla_out_kernel(o_ref, wuv_ref, wo_ref, h_ref, y_ref):
    parts = [_dot(o_ref[h], wuv_ref[h]).astype(BF16) for h in range(o_ref.shape[0])]
    y_ref[...] = h_ref[...] + _dot(jnp.concatenate(parts, axis=-1), wo_ref[...])


def mla_out(o, w_uv, w_o, h, tm=512):
    nh, m, c = o.shape
    d = h.shape[1]
    tm = _row_tile(m, tm)
    return pl.pallas_call(
        _mla_out_kernel,
        grid=(m // tm,),
        in_specs=[pl.BlockSpec((nh, tm, c), lambda i: (0, i, 0)), _resident(w_uv.shape), _resident(w_o.shape),
                  pl.BlockSpec((tm, d), lambda i: (i, 0))],
        out_specs=pl.BlockSpec((tm, d), lambda i: (i, 0)),
        out_shape=jax.ShapeDtypeStruct((m, d), F32),
        compiler_params=_params("parallel"),
        name="mla_out",
    )(o, w_uv, w_o, h)


SSD_TAIL = SUBLANES


def _ssd_kernel(xbc_ref, z_ref, dt_ref, h0_ref, hist_ref, wc_ref, bc_ref, dtb_ref, alog_ref, dexp_ref, gn_ref,
                e_ref, et_ref, y_ref, hfin_ref, state_scr, tail_scr, full_scr, z_scr, dt_scr, y_scr,
                *, q, qb, nh, hd, ng, ns):
    c = pl.program_id(1)
    inner = nh * hd
    hpg = nh // ng
    gw = hpg * hd

    @pl.when(c == 0)
    def _():
        state_scr[...] = h0_ref[0]
        tail_scr[...] = hist_ref[0]

    full_scr[0:SSD_TAIL] = tail_scr[...]
    full_scr[SSD_TAIL:SSD_TAIL + qb] = xbc_ref[0]
    if qb < q:
        full_scr[SSD_TAIL + qb:SSD_TAIL + q] = jnp.zeros((q - qb, full_scr.shape[1]), F32)
        z_scr[0:qb] = z_ref[0]
        z_scr[qb:q] = jnp.zeros((q - qb, inner), F32)
        dt_scr[0:qb] = dt_ref[0]
        dt_scr[qb:q] = jnp.zeros((q - qb, LANES), F32)
        z, dt_raw = z_scr[...], dt_scr[...]
    else:
        tail_scr[...] = full_scr[q:q + SSD_TAIL]
        z, dt_raw = z_ref[0], dt_ref[0]

    off = SSD_TAIL - (SSD_CONV - 1)
    conv = bc_ref[...] + full_scr[off:off + q] * wc_ref[0:1]
    for k in range(1, SSD_CONV):
        conv = conv + full_scr[off + k:off + k + q] * wc_ref[k:k + 1]
    xc = _silu(conv)
    dt = jnp.maximum(dt_raw + dtb_ref[...], 0.0) + jnp.log1p(jnp.exp(-jnp.abs(dt_raw + dtb_ref[...])))
    if qb < q:
        live = lax.broadcasted_iota(jnp.int32, (q, 1), 0) < qb
        xc = jnp.where(live, xc, 0.0)
        dt = jnp.where(live, dt, 0.0)

    la = dt * (-jnp.exp(alog_ref[...]))
    ri = lax.broadcasted_iota(jnp.int32, (q, q), 0)
    ci = lax.broadcasted_iota(jnp.int32, (q, q), 1)
    causal = ri >= ci
    cs = _dot_sel_lhs(causal.astype(BF16), la)
    cs_t = cs.T
    cs_last = cs[q - 1:q, :]
    per_head = jnp.concatenate([dt, jnp.exp(cs), jnp.exp(cs_last - cs)], axis=0)
    spread = _dot_sel_rhs(per_head, e_ref[...])
    dt_x, ecs_x, edec_x = spread[0:q], spread[q:2 * q], spread[2 * q:3 * q]
    chunk_decay = jnp.broadcast_to(jnp.exp(cs_t[:, q - 1:q]), (LANES, LANES))
    decay_rows = _dot_sel_lhs(et_ref[...], chunk_decay)

    xs = xc[:, :inner]
    xdt = xs * dt_x
    xw = xdt * edec_x
    for g in range(ng):
        bg = xc[:, inner + g * ns:inner + (g + 1) * ns].astype(BF16)
        cg = xc[:, inner + ng * ns + g * ns:inner + ng * ns + (g + 1) * ns].astype(BF16)
        cb = _dot_nt(cg, bg)
        rows = slice(g * gw, (g + 1) * gw)
        st = state_scr[rows, :]
        y_scr[:, rows] = _dot_nt(cg, st.astype(BF16)) * ecs_x[:, rows]
        state_scr[rows, :] = decay_rows[rows, :] * st + _dot(xw[:, rows].T.astype(BF16), bg)
        for r in range(hpg):
            h = g * hpg + r
            hs = slice(h * hd, (h + 1) * hd)
            seg = cs[:, h:h + 1] - cs_t[h:h + 1, :]
            mat = (cb * jnp.exp(jnp.where(causal, seg, -jnp.inf))).astype(BF16)
            y_scr[:, hs] += _dot(mat, xdt[:, hs].astype(BF16))

    gated = (y_scr[...] + xs * dexp_ref[...]) * _silu(z)
    normed = []
    for g in range(ng):
        grp = gated[:, g * gw:(g + 1) * gw]
        normed.append(grp * lax.rsqrt(jnp.mean(grp * grp, axis=-1, keepdims=True) + EPS))
    out = jnp.concatenate(normed, axis=-1) * gn_ref[...]
    y_ref[0] = out[0:qb]

    @pl.when(c == pl.num_programs(1) - 1)
    def _():
        hfin_ref[0] = state_scr[...]


def ssd_core(xbc, z, dt, h0, hist, w_conv, b_conv, dt_bias, a_log, d_skip, g_norm):
    b, l, conv_dim = xbc.shape
    inner = z.shape[2]
    nh = inner // SSD_HEAD_DIM
    q = SSD_CHUNK
    qb = min(l, q)
    assert l % qb == 0 and (qb == q or l == qb)
    nc = l // qb
    hp = nh * SSD_HEAD_DIM
    head_of_lane = jnp.arange(inner) // SSD_HEAD_DIM
    e = (jnp.arange(LANES)[:, None] == head_of_lane[None, :]).astype(BF16)
    pad = lambda v: jnp.pad(v.astype(F32), (0, LANES - nh)).reshape(1, LANES)
    row = lambda v: v.astype(F32).reshape(1, -1)
    blk = lambda w: pl.BlockSpec((1, qb, w), lambda bi, ci: (bi, ci, 0))
    per_b = lambda r, w: pl.BlockSpec((1, r, w), lambda bi, ci: (bi, 0, 0))
    return pl.pallas_call(
        functools.partial(_ssd_kernel, q=q, qb=qb, nh=nh, hd=SSD_HEAD_DIM, ng=SSD_GROUPS, ns=SSD_STATE),
        grid=(b, nc),
        in_specs=[blk(conv_dim), blk(inner), blk(LANES), per_b(hp, SSD_STATE), per_b(SSD_TAIL, conv_dim),
                  _resident((SSD_CONV, conv_dim)), _resident((1, conv_dim)), _resident((1, LANES)),
                  _resident((1, LANES)), _resident((1, inner)), _resident((1, inner)),
                  _resident((LANES, inner)), _resident((inner, LANES))],
        out_specs=[blk(inner), per_b(hp, SSD_STATE)],
        out_shape=[jax.ShapeDtypeStruct((b, l, inner), F32), jax.ShapeDtypeStruct((b, hp, SSD_STATE), F32)],
        scratch_shapes=[pltpu.VMEM((hp, SSD_STATE), F32), pltpu.VMEM((SSD_TAIL, conv_dim), F32),
                        pltpu.VMEM((SSD_TAIL + q, conv_dim), F32), pltpu.VMEM((q, inner), F32),
                        pltpu.VMEM((q, LANES), F32), pltpu.VMEM((q, inner), F32)],
        compiler_params=_params("parallel", "arbitrary"),
        name="ssd_core",
    )(xbc, z, dt, h0, hist, w_conv.astype(F32), row(b_conv), pad(dt_bias), pad(a_log),
      row(jnp.repeat(d_skip, SSD_HEAD_DIM)), row(g_norm), e, e.T)


def _t5_bucket(dist):
    max_exact = REL_BUCKETS // 2
    d = jnp.maximum(dist, 1).astype(F32)
    large = max_exact + (jnp.log(d / max_exact) / math.log(REL_MAX_DIST / max_exact)
                         * (REL_BUCKETS - max_exact)).astype(jnp.int32)
    return jnp.where(dist < max_exact, dist, jnp.minimum(large, REL_BUCKETS - 1))


def _group_bias(rel_bias, g, r, nk):
    tab = rel_bias[_t5_bucket(r * jnp.arange(nk + 1))]
    return tab[:, g * DIL_HEADS_PER_GROUP:(g + 1) * DIL_HEADS_PER_GROUP].T.astype(F32)


def _dil_prompt_kernel(q_ref, kc_ref, kp_ref, vc_ref, vp_ref, bias_ref, o_ref, lse_ref, *, blk, nh, hd):
    i = pl.program_id(2)
    q = q_ref[0].astype(BF16)
    k = jnp.concatenate([kp_ref[0], kc_ref[0]], axis=0).astype(BF16)
    v = jnp.concatenate([vp_ref[0], vc_ref[0]], axis=0).astype(BF16)
    qi = lax.broadcasted_iota(jnp.int32, (blk, 2 * blk), 0)
    ki = lax.broadcasted_iota(jnp.int32, (blk, 2 * blk), 1)
    dm = qi + blk - ki
    valid = (dm >= 0) & (dm <= blk) & ((ki >= blk) | (i > 0))
    for h in range(nh):
        hs = slice(h * hd, (h + 1) * hd)
        s = _dot_nt(q[:, hs], k[:, hs]) * DIL_SCALE + bias_ref[h]
        s = jnp.where(valid, s, -jnp.inf)
        m = jnp.max(s, axis=-1, keepdims=True)
        p = jnp.exp(s - m)
        l = jnp.sum(p, axis=-1, keepdims=True)
        o_ref[0, :, hs] = _dot(p.astype(BF16), v[:, hs]) / l
        lse_ref[0, :, hs] = jnp.broadcast_to(m + jnp.log(l), (blk, hd))


def dil_attn_prompt(qkv, g, ng, r, bias_mat, blk, half):
    b, s, cols = qkv.shape
    nh = bias_mat.shape[0]
    l = s // r
    assert s % r == 0 and l % blk == 0 and cols == 3 * ng * half
    ncol = cols // half

    def spec(kind, prev):
        def index(bi, c, i):
            return (bi, jnp.maximum(i - 1, 0) if prev else i, c * ncol + kind * ng + g)
        return pl.BlockSpec((1, blk, half), index)

    out = pl.BlockSpec((1, blk, half), lambda bi, c, i: (bi, i, c))
    view = qkv.reshape(b, l, r * cols)
    o, lse = pl.pallas_call(
        functools.partial(_dil_prompt_kernel, blk=blk, nh=nh, hd=half // nh),
        grid=(b, r, l // blk),
        in_specs=[spec(0, False), spec(1, False), spec(1, True), spec(2, False), spec(2, True),
                  _resident(bias_mat.shape)],
        out_specs=[out, out],
        out_shape=[jax.ShapeDtypeStruct((b, l, r * half), F32)] * 2,
        compiler_params=_params("parallel", "parallel", "arbitrary"),
        name="dil_attn_prompt",
    )(view, view, view, view, view, bias_mat)
    return o.reshape(b * s, half), lse.reshape(b * s, half)


def _transpose_rows_kernel(k_ref, v_ref, o_ref):
    half = k_ref.shape[2]
    for c in range(half // LANES):
        cs = slice(c * LANES, (c + 1) * LANES)
        o_ref[0, 0, cs, :] = k_ref[0, :, cs].T
        o_ref[0, 1, cs, :] = v_ref[0, :, cs].T


def dil_state_prompt(qkv, g, ng, keep, half):
    b, s, cols = qkv.shape
    assert keep % LANES == 0 and (s - keep) % LANES == 0
    first = (s - keep) // LANES
    return pl.pallas_call(
        _transpose_rows_kernel,
        grid=(b, keep // LANES),
        in_specs=[pl.BlockSpec((1, LANES, half), lambda bi, i: (bi, first + i, ng + g)),
                  pl.BlockSpec((1, LANES, half), lambda bi, i: (bi, first + i, 2 * ng + g))],
        out_specs=pl.BlockSpec((1, 2, half, LANES), lambda bi, i: (bi, 0, 0, i)),
        out_shape=jax.ShapeDtypeStruct((b, 2, half, keep), F32),
        compiler_params=_params("parallel", "parallel"),
        name="dil_state_prompt",
    )(qkv, qkv)


def _dil_sample_kernel(buf_ref, q_ref, k_ref, v_ref, bias_ref, nbias_ref, nbuf_ref, o_ref, lse_ref, new_scr,
                       *, t, nh, hd):
    w = buf_ref.shape[2]
    half = nh * hd
    first_new = LANES - t
    new_scr[...] = jnp.zeros(new_scr.shape, F32)
    new_scr[first_new:, :half] = k_ref[0]
    new_scr[first_new:, half:] = v_ref[0]
    lane = lax.broadcasted_iota(jnp.int32, (LANES, LANES), 1)
    for c in range(2 * half // LANES):
        blk = slice(c * LANES, (c + 1) * LANES)
        rolled = pltpu.roll(buf_ref[0, blk, :], w - t, axis=1)
        nbuf_ref[0, blk, :] = rolled
        nbuf_ref[0, blk, w - LANES:] = jnp.where(lane >= first_new, new_scr[:, blk].T, rolled[:, w - LANES:])

    q8 = jnp.concatenate([q_ref[0], jnp.zeros((SUBLANES - t, half), F32)], axis=0)
    new8 = new_scr[LANES - SUBLANES:, :]
    outs, lses = [], []
    for h in range(nh):
        hs = slice(h * hd, (h + 1) * hd)
        k_t = buf_ref[0, h * hd:(h + 1) * hd, :].astype(BF16)
        v_t = buf_ref[0, half + h * hd:half + (h + 1) * hd, :].astype(BF16)
        qh = q8[:, hs]
        s = _dot(qh.astype(BF16), k_t) * DIL_SCALE + bias_ref[h]
        m = jnp.max(s, axis=-1, keepdims=True)
        s_new = []
        for i in range(t):
            row = SUBLANES - t + i
            s_i = (jnp.sum(qh * new8[row:row + 1, hs], axis=-1, keepdims=True) * DIL_SCALE
                   + nbias_ref[h, :, i:i + 1])
            s_new.append(s_i)
            m = jnp.maximum(m, s_i)
        p = jnp.exp(s - m)
        l = jnp.sum(p, axis=-1, keepdims=True)
        o = _dot_nt(p.astype(BF16), v_t)
        for i in range(t):
            row = SUBLANES - t + i
            p_i = jnp.exp(s_new[i] - m)
            l = l + p_i
            o = o + p_i * new8[row:row + 1, half + h * hd:half + (h + 1) * hd]
        outs.append(o / l)
        lses.append(jnp.broadcast_to(m + jnp.log(l), (SUBLANES, hd)))
    o_ref[0] = jnp.concatenate(outs, axis=-1)[0:t]
    lse_ref[0] = jnp.concatenate(lses, axis=-1)[0:t]


def _sample_bias_tables(bias, r, nk, t):
    nh = bias.shape[0]
    w = r * nk
    neg = lambda *shape: jnp.full(shape, -jnp.inf, F32)
    rev = bias[:, nk:0:-1]
    up = rev if r == 1 else jnp.concatenate([rev[:, :, None], neg(nh, nk, r - 1)], axis=2).reshape(nh, w)
    rows = []
    for tt in range(SUBLANES):
        if tt >= t:
            rows.append(neg(nh, w))
        elif tt == 0:
            rows.append(up)
        else:
            rows.append(jnp.concatenate([neg(nh, tt), up[:, :w - tt]], axis=1))
    old = jnp.stack(rows, axis=1)
    cols = []
    for i in range(t):
        col = []
        for tt in range(SUBLANES):
            if tt >= t:
                col.append(jnp.zeros((nh, 1), F32))
            elif i <= tt and (tt - i) % r == 0:
                d = (tt - i) // r
                col.append(bias[:, d:d + 1])
            else:
                col.append(neg(nh, 1))
        cols.append(jnp.concatenate(col, axis=1))
    new = jnp.stack(cols, axis=2)
    return old, jnp.pad(new, ((0, 0), (0, 0), (0, LANES - t)))


def dil_attn_sample(buf_t, qkv, g, bias, r, nk):
    nb, kvw, w = buf_t.shape
    half = kvw // 2
    t = qkv.shape[1]
    n_groups = qkv.shape[2] // (3 * half)
    nh = bias.shape[0]
    assert w == r * nk, "the buffer holds exactly one window"
    assert t <= SUBLANES and w % LANES == 0
    bias_old, bias_new = _sample_bias_tables(bias, r, nk, t)
    col = lambda c: pl.BlockSpec((1, t, half), lambda bi: (bi, 0, c))
    whole = pl.BlockSpec((1, kvw, w), lambda bi: (bi, 0, 0))
    return pl.pallas_call(
        functools.partial(_dil_sample_kernel, t=t, nh=nh, hd=half // nh),
        grid=(nb,),
        in_specs=[whole, col(g), col(n_groups + g), col(2 * n_groups + g),
                  _resident(bias_old.shape), _resident(bias_new.shape)],
        out_specs=[whole, pl.BlockSpec((1, t, half), lambda bi: (bi, 0, 0)),
                   pl.BlockSpec((1, t, half), lambda bi: (bi, 0, 0))],
        out_shape=[jax.ShapeDtypeStruct((nb, kvw, w), F32), jax.ShapeDtypeStruct((nb, t, half), F32),
                   jax.ShapeDtypeStruct((nb, t, half), F32)],
        scratch_shapes=[pltpu.VMEM((LANES, kvw), F32)],
        compiler_params=_params("parallel"),
        name="dil_attn_sample",
    )(buf_t, qkv, qkv, qkv, bias_old, bias_new)


def _band_bias(bias, nk):
    nh = bias.shape[0]
    period = 3 * nk
    v = jnp.concatenate([bias[:, ::-1], jnp.broadcast_to(bias[:, :1], (nh, nk - 1)),
                         jnp.broadcast_to(bias[:, nk:], (nh, nk))], axis=1)
    assert v.shape[1] == period
    skew = jnp.broadcast_to(v[:, None, :], (nh, nk, period)).reshape(nh, nk * period)
    return skew[:, :nk * (period - 1)].reshape(nh, nk, period - 1)[:, :, :2 * nk]


def _dil_combine_kernel(*refs, ng):
    o_refs, l_refs = refs[:ng], refs[ng:2 * ng]
    wo_ref, h_ref, y_ref = refs[2 * ng:]
    lses = [l_ref[...] for l_ref in l_refs]
    m = functools.reduce(jnp.maximum, lses)
    es = [jnp.exp(l - m) for l in lses]
    tot = functools.reduce(lambda a, b: a + b, es)
    o = functools.reduce(lambda a, b: a + b, [(e / tot) * o_ref[...] for e, o_ref in zip(es, o_refs)])
    y_ref[...] = h_ref[...] + _dot(o.astype(BF16), wo_ref[...])


def dil_combine_out(outs, lses, w_o, h, tm=512):
    m, w = outs[0].shape
    d = h.shape[1]
    tm = _row_tile(m, tm)
    ng = len(outs)
    rows = lambda width: pl.BlockSpec((tm, width), lambda i: (i, 0))
    return pl.pallas_call(
        functools.partial(_dil_combine_kernel, ng=ng),
        grid=(m // tm,),
        in_specs=[rows(w)] * (2 * ng) + [_resident(w_o.shape), rows(d)],
        out_specs=rows(d),
        out_shape=jax.ShapeDtypeStruct((m, d), F32),
        compiler_params=_params("parallel"),
        name="dil_combine_out",
    )(*outs, *lses, w_o, h)


def _conv_layer(hp, hs, g_mix, state, w_in, w_dw, b_dw, ln_g, ln_b, w_out, bp, s):
    d = hp.shape[1]
    w_in, w_out = w_in.astype(BF16), w_out.astype(BF16)
    hist = w_dw.shape[0] - 1
    glu_p = conv_in(hp, g_mix, w_in)
    glu_s = conv_in(hs, g_mix, w_in)
    glu_p3 = glu_p.reshape(bp, s, d)
    new_hp = conv_core_prompt(glu_p3, hp.reshape(bp, s, d), w_dw, b_dw, ln_g, ln_b, w_out).reshape(bp * s, d)
    new_hs, st_s = conv_core_sample(glu_s, state, hs, w_dw, b_dw, ln_g, ln_b, w_out)
    st_p = jnp.concatenate([jnp.zeros((bp, hist, d), F32), glu_p3], axis=1)[:, s:]
    return new_hp, new_hs, st_p, st_s


def _rope_tables(pos):
    inv = ROPE_THETA ** (-jnp.arange(0, MLA_ROPE, 2, dtype=F32) / MLA_ROPE)
    ang = pos.astype(F32)[:, None] * inv[None, :]
    cos, sin = jnp.cos(ang), jnp.sin(ang)
    reps = LANES // MLA_ROPE
    return jnp.tile(jnp.concatenate([cos, cos], axis=1), (1, reps)), jnp.tile(jnp.concatenate([-sin, sin], axis=1), (1, reps))


def _swap_halves(w):
    k = w.shape[0]
    w4 = w.reshape(k, -1, 2, MLA_ROPE // 2)
    return w4[:, :, ::-1, :].reshape(k, -1)


def _mla_layer(hp, hs, g_mix, ckv_pool, kpe_pool, page_table, w_dq, g_q, w_uq, w_dkv, g_kv, w_uk, w_uv, w_o, bp, s):
    nb = page_table.shape[0]
    t = hs.shape[0] // nb
    past = page_table.shape[1] * ckv_pool.shape[1]
    qr = w_dq.shape[1]
    uq = w_uq.reshape(qr, MLA_HEADS, MLA_NOPE + MLA_ROPE)
    uq_nope = uq[:, :, :MLA_NOPE].reshape(qr, -1)
    uq_rope = uq[:, :, MLA_NOPE:].reshape(qr, -1)
    dkv_rope = w_dkv[:, MLA_KV_RANK:]
    w = {"dq": w_dq.astype(BF16), "gq": g_q,
         "uq": jnp.concatenate([uq_nope, uq_rope, _swap_halves(uq_rope)], axis=1).astype(BF16),
         "dkv": jnp.concatenate([w_dkv[:, :MLA_KV_RANK], dkv_rope, _swap_halves(dkv_rope)], axis=1).astype(BF16),
         "gkv": g_kv, "uk": w_uk.astype(BF16)}
    w_uv, w_o = w_uv.astype(BF16), w_o.astype(BF16)

    tm_p = _row_tile(s, 256)
    cos_p, sin_p = _rope_tables(jnp.arange(s))
    q_p, kcat_p, ckv_p, kpe_p = mla_in(hp, g_mix, w, cos_p, sin_p, lambda i: i % (s // tm_p), tm=tm_p)
    o_p = mla_attn_prompt(q_p, kcat_p, bp, s)
    new_hp = mla_out(o_p, w_uv, w_o, hp)

    ms = hs.shape[0]
    tm_s = _row_tile(ms, 256)
    assert tm_s % t == 0
    cos_s, sin_s = _rope_tables(past + jnp.arange(t))
    cos_s, sin_s = jnp.tile(cos_s, (tm_s // t, 1)), jnp.tile(sin_s, (tm_s // t, 1))
    q_s, kcat_s, ckv_s, kpe_s = mla_in(hs, g_mix, w, cos_s, sin_s, lambda i: 0, tm=tm_s)
    q_rows = q_s.reshape(MLA_HEADS, nb, t, MLA_QK).transpose(1, 0, 2, 3).reshape(nb, MLA_HEADS * t, MLA_QK)
    o_s = mla_attn_sample(q_rows, kcat_s.reshape(nb, t, MLA_QK), ckv_pool, jnp.swapaxes(kpe_pool, 1, 2), page_table)
    o_s = o_s.reshape(nb, MLA_HEADS, t, MLA_KV_RANK).transpose(1, 0, 2, 3).reshape(MLA_HEADS, ms, MLA_KV_RANK)
    new_hs = mla_out(o_s, w_uv, w_o, hs)
    return (new_hp, new_hs, ckv_p.reshape(bp, s, -1), kpe_p.reshape(bp, s, -1),
            ckv_s.reshape(nb, t, -1), kpe_s.reshape(nb, t, -1))


def _ssd_layer(hp, hs, g_mix, conv_state, ssm_state, w_in, w_conv, b_conv, dt_bias, a_log, d_skip, g_norm, w_out,
               bp, s):
    nb = conv_state.shape[0]
    t = hs.shape[0] // nb
    inner = g_norm.shape[0]
    conv_dim = w_conv.shape[1]
    nh = dt_bias.shape[0]
    hist = SSD_CONV - 1
    ws = [w_in[:, :inner].astype(BF16), w_in[:, inner:inner + conv_dim].astype(BF16),
          jnp.pad(w_in[:, inner + conv_dim:], ((0, 0), (0, LANES - nh))).astype(BF16)]
    w_out = w_out.astype(BF16)
    prm = (w_conv, b_conv, dt_bias, a_log, d_skip, g_norm)

    def run(h, b, l, h0, hist_rows):
        z, xbc, dt = norm_matmul(h, g_mix, ws)
        xbc3 = xbc.reshape(b, l, conv_dim)
        hist8 = jnp.pad(hist_rows, ((0, 0), (SSD_TAIL - hist, 0), (0, 0)))
        y, h_fin = ssd_core(xbc3, z.reshape(b, l, inner), dt.reshape(b, l, LANES), h0, hist8, *prm)
        new_h = matmul_residual(y.reshape(b * l, inner), w_out, h)
        new_hist = jnp.concatenate([hist_rows, xbc3], axis=1)[:, l:]
        return new_h, new_hist, h_fin.reshape(b, nh, SSD_HEAD_DIM, SSD_STATE)

    zero_state = jnp.zeros((bp, nh * SSD_HEAD_DIM, SSD_STATE), F32)
    new_hp, hc_p, hh_p = run(hp, bp, s, zero_state, jnp.zeros((bp, hist, conv_dim), F32))
    new_hs, hc_s, hh_s = run(hs, nb, t, ssm_state.reshape(nb, nh * SSD_HEAD_DIM, SSD_STATE), conv_state)
    return new_hp, new_hs, hc_p, hh_p, hc_s, hh_s


def _dil_layer(hp, hs, g_mix, bufs_in, w_qkv, w_o, rel_bias, bp, s):
    nb = bufs_in[0].shape[0]
    t = hs.shape[0] // nb
    ng = len(DIL_PATTERNS)
    half = DIL_HEADS_PER_GROUP * DIL_HEAD_DIM
    w_qkv, w_o = w_qkv.astype(BF16), w_o.astype(BF16)
    (qkv_p,) = norm_matmul(hp, g_mix, [w_qkv])
    (qkv_s,) = norm_matmul(hs, g_mix, [w_qkv])
    qkv_p3 = qkv_p.reshape(bp, s, 3 * ng * half)
    qkv_s3 = qkv_s.reshape(nb, t, 3 * ng * half)
    nh, hd = DIL_HEADS_PER_GROUP, DIL_HEAD_DIM

    outs_p, lses_p, bufs_p, outs_s, lses_s, bufs_s = [], [], [], [], [], []
    for g, (win, r) in enumerate(DIL_PATTERNS):
        nk = win // r
        bias = _group_bias(rel_bias, g, r, nk)
        o, lse = dil_attn_prompt(qkv_p3, g, ng, r, _band_bias(bias, nk), nk, half)
        outs_p.append(o)
        lses_p.append(lse)
        keep = min(win, s)
        st = dil_state_prompt(qkv_p3, g, ng, keep, half)
        bufs_p.append(st.reshape(bp, 2, nh, hd, keep).transpose(0, 4, 1, 2, 3))
        buf = bufs_in[g]
        wb = buf.shape[1]
        buf_t = buf.transpose(0, 2, 3, 4, 1).reshape(nb, 2 * half, wb)
        nbuf_t, o, lse = dil_attn_sample(buf_t, qkv_s3, g, bias, r, nk)
        outs_s.append(o.reshape(nb * t, half))
        lses_s.append(lse.reshape(nb * t, half))
        bufs_s.append(nbuf_t.reshape(nb, 2, nh, hd, wb).transpose(0, 4, 1, 2, 3))
    new_hp = dil_combine_out(outs_p, lses_p, w_o, hp)
    new_hs = dil_combine_out(outs_s, lses_s, w_o, hs)
    return new_hp, new_hs, bufs_p, bufs_s


def kernel(x_prompt, x_sample, state_conv, cache_mla_ckv, cache_mla_kpe, state_ssd_conv, state_ssd, state_dil0_kv, state_dil1_kv, state_dil2_kv, page_table, p_prompt, p_sample, norm_mix, norm_ffn, norm_ple, norm_final, conv_w_in, conv_w_dw, conv_b_dw, conv_ln_g, conv_ln_b, conv_w_out, mla_w_dq, mla_g_q, mla_w_uq, mla_w_dkv, mla_g_kv, mla_w_uk, mla_w_uv, mla_w_o, ssd_w_in, ssd_w_conv, ssd_b_conv, ssd_dt_bias, ssd_a_log, ssd_d, ssd_g_norm, ssd_w_out, dil_w_qkv, dil_w_o, rel_bias, ffn_w1, ffn_w2, ple_w_gate, ple_w_proj):
    bp, s, d = x_prompt.shape
    nb, t, _ = x_sample.shape
    depth = norm_mix.shape[0]
    hp = x_prompt.reshape(bp * s, d)
    hs = x_sample.reshape(nb * t, d)
    conv_p, conv_s = [], []
    ckv_p, kpe_p, ckv_s, kpe_s = [], [], [], []
    ssdc_p, ssdh_p, ssdc_s, ssdh_s = [], [], [], []
    dil_p, dil_s = [[], [], []], [[], [], []]
    dil_in = (state_dil0_kv, state_dil1_kv, state_dil2_kv)
    for i in range(depth):
        kind, j = i % 4, i // 4
        if kind == 0:
            hp, hs, st_p, st_s = _conv_layer(hp, hs, norm_mix[i], state_conv[j], conv_w_in[j], conv_w_dw[j],
                                             conv_b_dw[j], conv_ln_g[j], conv_ln_b[j], conv_w_out[j], bp, s)
            conv_p.append(st_p)
            conv_s.append(st_s)
        elif kind == 1:
            hp, hs, c_p, r_p, c_s, r_s = _mla_layer(hp, hs, norm_mix[i], cache_mla_ckv[j], cache_mla_kpe[j], page_table,
                                                    mla_w_dq[j], mla_g_q[j], mla_w_uq[j], mla_w_dkv[j], mla_g_kv[j],
                                                    mla_w_uk[j], mla_w_uv[j], mla_w_o[j], bp, s)
            ckv_p.append(c_p)
            kpe_p.append(r_p)
            ckv_s.append(c_s)
            kpe_s.append(r_s)
        elif kind == 2:
            hp, hs, hc_p, hh_p, hc_s, hh_s = _ssd_layer(hp, hs, norm_mix[i], state_ssd_conv[j], state_ssd[j], ssd_w_in[j],
                                                        ssd_w_conv[j], ssd_b_conv[j], ssd_dt_bias[j], ssd_a_log[j],
                                                        ssd_d[j], ssd_g_norm[j], ssd_w_out[j], bp, s)
            ssdc_p.append(hc_p)
            ssdh_p.append(hh_p)
            ssdc_s.append(hc_s)
            ssdh_s.append(hh_s)
        else:
            hp, hs, bufs_p, bufs_s = _dil_layer(hp, hs, norm_mix[i], [b[j] for b in dil_in], dil_w_qkv[j], dil_w_o[j],
                                                rel_bias, bp, s)
            for g in range(len(DIL_PATTERNS)):
                dil_p[g].append(bufs_p[g])
                dil_s[g].append(bufs_s[g])
        final = i == depth - 1
        w1, w2 = ffn_w1[i].astype(BF16), ffn_w2[i].astype(BF16)
        wg, wp = ple_w_gate[i].astype(BF16), ple_w_proj[i].astype(BF16)
        hp = ffn(hp, norm_ffn[i], w1, w2)
        hs = ffn(hs, norm_ffn[i], w1, w2)
        hp = ple(hp, p_prompt[i].reshape(bp * s, -1), norm_ple[i], wg, wp, norm_final, final)
        hs = ple(hs, p_sample[i].reshape(nb * t, -1), norm_ple[i], wg, wp, norm_final, final)
    return (hp.reshape(bp, s, d), hs.reshape(nb, t, d),
            jnp.stack(conv_p), jnp.stack(conv_s),
            jnp.stack(ckv_p), jnp.stack(kpe_p), jnp.stack(ckv_s), jnp.stack(kpe_s),
            jnp.stack(ssdc_p), jnp.stack(ssdh_p), jnp.stack(ssdc_s), jnp.stack(ssdh_s),
            jnp.stack(dil_p[0]), jnp.stack(dil_p[1]), jnp.stack(dil_p[2]),
            jnp.stack(dil_s[0]), jnp.stack(dil_s[1]), jnp.stack(dil_s[2]))
```

```python
import functools
import math

import jax
import jax.numpy as jnp
from jax import lax
from jax.experimental import pallas as pl
from jax.experimental.pallas import tpu as pltpu

F32 = jnp.float32
BF16 = jnp.bfloat16
EPS = 1e-6

LANES = 128
SUBLANES = 8
VMEM_LIMIT_BYTES = 56 * 1024 * 1024

PAGE_SIZE = 128
CONV_WIDTH = 31
MLA_HEADS = 8
MLA_NOPE = 128
MLA_ROPE = 64
MLA_KV_RANK = 256
MLA_SCALE = (MLA_NOPE + MLA_ROPE) ** -0.5
ROPE_THETA = 10000.0
SSD_HEAD_DIM = 64
SSD_GROUPS = 4
SSD_STATE = 128
SSD_CONV = 4
SSD_CHUNK = 128
DIL_PATTERNS = ((128, 1), (512, 4), (2048, 16))
DIL_HEADS_PER_GROUP = 8
DIL_HEAD_DIM = 64
DIL_SCALE = DIL_HEAD_DIM ** -0.5
REL_BUCKETS = 32
REL_MAX_DIST = 2048


def _params(*sem):
    return pltpu.CompilerParams(dimension_semantics=sem, vmem_limit_bytes=VMEM_LIMIT_BYTES)


def _resident(shape):
    zeros = (0,) * len(shape)
    return pl.BlockSpec(shape, lambda *_: zeros)


def _rms(x, g):
    return x * lax.rsqrt(jnp.mean(x * x, axis=-1, keepdims=True) + EPS) * g


def _silu(x):
    return x * jax.nn.sigmoid(x)


def _dot(a, b):
    return jnp.dot(a, b, preferred_element_type=F32)


def _dot_nt(a, b):
    return lax.dot_general(a, b, (((1,), (1,)), ((), ())), preferred_element_type=F32)


def _split3(x):
    p1 = x.astype(BF16)
    r1 = x - p1.astype(F32)
    p2 = r1.astype(BF16)
    p3 = (r1 - p2.astype(F32)).astype(BF16)
    return p1, p2, p3


def _dot_sel_lhs(sel, x):
    p1, p2, p3 = _split3(x)
    return _dot(sel, p1) + _dot(sel, p2) + _dot(sel, p3)


def _dot_sel_rhs(x, sel):
    p1, p2, p3 = _split3(x)
    return _dot(p1, sel) + _dot(p2, sel) + _dot(p3, sel)


def _row_tile(m, want):
    t = min(m, want)
    assert m % t == 0, (m, t)
    return t


def _norm_matmul_kernel(x_ref, g_ref, *refs, n_w, chunk):
    w_refs, o_refs = refs[:n_w], refs[n_w:]
    xn = _rms(x_ref[...], g_ref[...]).astype(BF16)
    for w_ref, o_ref in zip(w_refs, o_refs):
        n = w_ref.shape[1]
        for c0 in range(0, n, chunk):
            c1 = min(c0 + chunk, n)
            o_ref[:, c0:c1] = _dot(xn, w_ref[:, c0:c1]).astype(o_ref.dtype)


def norm_matmul(x, g, ws, tm=256, chunk=512):
    m, k = x.shape
    tm = _row_tile(m, tm)
    return pl.pallas_call(
        functools.partial(_norm_matmul_kernel, n_w=len(ws), chunk=chunk),
        grid=(m // tm,),
        in_specs=[pl.BlockSpec((tm, k), lambda i: (i, 0)), _resident((1, k))]
        + [_resident(w.shape) for w in ws],
        out_specs=[pl.BlockSpec((tm, w.shape[1]), lambda i: (i, 0)) for w in ws],
        out_shape=[jax.ShapeDtypeStruct((m, w.shape[1]), F32) for w in ws],
        compiler_params=_params("parallel"),
        name="norm_matmul",
    )(x, g.reshape(1, k), *ws)


def _matmul_residual_kernel(a_ref, w_ref, h_ref, o_ref):
    o_ref[...] = h_ref[...] + _dot(a_ref[...].astype(BF16), w_ref[...])


def matmul_residual(a, w, h, tm=512):
    m, k = a.shape
    n = w.shape[1]
    tm = _row_tile(m, tm)
    return pl.pallas_call(
        _matmul_residual_kernel,
        grid=(m // tm,),
        in_specs=[pl.BlockSpec((tm, k), lambda i: (i, 0)), _resident(w.shape),
                  pl.BlockSpec((tm, n), lambda i: (i, 0))],
        out_specs=pl.BlockSpec((tm, n), lambda i: (i, 0)),
        out_shape=jax.ShapeDtypeStruct((m, n), F32),
        compiler_params=_params("parallel"),
        name="matmul_residual",
    )(a, w, h)


def _ffn_kernel(x_ref, g_ref, w1_ref, w2_ref, o_ref, xn_scr, acc_scr):
    j = pl.program_id(1)

    @pl.when(j == 0)
    def _():
        xn_scr[...] = _rms(x_ref[...], g_ref[...]).astype(BF16)
        acc_scr[...] = jnp.zeros_like(acc_scr)

    a = _dot(xn_scr[...], w1_ref[...])
    a = jnp.square(jnp.maximum(a, 0.0)).astype(BF16)
    acc_scr[...] += _dot(a, w2_ref[...])

    @pl.when(j == pl.num_programs(1) - 1)
    def _():
        o_ref[...] = x_ref[...] + acc_scr[...]


def ffn(x, g, w1, w2, tm=512, tf=1024):
    m, d = x.shape
    f = w1.shape[1]
    tm = _row_tile(m, tm)
    return pl.pallas_call(
        _ffn_kernel,
        grid=(m // tm, f // tf),
        in_specs=[pl.BlockSpec((tm, d), lambda i, j: (i, 0)), _resident((1, d)),
                  pl.BlockSpec((d, tf), lambda i, j: (0, j)),
                  pl.BlockSpec((tf, d), lambda i, j: (j, 0))],
        out_specs=pl.BlockSpec((tm, d), lambda i, j: (i, 0)),
        out_shape=jax.ShapeDtypeStruct((m, d), F32),
        scratch_shapes=[pltpu.VMEM((tm, d), BF16), pltpu.VMEM((tm, d), F32)],
        compiler_params=_params("parallel", "arbitrary"),
        name="ffn",
    )(x, g.reshape(1, d), w1, w2)


def _ple_kernel(x_ref, p_ref, g_ref, wg_ref, wp_ref, gf_ref, o_ref, *, final):
    x = x_ref[...]
    xn = _rms(x, g_ref[...]).astype(BF16)
    gate = jax.nn.sigmoid(_dot(xn, wg_ref[...]))
    y = x + gate * _dot(p_ref[...].astype(BF16), wp_ref[...])
    if final:
        y = _rms(y, gf_ref[...])
    o_ref[...] = y


def ple(x, p, g, wg, wp, g_final, final, tm=512):
    m, d = x.shape
    pd = p.shape[1]
    tm = _row_tile(m, tm)
    return pl.pallas_call(
        functools.partial(_ple_kernel, final=final),
        grid=(m // tm,),
        in_specs=[pl.BlockSpec((tm, d), lambda i: (i, 0)), pl.BlockSpec((tm, pd), lambda i: (i, 0)),
                  _resident((1, d)), _resident(wg.shape), _resident(wp.shape), _resident((1, d))],
        out_specs=pl.BlockSpec((tm, d), lambda i: (i, 0)),
        out_shape=jax.ShapeDtypeStruct((m, d), F32),
        compiler_params=_params("parallel"),
        name="ple",
    )(x, p, g.reshape(1, d), wg, wp, g_final.reshape(1, d))


def _conv_in_kernel(x_ref, g_ref, w_ref, o_ref):
    xn = _rms(x_ref[...], g_ref[...]).astype(BF16)
    d = o_ref.shape[1]
    o_ref[...] = _dot(xn, w_ref[:, :d]) * jax.nn.sigmoid(_dot(xn, w_ref[:, d:]))


def conv_in(x, g, w_in, tm=512):
    m, d = x.shape
    tm = _row_tile(m, tm)
    return pl.pallas_call(
        _conv_in_kernel,
        grid=(m // tm,),
        in_specs=[pl.BlockSpec((tm, d), lambda i: (i, 0)), _resident((1, d)), _resident(w_in.shape)],
        out_specs=pl.BlockSpec((tm, d), lambda i: (i, 0)),
        out_shape=jax.ShapeDtypeStruct((m, d), F32),
        compiler_params=_params("parallel"),
        name="conv_in",
    )(x, g.reshape(1, d), w_in)


def _ln_silu_out(c, h, lng_ref, lnb_ref, wo_ref):
    xc = c - jnp.mean(c, axis=-1, keepdims=True)
    y = xc * lax.rsqrt(jnp.mean(xc * xc, axis=-1, keepdims=True) + EPS) * lng_ref[...] + lnb_ref[...]
    return h + _dot(_silu(y).astype(BF16), wo_ref[...])


CONV_HALO = 32


def _conv_prompt_kernel(cur_ref, halo_ref, h_ref, wdw_ref, bdw_ref, lng_ref, lnb_ref, wo_ref, o_ref,
                        full_scr, c_scr, *, ts, width):
    i = pl.program_id(1)
    d = cur_ref.shape[2]
    full_scr[0:CONV_HALO] = jnp.where(i > 0, halo_ref[0], 0.0)
    full_scr[CONV_HALO:CONV_HALO + ts] = cur_ref[0]
    off = CONV_HALO - (width - 1)
    for c in range(d // LANES):
        cs = slice(c * LANES, (c + 1) * LANES)
        acc = jnp.broadcast_to(bdw_ref[:, cs], (ts, LANES))
        for k in range(width):
            acc = acc + full_scr[off + k:off + k + ts, cs] * wdw_ref[k:k + 1, cs]
        c_scr[:, cs] = acc
    o_ref[0] = _ln_silu_out(c_scr[...], h_ref[0], lng_ref, lnb_ref, wo_ref)


def conv_core_prompt(glu, h, w_dw, b_dw, ln_g, ln_b, w_out, ts=256):
    b, s, d = glu.shape
    ts = _row_tile(s, ts)
    width = w_dw.shape[0]
    assert width - 1 <= CONV_HALO and ts % CONV_HALO == 0
    per = ts // CONV_HALO
    vec = lambda v: v.reshape(1, d)
    return pl.pallas_call(
        functools.partial(_conv_prompt_kernel, ts=ts, width=width),
        grid=(b, s // ts),
        in_specs=[pl.BlockSpec((1, ts, d), lambda bi, i: (bi, i, 0)),
                  pl.BlockSpec((1, CONV_HALO, d), lambda bi, i: (bi, jnp.maximum(i * per - 1, 0), 0)),
                  pl.BlockSpec((1, ts, d), lambda bi, i: (bi, i, 0)),
                  _resident(w_dw.shape), _resident((1, d)), _resident((1, d)), _resident((1, d)),
                  _resident(w_out.shape)],
        out_specs=pl.BlockSpec((1, ts, d), lambda bi, i: (bi, i, 0)),
        out_shape=jax.ShapeDtypeStruct((b, s, d), F32),
        scratch_shapes=[pltpu.VMEM((CONV_HALO + ts, d), F32), pltpu.VMEM((ts, d), F32)],
        compiler_params=_params("parallel", "arbitrary"),
        name="conv_core_prompt",
    )(glu, glu, h, w_dw, vec(b_dw), vec(ln_g), vec(ln_b), w_out)


def _conv_sample_kernel(glu_ref, st_ref, h_ref, wdw_ref, bdw_ref, lng_ref, lnb_ref, wo_ref, o_ref, nst_ref,
                        full_scr, c_scr, *, bb, t, width):
    hist = width - 1
    d = glu_ref.shape[1]
    for bi in range(bb):
        full_scr[0:hist] = st_ref[bi]
        full_scr[hist:hist + t] = glu_ref[bi * t:(bi + 1) * t, :]
        acc = jnp.broadcast_to(bdw_ref[...], (t, d))
        for k in range(width):
            acc = acc + full_scr[k:k + t, :] * wdw_ref[k:k + 1, :]
        c_scr[bi * t:(bi + 1) * t, :] = acc
        nst_ref[bi] = full_scr[t:t + hist]
    o_ref[...] = _ln_silu_out(c_scr[...], h_ref[...], lng_ref, lnb_ref, wo_ref)


def conv_core_sample(glu, state, h, w_dw, b_dw, ln_g, ln_b, w_out, bb=8):
    nb, hist, d = state.shape
    t = glu.shape[0] // nb
    width = w_dw.shape[0]
    assert hist == width - 1 and nb % bb == 0
    vec = lambda v: v.reshape(1, d)
    return pl.pallas_call(
        functools.partial(_conv_sample_kernel, bb=bb, t=t, width=width),
        grid=(nb // bb,),
        in_specs=[pl.BlockSpec((bb * t, d), lambda i: (i, 0)),
                  pl.BlockSpec((bb, hist, d), lambda i: (i, 0, 0)),
                  pl.BlockSpec((bb * t, d), lambda i: (i, 0)),
                  _resident(w_dw.shape), _resident((1, d)), _resident((1, d)), _resident((1, d)),
                  _resident(w_out.shape)],
        out_specs=[pl.BlockSpec((bb * t, d), lambda i: (i, 0)),
                   pl.BlockSpec((bb, hist, d), lambda i: (i, 0, 0))],
        out_shape=[jax.ShapeDtypeStruct((nb * t, d), F32), jax.ShapeDtypeStruct((nb, hist, d), F32)],
        scratch_shapes=[pltpu.VMEM((hist + t + SUBLANES, d), F32), pltpu.VMEM((bb * t, d), F32)],
        compiler_params=_params("parallel"),
        name="conv_core_sample",
    )(glu, state, h, w_dw, vec(b_dw), vec(ln_g), vec(ln_b), w_out)


MLA_QK = MLA_KV_RANK + MLA_ROPE


def _mla_in_kernel(x_ref, g_ref, wdq_ref, gq_ref, wuq_ref, wdkv_ref, gkv_ref, wuk_ref, cos_ref, sin_ref,
                   q_ref, kcat_ref, ckv_ref, kpe_ref):
    nope_w = MLA_HEADS * MLA_NOPE
    rope_w = MLA_HEADS * MLA_ROPE
    xn = _rms(x_ref[...], g_ref[...]).astype(BF16)
    cq = _rms(_dot(xn, wdq_ref[...]), gq_ref[...]).astype(BF16)
    kv = _dot(xn, wdkv_ref[...])
    ckv = _rms(kv[:, :MLA_KV_RANK], gkv_ref[...])
    cos, sin = cos_ref[...], sin_ref[...]
    kpe = (kv[:, MLA_KV_RANK:MLA_QK] * cos[:, :MLA_ROPE]
           + kv[:, MLA_QK:MLA_QK + MLA_ROPE] * sin[:, :MLA_ROPE])
    ckv_ref[...] = ckv
    kpe_ref[...] = kpe
    kcat_ref[:, :MLA_KV_RANK] = ckv.astype(BF16)
    kcat_ref[:, MLA_KV_RANK:] = kpe.astype(BF16)
    qp = _dot(cq, wuq_ref[:, nope_w:nope_w + rope_w])
    qps = _dot(cq, wuq_ref[:, nope_w + rope_w:])
    per = LANES // MLA_ROPE
    for c in range(rope_w // LANES):
        cs = slice(c * LANES, (c + 1) * LANES)
        roped = ((qp[:, cs] * cos + qps[:, cs] * sin) * MLA_SCALE).astype(BF16)
        for hh in range(per):
            q_ref[c * per + hh, :, MLA_KV_RANK:] = roped[:, hh * MLA_ROPE:(hh + 1) * MLA_ROPE]
    for h in range(MLA_HEADS):
        qn = _dot(cq, wuq_ref[:, h * MLA_NOPE:(h + 1) * MLA_NOPE]).astype(BF16)
        q_ref[h, :, :MLA_KV_RANK] = (_dot(qn, wuk_ref[h]) * MLA_SCALE).astype(BF16)


def mla_in(x, g, w, cos_tab, sin_tab, tab_index, tm=256):
    m, d = x.shape
    tm = _row_tile(m, tm)
    assert cos_tab.shape[0] % tm == 0 or cos_tab.shape[0] == tm
    return pl.pallas_call(
        _mla_in_kernel,
        grid=(m // tm,),
        in_specs=[pl.BlockSpec((tm, d), lambda i: (i, 0)), _resident((1, d)),
                  _resident(w["dq"].shape), _resident((1, w["dq"].shape[1])), _resident(w["uq"].shape),
                  _resident(w["dkv"].shape), _resident((1, MLA_KV_RANK)), _resident(w["uk"].shape),
                  pl.BlockSpec((tm, LANES), lambda i: (tab_index(i), 0)),
                  pl.BlockSpec((tm, LANES), lambda i: (tab_index(i), 0))],
        out_specs=[pl.BlockSpec((MLA_HEADS, tm, MLA_QK), lambda i: (0, i, 0)),
                   pl.BlockSpec((tm, MLA_QK), lambda i: (i, 0)),
                   pl.BlockSpec((tm, MLA_KV_RANK), lambda i: (i, 0)),
                   pl.BlockSpec((tm, MLA_ROPE), lambda i: (i, 0))],
        out_shape=[jax.ShapeDtypeStruct((MLA_HEADS, m, MLA_QK), BF16),
                   jax.ShapeDtypeStruct((m, MLA_QK), BF16),
                   jax.ShapeDtypeStruct((m, MLA_KV_RANK), F32),
                   jax.ShapeDtypeStruct((m, MLA_ROPE), F32)],
        compiler_params=_params("parallel"),
        name="mla_in",
    )(x, g.reshape(1, d), w["dq"], w["gq"].reshape(1, -1), w["uq"], w["dkv"], w["gkv"].reshape(1, -1),
      w["uk"], cos_tab, sin_tab)


def _mla_attn_kernel(q_ref, k_ref, o_ref, m_scr, l_scr, acc_scr, *, tq, tk):
    i, j = pl.program_id(1), pl.program_id(2)
    nh = q_ref.shape[0]
    last_j = ((i + 1) * tq - 1) // tk

    @pl.when(j == 0)
    def _():
        m_scr[...] = jnp.full_like(m_scr, -jnp.inf)
        l_scr[...] = jnp.zeros_like(l_scr)
        acc_scr[...] = jnp.zeros_like(acc_scr)

    def step(masked):
        k = k_ref[...]
        kv = k[:, :MLA_KV_RANK]
        state = [(m_scr[h], l_scr[h], acc_scr[h]) for h in range(nh)]
        scores = [_dot_nt(q_ref[h], k) for h in range(nh)]
        if masked:
            row = lax.broadcasted_iota(jnp.int32, (tq, tk), 0)
            col = lax.broadcasted_iota(jnp.int32, (tq, tk), 1)
            keep = col + j * tk <= row + i * tq
        new_state = []
        for h in range(nh):
            s = jnp.where(keep, scores[h], -jnp.inf) if masked else scores[h]
            m_prev, l_prev, acc_prev = state[h]
            m_new = jnp.maximum(m_prev, jnp.max(s, axis=-1, keepdims=True))
            alpha = jnp.exp(m_prev - m_new)
            p = jnp.exp(s - m_new)
            l_new = alpha * l_prev + jnp.sum(p, axis=-1, keepdims=True)
            new_state.append((m_new, l_new, alpha * acc_prev + _dot(p.astype(BF16), kv)))
        for h in range(nh):
            m_scr[h], l_scr[h], acc_scr[h] = new_state[h]

    crosses_diagonal = j * tk + tk - 1 > i * tq

    @pl.when((j <= last_j) & jnp.logical_not(crosses_diagonal))
    def _():
        step(False)

    @pl.when((j <= last_j) & crosses_diagonal)
    def _():
        step(True)

    @pl.when(j == last_j)
    def _():
        o_ref[...] = (acc_scr[...] / l_scr[...]).astype(o_ref.dtype)


def mla_attn_prompt(q, kcat, b, s, tq=256, tk=1024):
    nh = q.shape[0]
    tq, tk = _row_tile(s, tq), _row_tile(s, tk)
    assert tq & (tq - 1) == 0
    nq, nk = s // tq, s // tk

    def k_index(bi, i, j):
        return (bi * nk + jnp.minimum(j, ((i + 1) * tq - 1) // tk), 0)

    return pl.pallas_call(
        functools.partial(_mla_attn_kernel, tq=tq, tk=tk),
        grid=(b, nq, nk),
        in_specs=[pl.BlockSpec((nh, tq, MLA_QK), lambda bi, i, j: (0, bi * nq + i, 0)),
                  pl.BlockSpec((tk, MLA_QK), k_index)],
        out_specs=pl.BlockSpec((nh, tq, MLA_KV_RANK), lambda bi, i, j: (0, bi * nq + i, 0)),
        out_shape=jax.ShapeDtypeStruct((nh, b * s, MLA_KV_RANK), BF16),
        scratch_shapes=[pltpu.VMEM((nh, tq, 1), F32), pltpu.VMEM((nh, tq, 1), F32),
                        pltpu.VMEM((nh, tq, MLA_KV_RANK), F32)],
        compiler_params=_params("parallel", "parallel", "arbitrary"),
        name="mla_attn_prompt",
    )(q, kcat)


def _mla_sample_kernel(pt_ref, q_ref, knew_ref, *refs, pp, t):
    del pt_ref
    ckv_refs, kpe_refs = refs[:pp], refs[pp:2 * pp]
    o_ref, m_scr, l_scr, acc_scr = refs[2 * pp:]
    j = pl.program_id(1)

    @pl.when(j == 0)
    def _():
        m_scr[...] = jnp.full_like(m_scr, -jnp.inf)
        l_scr[...] = jnp.zeros_like(l_scr)
        acc_scr[...] = jnp.zeros_like(acc_scr)

    q = q_ref[0]
    kc = jnp.concatenate([ckv_refs[p][0].astype(BF16) for p in range(pp)], axis=0)
    kr = jnp.concatenate([kpe_refs[p][0].astype(BF16) for p in range(pp)], axis=1)
    s = _dot_nt(q[:, :MLA_KV_RANK], kc) + _dot(q[:, MLA_KV_RANK:], kr)
    m_prev = m_scr[...]
    m_new = jnp.maximum(m_prev, jnp.max(s, axis=-1, keepdims=True))
    alpha = jnp.exp(m_prev - m_new)
    p = jnp.exp(s - m_new)
    l_scr[...] = alpha * l_scr[...] + jnp.sum(p, axis=-1, keepdims=True)
    acc_scr[...] = alpha * acc_scr[...] + _dot(p.astype(BF16), kc)
    m_scr[...] = m_new

    @pl.when(j == pl.num_programs(1) - 1)
    def _():
        qf = q.astype(F32)
        kn = knew_ref[0].astype(F32)
        row_t = lax.broadcasted_iota(jnp.int32, (q.shape[0], 1), 0) % t
        m, l, acc = m_scr[...], l_scr[...], acc_scr[...]
        for tk in range(t):
            krow = kn[tk:tk + 1, :]
            s_t = jnp.where(row_t >= tk, jnp.sum(qf * krow, axis=-1, keepdims=True), -jnp.inf)
            m_new = jnp.maximum(m, s_t)
            alpha = jnp.exp(m - m_new)
            p_t = jnp.exp(s_t - m_new)
            l = alpha * l + p_t
            acc = alpha * acc + p_t * krow[:, :MLA_KV_RANK]
            m = m_new
        o_ref[0] = (acc / l).astype(o_ref.dtype)


def mla_attn_sample(q, knew, ckv_pool, kpe_pool_t, page_table, pp=32):
    nb, rows, _ = q.shape
    t = knew.shape[1]
    n_pages = page_table.shape[1]
    page = ckv_pool.shape[1]
    pp = min(pp, n_pages)
    assert n_pages % pp == 0

    def pool_spec(shape, p):
        return pl.BlockSpec((1,) + shape, lambda bi, j, pt: (pt[bi * n_pages + j * pp + p], 0, 0))

    grid_spec = pltpu.PrefetchScalarGridSpec(
        num_scalar_prefetch=1,
        grid=(nb, n_pages // pp),
        in_specs=[pl.BlockSpec((1, rows, MLA_QK), lambda bi, j, pt: (bi, 0, 0)),
                  pl.BlockSpec((1, t, MLA_QK), lambda bi, j, pt: (bi, 0, 0))]
        + [pool_spec((page, MLA_KV_RANK), p) for p in range(pp)]
        + [pool_spec((MLA_ROPE, page), p) for p in range(pp)],
        out_specs=pl.BlockSpec((1, rows, MLA_KV_RANK), lambda bi, j, pt: (bi, 0, 0)),
        scratch_shapes=[pltpu.VMEM((rows, 1), F32), pltpu.VMEM((rows, 1), F32),
                        pltpu.VMEM((rows, MLA_KV_RANK), F32)],
    )
    return pl.pallas_call(
        functools.partial(_mla_sample_kernel, pp=pp, t=t),
        grid_spec=grid_spec,
        out_shape=jax.ShapeDtypeStruct((nb, rows, MLA_KV_RANK), BF16),
        compiler_params=_params("parallel", "arbitrary"),
        name="mla_attn_sample",
    )(page_table.reshape(-1), q, knew, *([ckv_pool] * pp), *([kpe_pool_t] * pp))


def _mla_out_kernel(o_ref, wuv_ref, wo_ref, h_ref, y_ref):
    parts = [_dot(o_ref[h], wuv_ref[h]).astype(BF16) for h in range(o_ref.shape[0])]
    y_ref[...] = h_ref[...] + _dot(jnp.concatenate(parts, axis=-1), wo_ref[...])


def mla_out(o, w_uv, w_o, h, tm=512):
    nh, m, c = o.shape
    d = h.shape[1]
    tm = _row_tile(m, tm)
    return pl.pallas_call(
        _mla_out_kernel,
        grid=(m // tm,),
        in_specs=[pl.BlockSpec((nh, tm, c), lambda i: (0, i, 0)), _resident(w_uv.shape), _resident(w_o.shape),
                  pl.BlockSpec((tm, d), lambda i: (i, 0))],
        out_specs=pl.BlockSpec((tm, d), lambda i: (i, 0)),
        out_shape=jax.ShapeDtypeStruct((m, d), F32),
        compiler_params=_params("parallel"),
        name="mla_out",
    )(o, w_uv, w_o, h)


SSD_TAIL = SUBLANES


def _ssd_kernel(xbc_ref, z_ref, dt_ref, h0_ref, hist_ref, wc_ref, bc_ref, dtb_ref, alog_ref, dexp_ref, gn_ref,
                e_ref, et_ref, y_ref, hfin_ref, state_scr, tail_scr, full_scr, z_scr, dt_scr,
                *, q, qb, nh, hd, ng, ns):
    c = pl.program_id(1)
    inner = nh * hd
    hpg = nh // ng
    gw = hpg * hd

    @pl.when(c == 0)
    def _():
        state_scr[...] = h0_ref[0]
        tail_scr[...] = hist_ref[0]

    full_scr[0:SSD_TAIL] = tail_scr[...]
    full_scr[SSD_TAIL:SSD_TAIL + qb] = xbc_ref[0]
    if qb < q:
        full_scr[SSD_TAIL + qb:SSD_TAIL + q] = jnp.zeros((q - qb, full_scr.shape[1]), F32)
        z_scr[0:qb] = z_ref[0]
        z_scr[qb:q] = jnp.zeros((q - qb, inner), F32)
        dt_scr[0:qb] = dt_ref[0]
        dt_scr[qb:q] = jnp.zeros((q - qb, LANES), F32)
        z, dt_raw = z_scr[...], dt_scr[...]
    else:
        tail_scr[...] = full_scr[q:q + SSD_TAIL]
        z, dt_raw = z_ref[0], dt_ref[0]

    off = SSD_TAIL - (SSD_CONV - 1)
    conv = bc_ref[...] + full_scr[off:off + q] * wc_ref[0:1]
    for k in range(1, SSD_CONV):
        conv = conv + full_scr[off + k:off + k + q] * wc_ref[k:k + 1]
    xc = _silu(conv)
    dt = jnp.maximum(dt_raw + dtb_ref[...], 0.0) + jnp.log1p(jnp.exp(-jnp.abs(dt_raw + dtb_ref[...])))
    if qb < q:
        live = lax.broadcasted_iota(jnp.int32, (q, 1), 0) < qb
        xc = jnp.where(live, xc, 0.0)
        dt = jnp.where(live, dt, 0.0)

    la = dt * (-jnp.exp(alog_ref[...]))
    ri = lax.broadcasted_iota(jnp.int32, (q, q), 0)
    ci = lax.broadcasted_iota(jnp.int32, (q, q), 1)
    causal = ri >= ci
    cs = _dot_sel_lhs(causal.astype(BF16), la)
    cs_t = cs.T
    cs_last = cs[q - 1:q, :]
    per_head = jnp.concatenate([dt, jnp.exp(cs), jnp.exp(cs_last - cs)], axis=0)
    spread = _dot_sel_rhs(per_head, e_ref[...])
    dt_x, ecs_x, edec_x = spread[0:q], spread[q:2 * q], spread[2 * q:3 * q]
    chunk_decay = jnp.broadcast_to(jnp.exp(cs_t[:, q - 1:q]), (LANES, LANES))
    decay_rows = _dot_sel_lhs(et_ref[...], chunk_decay)

    xs = xc[:, :inner]
    xdt = xs * dt_x
    xw = xdt * edec_x
    states = [state_scr[g * gw:(g + 1) * gw, :] for g in range(ng)]
    per = LANES // hd
    lane_head = lax.broadcasted_iota(jnp.int32, (1, LANES), 1) // hd
    new_states, y_groups = [], []
    for g in range(ng):
        bg = xc[:, inner + g * ns:inner + (g + 1) * ns].astype(BF16)
        cg = xc[:, inner + ng * ns + g * ns:inner + ng * ns + (g + 1) * ns].astype(BF16)
        cb = _dot_nt(cg, bg)
        rows = slice(g * gw, (g + 1) * gw)
        st = states[g]
        y_off = _dot_nt(cg, st.astype(BF16)) * ecs_x[:, rows]
        new_states.append(decay_rows[rows, :] * st + _dot(xw[:, rows].T.astype(BF16), bg))
        y_diag = []
        for blk in range(gw // LANES):
            lanes = slice(g * gw + blk * LANES, g * gw + (blk + 1) * LANES)
            x_blk = xdt[:, lanes]
            acc = None
            for hh in range(per):
                h = (g * gw + blk * LANES) // hd + hh
                seg = cs[:, h:h + 1] - cs_t[h:h + 1, :]
                mat = (cb * jnp.exp(jnp.where(causal, seg, -jnp.inf))).astype(BF16)
                part = _dot(mat, jnp.where(lane_head == hh, x_blk, 0.0).astype(BF16))
                acc = part if acc is None else acc + part
            y_diag.append(acc)
        y_groups.append(y_off + jnp.concatenate(y_diag, axis=-1))
    for g in range(ng):
        state_scr[g * gw:(g + 1) * gw, :] = new_states[g]

    gated = (jnp.concatenate(y_groups, axis=-1) + xs * dexp_ref[...]) * _silu(z)
    normed = []
    for g in range(ng):
        grp = gated[:, g * gw:(g + 1) * gw]
        normed.append(grp * lax.rsqrt(jnp.mean(grp * grp, axis=-1, keepdims=True) + EPS))
    out = jnp.concatenate(normed, axis=-1) * gn_ref[...]
    y_ref[0] = out[0:qb]

    @pl.when(c == pl.num_programs(1) - 1)
    def _():
        hfin_ref[0] = state_scr[...]


def ssd_core(xbc, z, dt, h0, hist, w_conv, b_conv, dt_bias, a_log, d_skip, g_norm):
    b, l, conv_dim = xbc.shape
    inner = z.shape[2]
    nh = inner // SSD_HEAD_DIM
    q = SSD_CHUNK
    qb = min(l, q)
    assert l % qb == 0 and (qb == q or l == qb)
    nc = l // qb
    hp = nh * SSD_HEAD_DIM
    head_of_lane = jnp.arange(inner) // SSD_HEAD_DIM
    e = (jnp.arange(LANES)[:, None] == head_of_lane[None, :]).astype(BF16)
    pad = lambda v: jnp.pad(v.astype(F32), (0, LANES - nh)).reshape(1, LANES)
    row = lambda v: v.astype(F32).reshape(1, -1)
    blk = lambda w: pl.BlockSpec((1, qb, w), lambda bi, ci: (bi, ci, 0))
    per_b = lambda r, w: pl.BlockSpec((1, r, w), lambda bi, ci: (bi, 0, 0))
    return pl.pallas_call(
        functools.partial(_ssd_kernel, q=q, qb=qb, nh=nh, hd=SSD_HEAD_DIM, ng=SSD_GROUPS, ns=SSD_STATE),
        grid=(b, nc),
        in_specs=[blk(conv_dim), blk(inner), blk(LANES), per_b(hp, SSD_STATE), per_b(SSD_TAIL, conv_dim),
                  _resident((SSD_CONV, conv_dim)), _resident((1, conv_dim)), _resident((1, LANES)),
                  _resident((1, LANES)), _resident((1, inner)), _resident((1, inner)),
                  _resident((LANES, inner)), _resident((inner, LANES))],
        out_specs=[blk(inner), per_b(hp, SSD_STATE)],
        out_shape=[jax.ShapeDtypeStruct((b, l, inner), F32), jax.ShapeDtypeStruct((b, hp, SSD_STATE), F32)],
        scratch_shapes=[pltpu.VMEM((hp, SSD_STATE), F32), pltpu.VMEM((SSD_TAIL, conv_dim), F32),
                        pltpu.VMEM((SSD_TAIL + q, conv_dim), F32), pltpu.VMEM((q, inner), F32),
                        pltpu.VMEM((q, LANES), F32)],
        compiler_params=_params("parallel", "arbitrary"),
        name="ssd_core",
    )(xbc, z, dt, h0, hist, w_conv.astype(F32), row(b_conv), pad(dt_bias), pad(a_log),
      row(jnp.repeat(d_skip, SSD_HEAD_DIM)), row(g_norm), e, e.T)


def _t5_bucket(dist):
    max_exact = REL_BUCKETS // 2
    d = jnp.maximum(dist, 1).astype(F32)
    large = max_exact + (jnp.log(d / max_exact) / math.log(REL_MAX_DIST / max_exact)
                         * (REL_BUCKETS - max_exact)).astype(jnp.int32)
    return jnp.where(dist < max_exact, dist, jnp.minimum(large, REL_BUCKETS - 1))


def _group_bias(rel_bias, g, r, nk):
    tab = rel_bias[_t5_bucket(r * jnp.arange(nk + 1))]
    return tab[:, g * DIL_HEADS_PER_GROUP:(g + 1) * DIL_HEADS_PER_GROUP].T.astype(F32)


def _dil_prompt_kernel(q_ref, kc_ref, kp_ref, vc_ref, vp_ref, bias_ref, o_ref, lse_ref, *, blk, r, hd):
    i = pl.program_id(1)
    per = q_ref.shape[2] // hd
    qi = lax.broadcasted_iota(jnp.int32, (blk, 2 * blk), 0)
    ki = lax.broadcasted_iota(jnp.int32, (blk, 2 * blk), 1)
    dm = qi + blk - ki
    valid = (dm >= 0) & (dm <= blk) & ((ki >= blk) | (i > 0))
    biases = [bias_ref[hh] for hh in range(per)]
    results = []
    for c in range(r):
        rows = pl.ds(c, blk, stride=r) if r > 1 else slice(None)
        q = q_ref[0, rows, :].astype(BF16)
        k = jnp.concatenate([kp_ref[0, rows, :], kc_ref[0, rows, :]], axis=0).astype(BF16)
        v = jnp.concatenate([vp_ref[0, rows, :], vc_ref[0, rows, :]], axis=0).astype(BF16)
        outs, lses = [], []
        for hh in range(per):
            hs = slice(hh * hd, (hh + 1) * hd)
            s = _dot_nt(q[:, hs], k[:, hs]) * DIL_SCALE + biases[hh]
            s = jnp.where(valid, s, -jnp.inf)
            m = jnp.max(s, axis=-1, keepdims=True)
            p = jnp.exp(s - m)
            l = jnp.sum(p, axis=-1, keepdims=True)
            outs.append(_dot(p.astype(BF16), v[:, hs]) / l)
            lses.append(jnp.broadcast_to(m + jnp.log(l), (blk, hd)))
        results.append((rows, jnp.concatenate(outs, axis=-1), jnp.concatenate(lses, axis=-1)))
    for rows, o, lse in results:
        o_ref[0, rows, :] = o
        lse_ref[0, rows, :] = lse


def dil_attn_prompt(qkv, g, ng, r, bias_mat, blk, half):
    b, s, cols = qkv.shape
    nh = bias_mat.shape[0]
    hd = half // nh
    width = LANES if r > 1 else half
    per = width // hd
    tile = blk * r
    assert s % tile == 0 and cols == 3 * ng * half and half % width == 0
    lane_blocks = half // width

    def spec(kind, prev):
        def index(bi, i, hb):
            return (bi, jnp.maximum(i - 1, 0) if prev else i, (kind * ng + g) * lane_blocks + hb)
        return pl.BlockSpec((1, tile, width), index)

    out = pl.BlockSpec((1, tile, width), lambda bi, i, hb: (bi, i, hb))
    o, lse = pl.pallas_call(
        functools.partial(_dil_prompt_kernel, blk=blk, r=r, hd=hd),
        grid=(b, s // tile, lane_blocks),
        in_specs=[spec(0, False), spec(1, False), spec(1, True), spec(2, False), spec(2, True),
                  pl.BlockSpec((per, blk, 2 * blk), lambda bi, i, hb: (hb, 0, 0))],
        out_specs=[out, out],
        out_shape=[jax.ShapeDtypeStruct((b, s, half), F32)] * 2,
        compiler_params=_params("parallel", "arbitrary", "arbitrary"),
        name="dil_attn_prompt",
    )(qkv, qkv, qkv, qkv, qkv, bias_mat)
    return o.reshape(b * s, half), lse.reshape(b * s, half)


def _transpose_rows_kernel(k_ref, v_ref, o_ref):
    half = k_ref.shape[2]
    for c in range(half // LANES):
        cs = slice(c * LANES, (c + 1) * LANES)
        o_ref[0, 0, cs, :] = k_ref[0, :, cs].T
        o_ref[0, 1, cs, :] = v_ref[0, :, cs].T


def dil_state_prompt(qkv, g, ng, keep, half):
    b, s, cols = qkv.shape
    assert keep % LANES == 0 and (s - keep) % LANES == 0
    first = (s - keep) // LANES
    return pl.pallas_call(
        _transpose_rows_kernel,
        grid=(b, keep // LANES),
        in_specs=[pl.BlockSpec((1, LANES, half), lambda bi, i: (bi, first + i, ng + g)),
                  pl.BlockSpec((1, LANES, half), lambda bi, i: (bi, first + i, 2 * ng + g))],
        out_specs=pl.BlockSpec((1, 2, half, LANES), lambda bi, i: (bi, 0, 0, i)),
        out_shape=jax.ShapeDtypeStruct((b, 2, half, keep), F32),
        compiler_params=_params("parallel", "parallel"),
        name="dil_state_prompt",
    )(qkv, qkv)


def _dil_sample_kernel(buf_ref, q_ref, k_ref, v_ref, bias_ref, nbias_ref, hmask_ref, nbuf_ref, o_ref, lse_ref,
                       new_scr, *, t, nh, hd):
    w = buf_ref.shape[2]
    half = nh * hd
    first_new = LANES - t
    new_scr[...] = jnp.zeros(new_scr.shape, F32)
    new_scr[first_new:, :half] = k_ref[0]
    new_scr[first_new:, half:] = v_ref[0]
    lane = lax.broadcasted_iota(jnp.int32, (LANES, LANES), 1)
    for c in range(2 * half // LANES):
        blk = slice(c * LANES, (c + 1) * LANES)
        rolled = pltpu.roll(buf_ref[0, blk, :], w - t, axis=1)
        nbuf_ref[0, blk, :] = rolled
        nbuf_ref[0, blk, w - LANES:] = jnp.where(lane >= first_new, new_scr[:, blk].T, rolled[:, w - LANES:])

    q8 = jnp.concatenate([q_ref[0], jnp.zeros((SUBLANES - t, half), F32)], axis=0)
    new8 = new_scr[LANES - SUBLANES:, :]
    hmask = hmask_ref[...]
    q_heads = jnp.concatenate([q8] * nh, axis=0) * hmask
    k_t = buf_ref[0, 0:half, :].astype(BF16)
    v_t = buf_ref[0, half:, :].astype(BF16)
    s = _dot(q_heads.astype(BF16), k_t) * DIL_SCALE + bias_ref[...]
    m = jnp.max(s, axis=-1, keepdims=True)
    s_new = []
    for i in range(t):
        row = SUBLANES - t + i
        s_i = (jnp.sum(q_heads * new8[row:row + 1, :half], axis=-1, keepdims=True) * DIL_SCALE
               + nbias_ref[:, i:i + 1])
        s_new.append(s_i)
        m = jnp.maximum(m, s_i)
    p = jnp.exp(s - m)
    l = jnp.sum(p, axis=-1, keepdims=True)
    o = _dot_nt(p.astype(BF16), v_t)
    for i in range(t):
        row = SUBLANES - t + i
        p_i = jnp.exp(s_new[i] - m)
        l = l + p_i
        o = o + p_i * new8[row:row + 1, half:]
    o = (o / l) * hmask
    lse = (m + jnp.log(l)) * hmask
    blocks = lambda a: functools.reduce(lambda x, y: x + y, [a[h * SUBLANES:(h + 1) * SUBLANES] for h in range(nh)])
    o_ref[0] = blocks(o)[0:t]
    lse_ref[0] = blocks(lse)[0:t]


def _sample_bias_tables(bias, r, nk, t):
    nh = bias.shape[0]
    w = r * nk
    neg = lambda *shape: jnp.full(shape, -jnp.inf, F32)
    rev = bias[:, nk:0:-1]
    up = rev if r == 1 else jnp.concatenate([rev[:, :, None], neg(nh, nk, r - 1)], axis=2).reshape(nh, w)
    rows = []
    for tt in range(SUBLANES):
        if tt >= t:
            rows.append(neg(nh, w))
        elif tt == 0:
            rows.append(up)
        else:
            rows.append(jnp.concatenate([neg(nh, tt), up[:, :w - tt]], axis=1))
    old = jnp.stack(rows, axis=1)
    cols = []
    for i in range(t):
        col = []
        for tt in range(SUBLANES):
            if tt >= t:
                col.append(jnp.zeros((nh, 1), F32))
            elif i <= tt and (tt - i) % r == 0:
                d = (tt - i) // r
                col.append(bias[:, d:d + 1])
            else:
                col.append(neg(nh, 1))
        cols.append(jnp.concatenate(col, axis=1))
    new = jnp.stack(cols, axis=2)
    return old, jnp.pad(new, ((0, 0), (0, 0), (0, LANES - t)))


def dil_attn_sample(buf_t, qkv, g, bias, r, nk):
    nb, kvw, w = buf_t.shape
    half = kvw // 2
    t = qkv.shape[1]
    n_groups = qkv.shape[2] // (3 * half)
    nh = bias.shape[0]
    assert w == r * nk, "the buffer holds exactly one window"
    assert t <= SUBLANES and w % LANES == 0
    bias_old, bias_new = _sample_bias_tables(bias, r, nk, t)
    bias_old = bias_old.reshape(nh * SUBLANES, w)
    bias_new = bias_new.reshape(nh * SUBLANES, LANES)
    hd = half // nh
    hmask = (jnp.arange(nh * SUBLANES)[:, None] // SUBLANES == jnp.arange(half)[None, :] // hd).astype(F32)
    col = lambda c: pl.BlockSpec((1, t, half), lambda bi: (bi, 0, c))
    whole = pl.BlockSpec((1, kvw, w), lambda bi: (bi, 0, 0))
    return pl.pallas_call(
        functools.partial(_dil_sample_kernel, t=t, nh=nh, hd=hd),
        grid=(nb,),
        in_specs=[whole, col(g), col(n_groups + g), col(2 * n_groups + g),
                  _resident(bias_old.shape), _resident(bias_new.shape), _resident(hmask.shape)],
        out_specs=[whole, pl.BlockSpec((1, t, half), lambda bi: (bi, 0, 0)),
                   pl.BlockSpec((1, t, half), lambda bi: (bi, 0, 0))],
        out_shape=[jax.ShapeDtypeStruct((nb, kvw, w), F32), jax.ShapeDtypeStruct((nb, t, half), F32),
                   jax.ShapeDtypeStruct((nb, t, half), F32)],
        scratch_shapes=[pltpu.VMEM((LANES, kvw), F32)],
        compiler_params=_params("parallel"),
        name="dil_attn_sample",
    )(buf_t, qkv, qkv, qkv, bias_old, bias_new, hmask)


def _band_bias(bias, nk):
    nh = bias.shape[0]
    period = 3 * nk
    v = jnp.concatenate([bias[:, ::-1], jnp.broadcast_to(bias[:, :1], (nh, nk - 1)),
                         jnp.broadcast_to(bias[:, nk:], (nh, nk))], axis=1)
    assert v.shape[1] == period
    skew = jnp.broadcast_to(v[:, None, :], (nh, nk, period)).reshape(nh, nk * period)
    return skew[:, :nk * (period - 1)].reshape(nh, nk, period - 1)[:, :, :2 * nk]


def _dil_combine_kernel(*refs, ng):
    o_refs, l_refs = refs[:ng], refs[ng:2 * ng]
    wo_ref, h_ref, y_ref = refs[2 * ng:]
    lses = [l_ref[...] for l_ref in l_refs]
    m = functools.reduce(jnp.maximum, lses)
    es = [jnp.exp(l - m) for l in lses]
    tot = functools.reduce(lambda a, b: a + b, es)
    o = functools.reduce(lambda a, b: a + b, [(e / tot) * o_ref[...] for e, o_ref in zip(es, o_refs)])
    y_ref[...] = h_ref[...] + _dot(o.astype(BF16), wo_ref[...])


def dil_combine_out(outs, lses, w_o, h, tm=512):
    m, w = outs[0].shape
    d = h.shape[1]
    tm = _row_tile(m, tm)
    ng = len(outs)
    rows = lambda width: pl.BlockSpec((tm, width), lambda i: (i, 0))
    return pl.pallas_call(
        functools.partial(_dil_combine_kernel, ng=ng),
        grid=(m // tm,),
        in_specs=[rows(w)] * (2 * ng) + [_resident(w_o.shape), rows(d)],
        out_specs=rows(d),
        out_shape=jax.ShapeDtypeStruct((m, d), F32),
        compiler_params=_params("parallel"),
        name="dil_combine_out",
    )(*outs, *lses, w_o, h)


def _conv_layer(hp, hs, g_mix, state, w_in, w_dw, b_dw, ln_g, ln_b, w_out, bp, s):
    d = hp.shape[1]
    w_in, w_out = w_in.astype(BF16), w_out.astype(BF16)
    hist = w_dw.shape[0] - 1
    glu_p = conv_in(hp, g_mix, w_in)
    glu_s = conv_in(hs, g_mix, w_in)
    glu_p3 = glu_p.reshape(bp, s, d)
    new_hp = conv_core_prompt(glu_p3, hp.reshape(bp, s, d), w_dw, b_dw, ln_g, ln_b, w_out).reshape(bp * s, d)
    new_hs, st_s = conv_core_sample(glu_s, state, hs, w_dw, b_dw, ln_g, ln_b, w_out)
    st_p = jnp.concatenate([jnp.zeros((bp, hist, d), F32), glu_p3], axis=1)[:, s:]
    return new_hp, new_hs, st_p, st_s


def _rope_tables(pos):
    inv = ROPE_THETA ** (-jnp.arange(0, MLA_ROPE, 2, dtype=F32) / MLA_ROPE)
    ang = pos.astype(F32)[:, None] * inv[None, :]
    cos, sin = jnp.cos(ang), jnp.sin(ang)
    reps = LANES // MLA_ROPE
    return jnp.tile(jnp.concatenate([cos, cos], axis=1), (1, reps)), jnp.tile(jnp.concatenate([-sin, sin], axis=1), (1, reps))


def _swap_halves(w):
    k = w.shape[0]
    w4 = w.reshape(k, -1, 2, MLA_ROPE // 2)
    return w4[:, :, ::-1, :].reshape(k, -1)


def _mla_layer(hp, hs, g_mix, ckv_pool, kpe_pool, page_table, w_dq, g_q, w_uq, w_dkv, g_kv, w_uk, w_uv, w_o, bp, s):
    nb = page_table.shape[0]
    t = hs.shape[0] // nb
    past = page_table.shape[1] * ckv_pool.shape[1]
    qr = w_dq.shape[1]
    uq = w_uq.reshape(qr, MLA_HEADS, MLA_NOPE + MLA_ROPE)
    uq_nope = uq[:, :, :MLA_NOPE].reshape(qr, -1)
    uq_rope = uq[:, :, MLA_NOPE:].reshape(qr, -1)
    dkv_rope = w_dkv[:, MLA_KV_RANK:]
    w = {"dq": w_dq.astype(BF16), "gq": g_q,
         "uq": jnp.concatenate([uq_nope, uq_rope, _swap_halves(uq_rope)], axis=1).astype(BF16),
         "dkv": jnp.concatenate([w_dkv[:, :MLA_KV_RANK], dkv_rope, _swap_halves(dkv_rope)], axis=1).astype(BF16),
         "gkv": g_kv, "uk": w_uk.astype(BF16)}
    w_uv, w_o = w_uv.astype(BF16), w_o.astype(BF16)

    tm_p = _row_tile(s, 256)
    cos_p, sin_p = _rope_tables(jnp.arange(s))
    q_p, kcat_p, ckv_p, kpe_p = mla_in(hp, g_mix, w, cos_p, sin_p, lambda i: i % (s // tm_p), tm=tm_p)
    o_p = mla_attn_prompt(q_p, kcat_p, bp, s)
    new_hp = mla_out(o_p, w_uv, w_o, hp)

    ms = hs.shape[0]
    tm_s = _row_tile(ms, 256)
    assert tm_s % t == 0
    cos_s, sin_s = _rope_tables(past + jnp.arange(t))
    cos_s, sin_s = jnp.tile(cos_s, (tm_s // t, 1)), jnp.tile(sin_s, (tm_s // t, 1))
    q_s, kcat_s, ckv_s, kpe_s = mla_in(hs, g_mix, w, cos_s, sin_s, lambda i: 0, tm=tm_s)
    q_rows = q_s.reshape(MLA_HEADS, nb, t, MLA_QK).transpose(1, 0, 2, 3).reshape(nb, MLA_HEADS * t, MLA_QK)
    o_s = mla_attn_sample(q_rows, kcat_s.reshape(nb, t, MLA_QK), ckv_pool, jnp.swapaxes(kpe_pool, 1, 2), page_table)
    o_s = o_s.reshape(nb, MLA_HEADS, t, MLA_KV_RANK).transpose(1, 0, 2, 3).reshape(MLA_HEADS, ms, MLA_KV_RANK)
    new_hs = mla_out(o_s, w_uv, w_o, hs)
    return (new_hp, new_hs, ckv_p.reshape(bp, s, -1), kpe_p.reshape(bp, s, -1),
            ckv_s.reshape(nb, t, -1), kpe_s.reshape(nb, t, -1))


def _ssd_layer(hp, hs, g_mix, conv_state, ssm_state, w_in, w_conv, b_conv, dt_bias, a_log, d_skip, g_norm, w_out,
               bp, s):
    nb = conv_state.shape[0]
    t = hs.shape[0] // nb
    inner = g_norm.shape[0]
    conv_dim = w_conv.shape[1]
    nh = dt_bias.shape[0]
    hist = SSD_CONV - 1
    ws = [w_in[:, :inner].astype(BF16), w_in[:, inner:inner + conv_dim].astype(BF16),
          jnp.pad(w_in[:, inner + conv_dim:], ((0, 0), (0, LANES - nh))).astype(BF16)]
    w_out = w_out.astype(BF16)
    prm = (w_conv, b_conv, dt_bias, a_log, d_skip, g_norm)

    def run(h, b, l, h0, hist_rows):
        z, xbc, dt = norm_matmul(h, g_mix, ws)
        xbc3 = xbc.reshape(b, l, conv_dim)
        hist8 = jnp.pad(hist_rows, ((0, 0), (SSD_TAIL - hist, 0), (0, 0)))
        y, h_fin = ssd_core(xbc3, z.reshape(b, l, inner), dt.reshape(b, l, LANES), h0, hist8, *prm)
        new_h = matmul_residual(y.reshape(b * l, inner), w_out, h)
        new_hist = jnp.concatenate([hist_rows, xbc3], axis=1)[:, l:]
        return new_h, new_hist, h_fin.reshape(b, nh, SSD_HEAD_DIM, SSD_STATE)

    zero_state = jnp.zeros((bp, nh * SSD_HEAD_DIM, SSD_STATE), F32)
    new_hp, hc_p, hh_p = run(hp, bp, s, zero_state, jnp.zeros((bp, hist, conv_dim), F32))
    new_hs, hc_s, hh_s = run(hs, nb, t, ssm_state.reshape(nb, nh * SSD_HEAD_DIM, SSD_STATE), conv_state)
    return new_hp, new_hs, hc_p, hh_p, hc_s, hh_s


def _dil_layer(hp, hs, g_mix, bufs_in, w_qkv, w_o, rel_bias, bp, s):
    nb = bufs_in[0].shape[0]
    t = hs.shape[0] // nb
    ng = len(DIL_PATTERNS)
    half = DIL_HEADS_PER_GROUP * DIL_HEAD_DIM
    w_qkv, w_o = w_qkv.astype(BF16), w_o.astype(BF16)
    (qkv_p,) = norm_matmul(hp, g_mix, [w_qkv])
    (qkv_s,) = norm_matmul(hs, g_mix, [w_qkv])
    qkv_p3 = qkv_p.reshape(bp, s, 3 * ng * half)
    qkv_s3 = qkv_s.reshape(nb, t, 3 * ng * half)
    nh, hd = DIL_HEADS_PER_GROUP, DIL_HEAD_DIM

    outs_p, lses_p, bufs_p, outs_s, lses_s, bufs_s = [], [], [], [], [], []
    for g, (win, r) in enumerate(DIL_PATTERNS):
        nk = win // r
        bias = _group_bias(rel_bias, g, r, nk)
        o, lse = dil_attn_prompt(qkv_p3, g, ng, r, _band_bias(bias, nk), nk, half)
        outs_p.append(o)
        lses_p.append(lse)
        keep = min(win, s)
        st = dil_state_prompt(qkv_p3, g, ng, keep, half)
        bufs_p.append(st.reshape(bp, 2, nh, hd, keep).transpose(0, 4, 1, 2, 3))
        buf = bufs_in[g]
        wb = buf.shape[1]
        buf_t = buf.transpose(0, 2, 3, 4, 1).reshape(nb, 2 * half, wb)
        nbuf_t, o, lse = dil_attn_sample(buf_t, qkv_s3, g, bias, r, nk)
        outs_s.append(o.reshape(nb * t, half))
        lses_s.append(lse.reshape(nb * t, half))
        bufs_s.append(nbuf_t.reshape(nb, 2, nh, hd, wb).transpose(0, 4, 1, 2, 3))
    new_hp = dil_combine_out(outs_p, lses_p, w_o, hp)
    new_hs = dil_combine_out(outs_s, lses_s, w_o, hs)
    return new_hp, new_hs, bufs_p, bufs_s


def kernel(x_prompt, x_sample, state_conv, cache_mla_ckv, cache_mla_kpe, state_ssd_conv, state_ssd, state_dil0_kv, state_dil1_kv, state_dil2_kv, page_table, p_prompt, p_sample, norm_mix, norm_ffn, norm_ple, norm_final, conv_w_in, conv_w_dw, conv_b_dw, conv_ln_g, conv_ln_b, conv_w_out, mla_w_dq, mla_g_q, mla_w_uq, mla_w_dkv, mla_g_kv, mla_w_uk, mla_w_uv, mla_w_o, ssd_w_in, ssd_w_conv, ssd_b_conv, ssd_dt_bias, ssd_a_log, ssd_d, ssd_g_norm, ssd_w_out, dil_w_qkv, dil_w_o, rel_bias, ffn_w1, ffn_w2, ple_w_gate, ple_w_proj):
    bp, s, d = x_prompt.shape
    nb, t, _ = x_sample.shape
    depth = norm_mix.shape[0]
    hp = x_prompt.reshape(bp * s, d)
    hs = x_sample.reshape(nb * t, d)
    conv_p, conv_s = [], []
    ckv_p, kpe_p, ckv_s, kpe_s = [], [], [], []
    ssdc_p, ssdh_p, ssdc_s, ssdh_s = [], [], [], []
    dil_p, dil_s = [[], [], []], [[], [], []]
    dil_in = (state_dil0_kv, state_dil1_kv, state_dil2_kv)
    for i in range(depth):
        kind, j = i % 4, i // 4
        if kind == 0:
            hp, hs, st_p, st_s = _conv_layer(hp, hs, norm_mix[i], state_conv[j], conv_w_in[j], conv_w_dw[j],
                                             conv_b_dw[j], conv_ln_g[j], conv_ln_b[j], conv_w_out[j], bp, s)
            conv_p.append(st_p)
            conv_s.append(st_s)
        elif kind == 1:
            hp, hs, c_p, r_p, c_s, r_s = _mla_layer(hp, hs, norm_mix[i], cache_mla_ckv[j], cache_mla_kpe[j], page_table,
                                                    mla_w_dq[j], mla_g_q[j], mla_w_uq[j], mla_w_dkv[j], mla_g_kv[j],
                                                    mla_w_uk[j], mla_w_uv[j], mla_w_o[j], bp, s)
            ckv_p.append(c_p)
            kpe_p.append(r_p)
            ckv_s.append(c_s)
            kpe_s.append(r_s)
        elif kind == 2:
            hp, hs, hc_p, hh_p, hc_s, hh_s = _ssd_layer(hp, hs, norm_mix[i], state_ssd_conv[j], state_ssd[j], ssd_w_in[j],
                                                        ssd_w_conv[j], ssd_b_conv[j], ssd_dt_bias[j], ssd_a_log[j],
                                                        ssd_d[j], ssd_g_norm[j], ssd_w_out[j], bp, s)
            ssdc_p.append(hc_p)
            ssdh_p.append(hh_p)
            ssdc_s.append(hc_s)
            ssdh_s.append(hh_s)
        else:
            hp, hs, bufs_p, bufs_s = _dil_layer(hp, hs, norm_mix[i], [b[j] for b in dil_in], dil_w_qkv[j], dil_w_o[j],
                                                rel_bias, bp, s)
            for g in range(len(DIL_PATTERNS)):
                dil_p[g].append(bufs_p[g])
                dil_s[g].append(bufs_s[g])
        final = i == depth - 1
        w1, w2 = ffn_w1[i].astype(BF16), ffn_w2[i].astype(BF16)
        wg, wp = ple_w_gate[i].astype(BF16), ple_w_proj[i].astype(BF16)
        hp = ffn(hp, norm_ffn[i], w1, w2)
        hs = ffn(hs, norm_ffn[i], w1, w2)
        hp = ple(hp, p_prompt[i].reshape(bp * s, -1), norm_ple[i], wg, wp, norm_final, final)
        hs = ple(hs, p_sample[i].reshape(nb * t, -1), norm_ple[i], wg, wp, norm_final, final)
    return (hp.reshape(bp, s, d), hs.reshape(nb, t, d),
            jnp.stack(conv_p), jnp.stack(conv_s),
            jnp.stack(ckv_p), jnp.stack(kpe_p), jnp.stack(ckv_s), jnp.stack(kpe_s),
            jnp.stack(ssdc_p), jnp.stack(ssdh_p), jnp.stack(ssdc_s), jnp.stack(ssdh_s),
            jnp.stack(dil_p[0]), jnp.stack(dil_p[1]), jnp.stack(dil_p[2]),
            jnp.stack(dil_s[0]), jnp.stack(dil_s[1]), jnp.stack(dil_s[2]))
```

```python
import functools
import math

import jax
import jax.numpy as jnp
from jax import lax
from jax.experimental import pallas as pl
from jax.experimental.pallas import tpu as pltpu

F32 = jnp.float32
BF16 = jnp.bfloat16
EPS = 1e-6

LANES = 128
SUBLANES = 8
VMEM_LIMIT_BYTES = 56 * 1024 * 1024

PAGE_SIZE = 128
CONV_WIDTH = 31
MLA_HEADS = 8
MLA_NOPE = 128
MLA_ROPE = 64
MLA_KV_RANK = 256
MLA_SCALE = (MLA_NOPE + MLA_ROPE) ** -0.5
ROPE_THETA = 10000.0
SSD_HEAD_DIM = 64
SSD_GROUPS = 4
SSD_STATE = 128
SSD_CONV = 4
SSD_CHUNK = 128
DIL_PATTERNS = ((128, 1), (512, 4), (2048, 16))
DIL_HEADS_PER_GROUP = 8
DIL_HEAD_DIM = 64
DIL_SCALE = DIL_HEAD_DIM ** -0.5
REL_BUCKETS = 32
REL_MAX_DIST = 2048


def _params(*sem):
    return pltpu.CompilerParams(dimension_semantics=sem, vmem_limit_bytes=VMEM_LIMIT_BYTES)


def _resident(shape):
    zeros = (0,) * len(shape)
    return pl.BlockSpec(shape, lambda *_: zeros)


def _rms(x, g):
    return x * lax.rsqrt(jnp.mean(x * x, axis=-1, keepdims=True) + EPS) * g


def _silu(x):
    return x * jax.nn.sigmoid(x)


def _dot(a, b):
    return jnp.dot(a, b, preferred_element_type=F32)


def _dot_nt(a, b):
    return lax.dot_general(a, b, (((1,), (1,)), ((), ())), preferred_element_type=F32)


def _split3(x):
    p1 = x.astype(BF16)
    r1 = x - p1.astype(F32)
    p2 = r1.astype(BF16)
    p3 = (r1 - p2.astype(F32)).astype(BF16)
    return p1, p2, p3


def _dot_sel_lhs(sel, x):
    p1, p2, p3 = _split3(x)
    return _dot(sel, p1) + _dot(sel, p2) + _dot(sel, p3)


def _dot_sel_rhs(x, sel):
    p1, p2, p3 = _split3(x)
    return _dot(p1, sel) + _dot(p2, sel) + _dot(p3, sel)


def _row_tile(m, want):
    t = min(m, want)
    assert m % t == 0, (m, t)
    return t


def _norm_matmul_kernel(x_ref, g_ref, *refs, n_w, chunk):
    w_refs, o_refs = refs[:n_w], refs[n_w:]
    xn = _rms(x_ref[...], g_ref[...]).astype(BF16)
    for w_ref, o_ref in zip(w_refs, o_refs):
        n = w_ref.shape[1]
        for c0 in range(0, n, chunk):
            c1 = min(c0 + chunk, n)
            o_ref[:, c0:c1] = _dot(xn, w_ref[:, c0:c1]).astype(o_ref.dtype)


def norm_matmul(x, g, ws, tm=256, chunk=512):
    m, k = x.shape
    tm = _row_tile(m, tm)
    return pl.pallas_call(
        functools.partial(_norm_matmul_kernel, n_w=len(ws), chunk=chunk),
        grid=(m // tm,),
        in_specs=[pl.BlockSpec((tm, k), lambda i: (i, 0)), _resident((1, k))]
        + [_resident(w.shape) for w in ws],
        out_specs=[pl.BlockSpec((tm, w.shape[1]), lambda i: (i, 0)) for w in ws],
        out_shape=[jax.ShapeDtypeStruct((m, w.shape[1]), F32) for w in ws],
        compiler_params=_params("parallel"),
        name="norm_matmul",
    )(x, g.reshape(1, k), *ws)


def _matmul_residual_kernel(a_ref, w_ref, h_ref, o_ref):
    o_ref[...] = h_ref[...] + _dot(a_ref[...].astype(BF16), w_ref[...])


def matmul_residual(a, w, h, tm=512):
    m, k = a.shape
    n = w.shape[1]
    tm = _row_tile(m, tm)
    return pl.pallas_call(
        _matmul_residual_kernel,
        grid=(m // tm,),
        in_specs=[pl.BlockSpec((tm, k), lambda i: (i, 0)), _resident(w.shape),
                  pl.BlockSpec((tm, n), lambda i: (i, 0))],
        out_specs=pl.BlockSpec((tm, n), lambda i: (i, 0)),
        out_shape=jax.ShapeDtypeStruct((m, n), F32),
        compiler_params=_params("parallel"),
        name="matmul_residual",
    )(a, w, h)


def _ffn_kernel(x_ref, g_ref, w1_ref, w2_ref, o_ref, xn_scr, acc_scr):
    j = pl.program_id(1)

    @pl.when(j == 0)
    def _():
        xn_scr[...] = _rms(x_ref[...], g_ref[...]).astype(BF16)
        acc_scr[...] = jnp.zeros_like(acc_scr)

    a = _dot(xn_scr[...], w1_ref[...])
    a = jnp.square(jnp.maximum(a, 0.0)).astype(BF16)
    acc_scr[...] += _dot(a, w2_ref[...])

    @pl.when(j == pl.num_programs(1) - 1)
    def _():
        o_ref[...] = x_ref[...] + acc_scr[...]


def ffn(x, g, w1, w2, tm=512, tf=1024):
    m, d = x.shape
    f = w1.shape[1]
    tm = _row_tile(m, tm)
    return pl.pallas_call(
        _ffn_kernel,
        grid=(m // tm, f // tf),
        in_specs=[pl.BlockSpec((tm, d), lambda i, j: (i, 0)), _resident((1, d)),
                  pl.BlockSpec((d, tf), lambda i, j: (0, j)),
                  pl.BlockSpec((tf, d), lambda i, j: (j, 0))],
        out_specs=pl.BlockSpec((tm, d), lambda i, j: (i, 0)),
        out_shape=jax.ShapeDtypeStruct((m, d), F32),
        scratch_shapes=[pltpu.VMEM((tm, d), BF16), pltpu.VMEM((tm, d), F32)],
        compiler_params=_params("parallel", "arbitrary"),
        name="ffn",
    )(x, g.reshape(1, d), w1, w2)


def _ple_kernel(x_ref, p_ref, g_ref, wg_ref, wp_ref, gf_ref, o_ref, *, final):
    x = x_ref[...]
    xn = _rms(x, g_ref[...]).astype(BF16)
    gate = jax.nn.sigmoid(_dot(xn, wg_ref[...]))
    y = x + gate * _dot(p_ref[...].astype(BF16), wp_ref[...])
    if final:
        y = _rms(y, gf_ref[...])
    o_ref[...] = y


def ple(x, p, g, wg, wp, g_final, final, tm=512):
    m, d = x.shape
    pd = p.shape[1]
    tm = _row_tile(m, tm)
    return pl.pallas_call(
        functools.partial(_ple_kernel, final=final),
        grid=(m // tm,),
        in_specs=[pl.BlockSpec((tm, d), lambda i: (i, 0)), pl.BlockSpec((tm, pd), lambda i: (i, 0)),
                  _resident((1, d)), _resident(wg.shape), _resident(wp.shape), _resident((1, d))],
        out_specs=pl.BlockSpec((tm, d), lambda i: (i, 0)),
        out_shape=jax.ShapeDtypeStruct((m, d), F32),
        compiler_params=_params("parallel"),
        name="ple",
    )(x, p, g.reshape(1, d), wg, wp, g_final.reshape(1, d))


def _conv_in_kernel(x_ref, g_ref, w_ref, o_ref):
    xn = _rms(x_ref[...], g_ref[...]).astype(BF16)
    d = o_ref.shape[1]
    o_ref[...] = _dot(xn, w_ref[:, :d]) * jax.nn.sigmoid(_dot(xn, w_ref[:, d:]))


def conv_in(x, g, w_in, tm=512):
    m, d = x.shape
    tm = _row_tile(m, tm)
    return pl.pallas_call(
        _conv_in_kernel,
        grid=(m // tm,),
        in_specs=[pl.BlockSpec((tm, d), lambda i: (i, 0)), _resident((1, d)), _resident(w_in.shape)],
        out_specs=pl.BlockSpec((tm, d), lambda i: (i, 0)),
        out_shape=jax.ShapeDtypeStruct((m, d), F32),
        compiler_params=_params("parallel"),
        name="conv_in",
    )(x, g.reshape(1, d), w_in)


def _ln_silu_out(c, h, lng_ref, lnb_ref, wo_ref):
    xc = c - jnp.mean(c, axis=-1, keepdims=True)
    y = xc * lax.rsqrt(jnp.mean(xc * xc, axis=-1, keepdims=True) + EPS) * lng_ref[...] + lnb_ref[...]
    return h + _dot(_silu(y).astype(BF16), wo_ref[...])


CONV_HALO = 32


def _conv_prompt_kernel(cur_ref, halo_ref, h_ref, wdw_ref, bdw_ref, lng_ref, lnb_ref, wo_ref, o_ref,
                        full_scr, *, ts, width):
    i = pl.program_id(1)
    d = cur_ref.shape[2]
    full_scr[0:CONV_HALO] = jnp.where(i > 0, halo_ref[0], 0.0)
    full_scr[CONV_HALO:CONV_HALO + ts] = cur_ref[0]
    off = CONV_HALO - (width - 1)
    cols = []
    for c in range(d // LANES):
        cs = slice(c * LANES, (c + 1) * LANES)
        fc = full_scr[:, cs]
        acc = jnp.broadcast_to(bdw_ref[:, cs], (ts, LANES))
        for phase in range(SUBLANES):
            taps = [k for k in range(width) if (off + k) % SUBLANES == phase]
            if not taps:
                continue
            shifted = pltpu.roll(fc, fc.shape[0] - phase, axis=0) if phase else fc
            for k in taps:
                base = off + k - phase
                acc = acc + shifted[base:base + ts] * wdw_ref[k:k + 1, cs]
        cols.append(acc)
    o_ref[0] = _ln_silu_out(jnp.concatenate(cols, axis=-1), h_ref[0], lng_ref, lnb_ref, wo_ref)


def conv_core_prompt(glu, h, w_dw, b_dw, ln_g, ln_b, w_out, ts=256):
    b, s, d = glu.shape
    ts = _row_tile(s, ts)
    width = w_dw.shape[0]
    assert width - 1 <= CONV_HALO and ts % CONV_HALO == 0
    per = ts // CONV_HALO
    vec = lambda v: v.reshape(1, d)
    return pl.pallas_call(
        functools.partial(_conv_prompt_kernel, ts=ts, width=width),
        grid=(b, s // ts),
        in_specs=[pl.BlockSpec((1, ts, d), lambda bi, i: (bi, i, 0)),
                  pl.BlockSpec((1, CONV_HALO, d), lambda bi, i: (bi, jnp.maximum(i * per - 1, 0), 0)),
                  pl.BlockSpec((1, ts, d), lambda bi, i: (bi, i, 0)),
                  _resident(w_dw.shape), _resident((1, d)), _resident((1, d)), _resident((1, d)),
                  _resident(w_out.shape)],
        out_specs=pl.BlockSpec((1, ts, d), lambda bi, i: (bi, i, 0)),
        out_shape=jax.ShapeDtypeStruct((b, s, d), F32),
        scratch_shapes=[pltpu.VMEM((CONV_HALO + ts, d), F32)],
        compiler_params=_params("parallel", "arbitrary"),
        name="conv_core_prompt",
    )(glu, glu, h, w_dw, vec(b_dw), vec(ln_g), vec(ln_b), w_out)


def _conv_sample_kernel(glu_ref, st_ref, h_ref, wdw_ref, bdw_ref, lng_ref, lnb_ref, wo_ref, o_ref, nst_ref,
                        full_scr, c_scr, *, bb, t, width):
    hist = width - 1
    d = glu_ref.shape[1]
    for bi in range(bb):
        full_scr[0:hist] = st_ref[bi]
        full_scr[hist:hist + t] = glu_ref[bi * t:(bi + 1) * t, :]
        acc = jnp.broadcast_to(bdw_ref[...], (t, d))
        for k in range(width):
            acc = acc + full_scr[k:k + t, :] * wdw_ref[k:k + 1, :]
        c_scr[bi * t:(bi + 1) * t, :] = acc
        nst_ref[bi] = full_scr[t:t + hist]
    o_ref[...] = _ln_silu_out(c_scr[...], h_ref[...], lng_ref, lnb_ref, wo_ref)


def conv_core_sample(glu, state, h, w_dw, b_dw, ln_g, ln_b, w_out, bb=8):
    nb, hist, d = state.shape
    t = glu.shape[0] // nb
    width = w_dw.shape[0]
    assert hist == width - 1 and nb % bb == 0
    vec = lambda v: v.reshape(1, d)
    return pl.pallas_call(
        functools.partial(_conv_sample_kernel, bb=bb, t=t, width=width),
        grid=(nb // bb,),
        in_specs=[pl.BlockSpec((bb * t, d), lambda i: (i, 0)),
                  pl.BlockSpec((bb, hist, d), lambda i: (i, 0, 0)),
                  pl.BlockSpec((bb * t, d), lambda i: (i, 0)),
                  _resident(w_dw.shape), _resident((1, d)), _resident((1, d)), _resident((1, d)),
                  _resident(w_out.shape)],
        out_specs=[pl.BlockSpec((bb * t, d), lambda i: (i, 0)),
                   pl.BlockSpec((bb, hist, d), lambda i: (i, 0, 0))],
        out_shape=[jax.ShapeDtypeStruct((nb * t, d), F32), jax.ShapeDtypeStruct((nb, hist, d), F32)],
        scratch_shapes=[pltpu.VMEM((hist + t + SUBLANES, d), F32), pltpu.VMEM((bb * t, d), F32)],
        compiler_params=_params("parallel"),
        name="conv_core_sample",
    )(glu, state, h, w_dw, vec(b_dw), vec(ln_g), vec(ln_b), w_out)


MLA_QK = MLA_KV_RANK + MLA_ROPE


def _mla_in_kernel(x_ref, g_ref, wdq_ref, gq_ref, wuq_ref, wdkv_ref, gkv_ref, wuk_ref, cos_ref, sin_ref,
                   q_ref, kcat_ref, ckv_ref, kpe_ref):
    nope_w = MLA_HEADS * MLA_NOPE
    rope_w = MLA_HEADS * MLA_ROPE
    xn = _rms(x_ref[...], g_ref[...]).astype(BF16)
    cq = _rms(_dot(xn, wdq_ref[...]), gq_ref[...]).astype(BF16)
    kv = _dot(xn, wdkv_ref[...])
    ckv = _rms(kv[:, :MLA_KV_RANK], gkv_ref[...])
    cos, sin = cos_ref[...], sin_ref[...]
    kpe = (kv[:, MLA_KV_RANK:MLA_QK] * cos[:, :MLA_ROPE]
           + kv[:, MLA_QK:MLA_QK + MLA_ROPE] * sin[:, :MLA_ROPE])
    ckv_ref[...] = ckv
    kpe_ref[...] = kpe
    kcat_ref[:, :MLA_KV_RANK] = ckv.astype(BF16)
    kcat_ref[:, MLA_KV_RANK:] = kpe.astype(BF16)
    qp = _dot(cq, wuq_ref[:, nope_w:nope_w + rope_w])
    qps = _dot(cq, wuq_ref[:, nope_w + rope_w:])
    per = LANES // MLA_ROPE
    for c in range(rope_w // LANES):
        cs = slice(c * LANES, (c + 1) * LANES)
        roped = ((qp[:, cs] * cos + qps[:, cs] * sin) * MLA_SCALE).astype(BF16)
        for hh in range(per):
            q_ref[c * per + hh, :, MLA_KV_RANK:] = roped[:, hh * MLA_ROPE:(hh + 1) * MLA_ROPE]
    for h in range(MLA_HEADS):
        qn = _dot(cq, wuq_ref[:, h * MLA_NOPE:(h + 1) * MLA_NOPE]).astype(BF16)
        q_ref[h, :, :MLA_KV_RANK] = (_dot(qn, wuk_ref[h]) * MLA_SCALE).astype(BF16)


def mla_in(x, g, w, cos_tab, sin_tab, tab_index, tm=256):
    m, d = x.shape
    tm = _row_tile(m, tm)
    assert cos_tab.shape[0] % tm == 0 or cos_tab.shape[0] == tm
    return pl.pallas_call(
        _mla_in_kernel,
        grid=(m // tm,),
        in_specs=[pl.BlockSpec((tm, d), lambda i: (i, 0)), _resident((1, d)),
                  _resident(w["dq"].shape), _resident((1, w["dq"].shape[1])), _resident(w["uq"].shape),
                  _resident(w["dkv"].shape), _resident((1, MLA_KV_RANK)), _resident(w["uk"].shape),
                  pl.BlockSpec((tm, LANES), lambda i: (tab_index(i), 0)),
                  pl.BlockSpec((tm, LANES), lambda i: (tab_index(i), 0))],
        out_specs=[pl.BlockSpec((MLA_HEADS, tm, MLA_QK), lambda i: (0, i, 0)),
                   pl.BlockSpec((tm, MLA_QK), lambda i: (i, 0)),
                   pl.BlockSpec((tm, MLA_KV_RANK), lambda i: (i, 0)),
                   pl.BlockSpec((tm, MLA_ROPE), lambda i: (i, 0))],
        out_shape=[jax.ShapeDtypeStruct((MLA_HEADS, m, MLA_QK), BF16),
                   jax.ShapeDtypeStruct((m, MLA_QK), BF16),
                   jax.ShapeDtypeStruct((m, MLA_KV_RANK), F32),
                   jax.ShapeDtypeStruct((m, MLA_ROPE), F32)],
        compiler_params=_params("parallel"),
        name="mla_in",
    )(x, g.reshape(1, d), w["dq"], w["gq"].reshape(1, -1), w["uq"], w["dkv"], w["gkv"].reshape(1, -1),
      w["uk"], cos_tab, sin_tab)


def _mla_attn_kernel(q_ref, k_ref, o_ref, m_scr, l_scr, acc_scr, *, tq, tk):
    i, j = pl.program_id(1), pl.program_id(2)
    nh = q_ref.shape[0]
    last_j = ((i + 1) * tq - 1) // tk

    @pl.when(j == 0)
    def _():
        m_scr[...] = jnp.full_like(m_scr, -jnp.inf)
        l_scr[...] = jnp.zeros_like(l_scr)
        acc_scr[...] = jnp.zeros_like(acc_scr)

    def step(masked):
        k = k_ref[...]
        kv = k[:, :MLA_KV_RANK]
        state = [(m_scr[h], l_scr[h], acc_scr[h]) for h in range(nh)]
        scores = [_dot_nt(q_ref[h], k) for h in range(nh)]
        if masked:
            row = lax.broadcasted_iota(jnp.int32, (tq, tk), 0)
            col = lax.broadcasted_iota(jnp.int32, (tq, tk), 1)
            keep = col + j * tk <= row + i * tq
        new_state = []
        for h in range(nh):
            s = jnp.where(keep, scores[h], -jnp.inf) if masked else scores[h]
            m_prev, l_prev, acc_prev = state[h]
            m_new = jnp.maximum(m_prev, jnp.max(s, axis=-1, keepdims=True))
            alpha = jnp.exp(m_prev - m_new)
            p = jnp.exp(s - m_new)
            l_new = alpha * l_prev + jnp.sum(p, axis=-1, keepdims=True)
            new_state.append((m_new, l_new, alpha * acc_prev + _dot(p.astype(BF16), kv)))
        for h in range(nh):
            m_scr[h], l_scr[h], acc_scr[h] = new_state[h]

    crosses_diagonal = j * tk + tk - 1 > i * tq

    @pl.when((j <= last_j) & jnp.logical_not(crosses_diagonal))
    def _():
        step(False)

    @pl.when((j <= last_j) & crosses_diagonal)
    def _():
        step(True)

    @pl.when(j == last_j)
    def _():
        o_ref[...] = (acc_scr[...] / l_scr[...]).astype(o_ref.dtype)


def mla_attn_prompt(q, kcat, b, s, tq=256, tk=1024):
    nh = q.shape[0]
    tq, tk = _row_tile(s, tq), _row_tile(s, tk)
    assert tq & (tq - 1) == 0
    nq, nk = s // tq, s // tk

    def k_index(bi, i, j):
        return (bi * nk + jnp.minimum(j, ((i + 1) * tq - 1) // tk), 0)

    return pl.pallas_call(
        functools.partial(_mla_attn_kernel, tq=tq, tk=tk),
        grid=(b, nq, nk),
        in_specs=[pl.BlockSpec((nh, tq, MLA_QK), lambda bi, i, j: (0, bi * nq + i, 0)),
                  pl.BlockSpec((tk, MLA_QK), k_index)],
        out_specs=pl.BlockSpec((nh, tq, MLA_KV_RANK), lambda bi, i, j: (0, bi * nq + i, 0)),
        out_shape=jax.ShapeDtypeStruct((nh, b * s, MLA_KV_RANK), BF16),
        scratch_shapes=[pltpu.VMEM((nh, tq, 1), F32), pltpu.VMEM((nh, tq, 1), F32),
                        pltpu.VMEM((nh, tq, MLA_KV_RANK), F32)],
        compiler_params=_params("parallel", "parallel", "arbitrary"),
        name="mla_attn_prompt",
    )(q, kcat)


def _mla_sample_kernel(pt_ref, q_ref, knew_ref, ckv_hbm, kpe_hbm, o_ref, ckv_buf, kpe_buf, sems, *, n_pages, t):
    b = pl.program_id(0)
    nb = pl.num_programs(0)
    page = ckv_hbm.shape[1]
    slot = b % 2

    def page_copies(elem, sl, p):
        idx = pt_ref[elem * n_pages + p]
        return (pltpu.make_async_copy(ckv_hbm.at[idx], ckv_buf.at[sl, pl.ds(p * page, page), :], sems.at[sl]),
                pltpu.make_async_copy(kpe_hbm.at[idx], kpe_buf.at[sl, :, pl.ds(p * page, page)], sems.at[sl]))

    def start_fetch(elem, sl):
        for p in range(n_pages):
            for cp in page_copies(elem, sl, p):
                cp.start()

    def wait_fetch(elem, sl):
        for p in range(n_pages):
            for cp in page_copies(elem, sl, p):
                cp.wait()

    @pl.when(b == 0)
    def _():
        start_fetch(0, 0)

    wait_fetch(b, slot)
    nxt = jnp.minimum(b + 1, nb - 1)
    start_fetch(nxt, 1 - slot)

    q = q_ref[0]
    kc = ckv_buf[slot].astype(BF16)
    kr = kpe_buf[slot].astype(BF16)
    s = _dot_nt(q[:, :MLA_KV_RANK], kc) + _dot(q[:, MLA_KV_RANK:], kr)
    m = jnp.max(s, axis=-1, keepdims=True)
    p = jnp.exp(s - m)
    l = jnp.sum(p, axis=-1, keepdims=True)
    acc = _dot(p.astype(BF16), kc)
    qf = q.astype(F32)
    kn = knew_ref[0].astype(F32)
    row_t = lax.broadcasted_iota(jnp.int32, (q.shape[0], 1), 0) % t
    for tk in range(t):
        krow = kn[tk:tk + 1, :]
        s_t = jnp.where(row_t >= tk, jnp.sum(qf * krow, axis=-1, keepdims=True), -jnp.inf)
        m_new = jnp.maximum(m, s_t)
        alpha = jnp.exp(m - m_new)
        p_t = jnp.exp(s_t - m_new)
        l = alpha * l + p_t
        acc = alpha * acc + p_t * krow[:, :MLA_KV_RANK]
        m = m_new
    o_ref[0] = (acc / l).astype(o_ref.dtype)

    @pl.when(b == nb - 1)
    def _():
        wait_fetch(nxt, 1 - slot)


def mla_attn_sample(q, knew, ckv_pool, kpe_pool_t, page_table):
    nb, rows, _ = q.shape
    t = knew.shape[1]
    n_pages = page_table.shape[1]
    page = ckv_pool.shape[1]
    grid_spec = pltpu.PrefetchScalarGridSpec(
        num_scalar_prefetch=1,
        grid=(nb,),
        in_specs=[pl.BlockSpec((1, rows, MLA_QK), lambda bi, pt: (bi, 0, 0)),
                  pl.BlockSpec((1, t, MLA_QK), lambda bi, pt: (bi, 0, 0)),
                  pl.BlockSpec(memory_space=pl.ANY), pl.BlockSpec(memory_space=pl.ANY)],
        out_specs=pl.BlockSpec((1, rows, MLA_KV_RANK), lambda bi, pt: (bi, 0, 0)),
        scratch_shapes=[pltpu.VMEM((2, n_pages * page, MLA_KV_RANK), F32),
                        pltpu.VMEM((2, MLA_ROPE, n_pages * page), F32),
                        pltpu.SemaphoreType.DMA((2,))],
    )
    return pl.pallas_call(
        functools.partial(_mla_sample_kernel, n_pages=n_pages, t=t),
        grid_spec=grid_spec,
        out_shape=jax.ShapeDtypeStruct((nb, rows, MLA_KV_RANK), BF16),
        compiler_params=_params("arbitrary"),
        name="mla_attn_sample",
    )(page_table.reshape(-1), q, knew, ckv_pool, kpe_pool_t)


def _mla_out_kernel(o_ref, wuv_ref, wo_ref, h_ref, y_ref):
    parts = [_dot(o_ref[h], wuv_ref[h]).astype(BF16) for h in range(o_ref.shape[0])]
    y_ref[...] = h_ref[...] + _dot(jnp.concatenate(parts, axis=-1), wo_ref[...])


def mla_out(o, w_uv, w_o, h, tm=512):
    nh, m, c = o.shape
    d = h.shape[1]
    tm = _row_tile(m, tm)
    return pl.pallas_call(
        _mla_out_kernel,
        grid=(m // tm,),
        in_specs=[pl.BlockSpec((nh, tm, c), lambda i: (0, i, 0)), _resident(w_uv.shape), _resident(w_o.shape),
                  pl.BlockSpec((tm, d), lambda i: (i, 0))],
        out_specs=pl.BlockSpec((tm, d), lambda i: (i, 0)),
        out_shape=jax.ShapeDtypeStruct((m, d), F32),
        compiler_params=_params("parallel"),
        name="mla_out",
    )(o, w_uv, w_o, h)


SSD_TAIL = SUBLANES


def _ssd_kernel(xbc_ref, z_ref, dt_ref, h0_ref, hist_ref, wc_ref, bc_ref, dtb_ref, alog_ref, dexp_ref, gn_ref,
                e_ref, et_ref, y_ref, hfin_ref, state_scr, tail_scr, full_scr, z_scr, dt_scr,
                *, q, qb, nh, hd, ng, ns):
    c = pl.program_id(1)
    inner = nh * hd
    hpg = nh // ng
    gw = hpg * hd

    @pl.when(c == 0)
    def _():
        state_scr[...] = h0_ref[0]
        tail_scr[...] = hist_ref[0]

    full_scr[0:SSD_TAIL] = tail_scr[...]
    full_scr[SSD_TAIL:SSD_TAIL + qb] = xbc_ref[0]
    if qb < q:
        full_scr[SSD_TAIL + qb:SSD_TAIL + q] = jnp.zeros((q - qb, full_scr.shape[1]), F32)
        z_scr[0:qb] = z_ref[0]
        z_scr[qb:q] = jnp.zeros((q - qb, inner), F32)
        dt_scr[0:qb] = dt_ref[0]
        dt_scr[qb:q] = jnp.zeros((q - qb, LANES), F32)
        z, dt_raw = z_scr[...], dt_scr[...]
    else:
        tail_scr[...] = full_scr[q:q + SSD_TAIL]
        z, dt_raw = z_ref[0], dt_ref[0]

    off = SSD_TAIL - (SSD_CONV - 1)
    conv = bc_ref[...] + full_scr[off:off + q] * wc_ref[0:1]
    for k in range(1, SSD_CONV):
        conv = conv + full_scr[off + k:off + k + q] * wc_ref[k:k + 1]
    xc = _silu(conv)
    dt = jnp.maximum(dt_raw + dtb_ref[...], 0.0) + jnp.log1p(jnp.exp(-jnp.abs(dt_raw + dtb_ref[...])))
    if qb < q:
        live = lax.broadcasted_iota(jnp.int32, (q, 1), 0) < qb
        xc = jnp.where(live, xc, 0.0)
        dt = jnp.where(live, dt, 0.0)

    la = dt * (-jnp.exp(alog_ref[...]))
    ri = lax.broadcasted_iota(jnp.int32, (q, q), 0)
    ci = lax.broadcasted_iota(jnp.int32, (q, q), 1)
    causal = ri >= ci
    cs = _dot_sel_lhs(causal.astype(BF16), la)
    cs_t = cs.T
    cs_last = cs[q - 1:q, :]
    per_head = jnp.concatenate([dt, jnp.exp(cs), jnp.exp(cs_last - cs)], axis=0)
    spread = _dot_sel_rhs(per_head, e_ref[...])
    dt_x, ecs_x, edec_x = spread[0:q], spread[q:2 * q], spread[2 * q:3 * q]
    chunk_decay = jnp.broadcast_to(jnp.exp(cs_t[:, q - 1:q]), (LANES, LANES))
    decay_rows = _dot_sel_lhs(et_ref[...], chunk_decay)

    xs = xc[:, :inner]
    xdt = xs * dt_x
    xw = xdt * edec_x
    states = [state_scr[g * gw:(g + 1) * gw, :] for g in range(ng)]
    per = LANES // hd
    lane_head = lax.broadcasted_iota(jnp.int32, (1, LANES), 1) // hd
    new_states, y_groups = [], []
    for g in range(ng):
        bg = xc[:, inner + g * ns:inner + (g + 1) * ns].astype(BF16)
        cg = xc[:, inner + ng * ns + g * ns:inner + ng * ns + (g + 1) * ns].astype(BF16)
        cb = _dot_nt(cg, bg)
        rows = slice(g * gw, (g + 1) * gw)
        st = states[g]
        y_off = _dot_nt(cg, st.astype(BF16)) * ecs_x[:, rows]
        new_states.append(decay_rows[rows, :] * st + _dot(xw[:, rows].T.astype(BF16), bg))
        y_diag = []
        for blk in range(gw // LANES):
            lanes = slice(g * gw + blk * LANES, g * gw + (blk + 1) * LANES)
            x_blk = xdt[:, lanes]
            acc = None
            for hh in range(per):
                h = (g * gw + blk * LANES) // hd + hh
                seg = cs[:, h:h + 1] - cs_t[h:h + 1, :]
                mat = (cb * jnp.exp(jnp.where(causal, seg, -jnp.inf))).astype(BF16)
                part = _dot(mat, jnp.where(lane_head == hh, x_blk, 0.0).astype(BF16))
                acc = part if acc is None else acc + part
            y_diag.append(acc)
        y_groups.append(y_off + jnp.concatenate(y_diag, axis=-1))
    for g in range(ng):
        state_scr[g * gw:(g + 1) * gw, :] = new_states[g]

    gated = (jnp.concatenate(y_groups, axis=-1) + xs * dexp_ref[...]) * _silu(z)
    normed = []
    for g in range(ng):
        grp = gated[:, g * gw:(g + 1) * gw]
        normed.append(grp * lax.rsqrt(jnp.mean(grp * grp, axis=-1, keepdims=True) + EPS))
    out = jnp.concatenate(normed, axis=-1) * gn_ref[...]
    y_ref[0] = out[0:qb]

    @pl.when(c == pl.num_programs(1) - 1)
    def _():
        hfin_ref[0] = state_scr[...]


def ssd_core(xbc, z, dt, h0, hist, w_conv, b_conv, dt_bias, a_log, d_skip, g_norm):
    b, l, conv_dim = xbc.shape
    inner = z.shape[2]
    nh = inner // SSD_HEAD_DIM
    q = SSD_CHUNK
    qb = min(l, q)
    assert l % qb == 0 and (qb == q or l == qb)
    nc = l // qb
    hp = nh * SSD_HEAD_DIM
    head_of_lane = jnp.arange(inner) // SSD_HEAD_DIM
    e = (jnp.arange(LANES)[:, None] == head_of_lane[None, :]).astype(BF16)
    pad = lambda v: jnp.pad(v.astype(F32), (0, LANES - nh)).reshape(1, LANES)
    row = lambda v: v.astype(F32).reshape(1, -1)
    blk = lambda w: pl.BlockSpec((1, qb, w), lambda bi, ci: (bi, ci, 0))
    per_b = lambda r, w: pl.BlockSpec((1, r, w), lambda bi, ci: (bi, 0, 0))
    return pl.pallas_call(
        functools.partial(_ssd_kernel, q=q, qb=qb, nh=nh, hd=SSD_HEAD_DIM, ng=SSD_GROUPS, ns=SSD_STATE),
        grid=(b, nc),
        in_specs=[blk(conv_dim), blk(inner), blk(LANES), per_b(hp, SSD_STATE), per_b(SSD_TAIL, conv_dim),
                  _resident((SSD_CONV, conv_dim)), _resident((1, conv_dim)), _resident((1, LANES)),
                  _resident((1, LANES)), _resident((1, inner)), _resident((1, inner)),
                  _resident((LANES, inner)), _resident((inner, LANES))],
        out_specs=[blk(inner), per_b(hp, SSD_STATE)],
        out_shape=[jax.ShapeDtypeStruct((b, l, inner), F32), jax.ShapeDtypeStruct((b, hp, SSD_STATE), F32)],
        scratch_shapes=[pltpu.VMEM((hp, SSD_STATE), F32), pltpu.VMEM((SSD_TAIL, conv_dim), F32),
                        pltpu.VMEM((SSD_TAIL + q, conv_dim), F32), pltpu.VMEM((q, inner), F32),
                        pltpu.VMEM((q, LANES), F32)],
        compiler_params=_params("parallel", "arbitrary"),
        name="ssd_core",
    )(xbc, z, dt, h0, hist, w_conv.astype(F32), row(b_conv), pad(dt_bias), pad(a_log),
      row(jnp.repeat(d_skip, SSD_HEAD_DIM)), row(g_norm), e, e.T)


def _t5_bucket(dist):
    max_exact = REL_BUCKETS // 2
    d = jnp.maximum(dist, 1).astype(F32)
    large = max_exact + (jnp.log(d / max_exact) / math.log(REL_MAX_DIST / max_exact)
                         * (REL_BUCKETS - max_exact)).astype(jnp.int32)
    return jnp.where(dist < max_exact, dist, jnp.minimum(large, REL_BUCKETS - 1))


def _group_bias(rel_bias, g, r, nk):
    tab = rel_bias[_t5_bucket(r * jnp.arange(nk + 1))]
    return tab[:, g * DIL_HEADS_PER_GROUP:(g + 1) * DIL_HEADS_PER_GROUP].T.astype(F32)


def _dil_prompt_kernel(q_ref, kc_ref, kp_ref, vc_ref, vp_ref, bias_ref, o_ref, lse_ref, *, blk, r, hd):
    i = pl.program_id(1)
    per = q_ref.shape[2] // hd
    qi = lax.broadcasted_iota(jnp.int32, (blk, 2 * blk), 0)
    ki = lax.broadcasted_iota(jnp.int32, (blk, 2 * blk), 1)
    dm = qi + blk - ki
    valid = (dm >= 0) & (dm <= blk) & ((ki >= blk) | (i > 0))
    biases = [bias_ref[hh] for hh in range(per)]
    results = []
    for c in range(r):
        rows = pl.ds(c, blk, stride=r) if r > 1 else slice(None)
        q = q_ref[0, rows, :].astype(BF16)
        k = jnp.concatenate([kp_ref[0, rows, :], kc_ref[0, rows, :]], axis=0).astype(BF16)
        v = jnp.concatenate([vp_ref[0, rows, :], vc_ref[0, rows, :]], axis=0).astype(BF16)
        outs, lses = [], []
        for hh in range(per):
            hs = slice(hh * hd, (hh + 1) * hd)
            s = _dot_nt(q[:, hs], k[:, hs]) * DIL_SCALE + biases[hh]
            s = jnp.where(valid, s, -jnp.inf)
            m = jnp.max(s, axis=-1, keepdims=True)
            p = jnp.exp(s - m)
            l = jnp.sum(p, axis=-1, keepdims=True)
            outs.append(_dot(p.astype(BF16), v[:, hs]) / l)
            lses.append(jnp.broadcast_to(m + jnp.log(l), (blk, hd)))
        results.append((rows, jnp.concatenate(outs, axis=-1), jnp.concatenate(lses, axis=-1)))
    for rows, o, lse in results:
        o_ref[0, rows, :] = o
        lse_ref[0, rows, :] = lse


def dil_attn_prompt(qkv, g, ng, r, bias_mat, blk, half):
    b, s, cols = qkv.shape
    nh = bias_mat.shape[0]
    hd = half // nh
    width = LANES if r > 1 else half
    per = width // hd
    tile = blk * r
    assert s % tile == 0 and cols == 3 * ng * half and half % width == 0
    lane_blocks = half // width

    def spec(kind, prev):
        def index(bi, i, hb):
            return (bi, jnp.maximum(i - 1, 0) if prev else i, (kind * ng + g) * lane_blocks + hb)
        return pl.BlockSpec((1, tile, width), index)

    out = pl.BlockSpec((1, tile, width), lambda bi, i, hb: (bi, i, hb))
    o, lse = pl.pallas_call(
        functools.partial(_dil_prompt_kernel, blk=blk, r=r, hd=hd),
        grid=(b, s // tile, lane_blocks),
        in_specs=[spec(0, False), spec(1, False), spec(1, True), spec(2, False), spec(2, True),
                  pl.BlockSpec((per, blk, 2 * blk), lambda bi, i, hb: (hb, 0, 0))],
        out_specs=[out, out],
        out_shape=[jax.ShapeDtypeStruct((b, s, half), F32)] * 2,
        compiler_params=_params("parallel", "arbitrary", "arbitrary"),
        name="dil_attn_prompt",
    )(qkv, qkv, qkv, qkv, qkv, bias_mat)
    return o.reshape(b * s, half), lse.reshape(b * s, half)


def _transpose_rows_kernel(k_ref, v_ref, o_ref):
    half = k_ref.shape[2]
    for c in range(half // LANES):
        cs = slice(c * LANES, (c + 1) * LANES)
        o_ref[0, 0, cs, :] = k_ref[0, :, cs].T
        o_ref[0, 1, cs, :] = v_ref[0, :, cs].T


def dil_state_prompt(qkv, g, ng, keep, half):
    b, s, cols = qkv.shape
    assert keep % LANES == 0 and (s - keep) % LANES == 0
    first = (s - keep) // LANES
    return pl.pallas_call(
        _transpose_rows_kernel,
        grid=(b, keep // LANES),
        in_specs=[pl.BlockSpec((1, LANES, half), lambda bi, i: (bi, first + i, ng + g)),
                  pl.BlockSpec((1, LANES, half), lambda bi, i: (bi, first + i, 2 * ng + g))],
        out_specs=pl.BlockSpec((1, 2, half, LANES), lambda bi, i: (bi, 0, 0, i)),
        out_shape=jax.ShapeDtypeStruct((b, 2, half, keep), F32),
        compiler_params=_params("parallel", "parallel"),
        name="dil_state_prompt",
    )(qkv, qkv)


def _dil_sample_kernel(buf_ref, q_ref, k_ref, v_ref, bias_ref, nbias_ref, hmask_ref, nbuf_ref, o_ref, lse_ref,
                       new_scr, *, eb, t, nh, hd):
    one = lambda ref, e: ref.at[pl.ds(e, 1)]
    for e in range(eb):
        _dil_shift_copy(one(buf_ref, e), one(k_ref, e), one(v_ref, e), one(nbuf_ref, e), new_scr.at[e], t=t)
    results = [_dil_decode_attn(one(buf_ref, e), one(q_ref, e), bias_ref, nbias_ref, hmask_ref, new_scr.at[e],
                                t=t, nh=nh, hd=hd) for e in range(eb)]
    for e, (o, lse) in enumerate(results):
        o_ref[e] = o
        lse_ref[e] = lse


def _dil_shift_copy(buf_ref, k_ref, v_ref, nbuf_ref, new_scr, *, t):
    w = buf_ref.shape[2]
    half = k_ref.shape[2]
    first_new = LANES - t
    new_scr[...] = jnp.zeros(new_scr.shape, F32)
    new_scr[first_new:, :half] = k_ref[0]
    new_scr[first_new:, half:] = v_ref[0]
    lane = lax.broadcasted_iota(jnp.int32, (LANES, LANES), 1)
    for c in range(2 * half // LANES):
        blk = slice(c * LANES, (c + 1) * LANES)
        rolled = pltpu.roll(buf_ref[0, blk, :], w - t, axis=1)
        nbuf_ref[0, blk, :] = rolled
        nbuf_ref[0, blk, w - LANES:] = jnp.where(lane >= first_new, new_scr[:, blk].T, rolled[:, w - LANES:])


def _dil_decode_attn(buf_ref, q_ref, bias_ref, nbias_ref, hmask_ref, new_scr, *, t, nh, hd):
    half = nh * hd
    q8 = jnp.concatenate([q_ref[0], jnp.zeros((SUBLANES - t, half), F32)], axis=0)
    new8 = new_scr[LANES - SUBLANES:, :]
    hmask = hmask_ref[...]
    q_heads = jnp.concatenate([q8] * nh, axis=0) * hmask
    k_t = buf_ref[0, 0:half, :].astype(BF16)
    v_t = buf_ref[0, half:, :].astype(BF16)
    s = _dot(q_heads.astype(BF16), k_t) * DIL_SCALE + bias_ref[...]
    m = jnp.max(s, axis=-1, keepdims=True)
    s_new = []
    for i in range(t):
        row = SUBLANES - t + i
        s_i = (jnp.sum(q_heads * new8[row:row + 1, :half], axis=-1, keepdims=True) * DIL_SCALE
               + nbias_ref[:, i:i + 1])
        s_new.append(s_i)
        m = jnp.maximum(m, s_i)
    p = jnp.exp(s - m)
    l = jnp.sum(p, axis=-1, keepdims=True)
    o = _dot_nt(p.astype(BF16), v_t)
    for i in range(t):
        row = SUBLANES - t + i
        p_i = jnp.exp(s_new[i] - m)
        l = l + p_i
        o = o + p_i * new8[row:row + 1, half:]
    o = (o / l) * hmask
    lse = (m + jnp.log(l)) * hmask
    blocks = lambda a: functools.reduce(lambda x, y: x + y, [a[h * SUBLANES:(h + 1) * SUBLANES] for h in range(nh)])
    return blocks(o)[0:t], blocks(lse)[0:t]


DIL_SAMPLE_BLOCK_BYTES = 4 * 1024 * 1024


def _sample_bias_tables(bias, r, nk, t):
    nh = bias.shape[0]
    w = r * nk
    neg = lambda *shape: jnp.full(shape, -jnp.inf, F32)
    rev = bias[:, nk:0:-1]
    up = rev if r == 1 else jnp.concatenate([rev[:, :, None], neg(nh, nk, r - 1)], axis=2).reshape(nh, w)
    rows = []
    for tt in range(SUBLANES):
        if tt >= t:
            rows.append(neg(nh, w))
        elif tt == 0:
            rows.append(up)
        else:
            rows.append(jnp.concatenate([neg(nh, tt), up[:, :w - tt]], axis=1))
    old = jnp.stack(rows, axis=1)
    cols = []
    for i in range(t):
        col = []
        for tt in range(SUBLANES):
            if tt >= t:
                col.append(jnp.zeros((nh, 1), F32))
            elif i <= tt and (tt - i) % r == 0:
                d = (tt - i) // r
                col.append(bias[:, d:d + 1])
            else:
                col.append(neg(nh, 1))
        cols.append(jnp.concatenate(col, axis=1))
    new = jnp.stack(cols, axis=2)
    return old, jnp.pad(new, ((0, 0), (0, 0), (0, LANES - t)))


def dil_attn_sample(buf_t, qkv, g, bias, r, nk):
    nb, kvw, w = buf_t.shape
    half = kvw // 2
    t = qkv.shape[1]
    n_groups = qkv.shape[2] // (3 * half)
    nh = bias.shape[0]
    assert w == r * nk, "the buffer holds exactly one window"
    assert t <= SUBLANES and w % LANES == 0
    bias_old, bias_new = _sample_bias_tables(bias, r, nk, t)
    bias_old = bias_old.reshape(nh * SUBLANES, w)
    bias_new = bias_new.reshape(nh * SUBLANES, LANES)
    hd = half // nh
    hmask = (jnp.arange(nh * SUBLANES)[:, None] // SUBLANES == jnp.arange(half)[None, :] // hd).astype(F32)
    eb = max(1, min(nb, DIL_SAMPLE_BLOCK_BYTES // (kvw * w * 4)))
    while nb % eb:
        eb -= 1
    col = lambda c: pl.BlockSpec((eb, t, half), lambda bi: (bi, 0, c))
    whole = pl.BlockSpec((eb, kvw, w), lambda bi: (bi, 0, 0))
    return pl.pallas_call(
        functools.partial(_dil_sample_kernel, eb=eb, t=t, nh=nh, hd=hd),
        grid=(nb // eb,),
        in_specs=[whole, col(g), col(n_groups + g), col(2 * n_groups + g),
                  _resident(bias_old.shape), _resident(bias_new.shape), _resident(hmask.shape)],
        out_specs=[whole, pl.BlockSpec((eb, t, half), lambda bi: (bi, 0, 0)),
                   pl.BlockSpec((eb, t, half), lambda bi: (bi, 0, 0))],
        out_shape=[jax.ShapeDtypeStruct((nb, kvw, w), F32), jax.ShapeDtypeStruct((nb, t, half), F32),
                   jax.ShapeDtypeStruct((nb, t, half), F32)],
        scratch_shapes=[pltpu.VMEM((eb, LANES, kvw), F32)],
        compiler_params=_params("parallel"),
        name="dil_attn_sample",
    )(buf_t, qkv, qkv, qkv, bias_old, bias_new, hmask)


def _band_bias(bias, nk):
    nh = bias.shape[0]
    period = 3 * nk
    v = jnp.concatenate([bias[:, ::-1], jnp.broadcast_to(bias[:, :1], (nh, nk - 1)),
                         jnp.broadcast_to(bias[:, nk:], (nh, nk))], axis=1)
    assert v.shape[1] == period
    skew = jnp.broadcast_to(v[:, None, :], (nh, nk, period)).reshape(nh, nk * period)
    return skew[:, :nk * (period - 1)].reshape(nh, nk, period - 1)[:, :, :2 * nk]


def _dil_combine_kernel(*refs, ng):
    o_refs, l_refs = refs[:ng], refs[ng:2 * ng]
    wo_ref, h_ref, y_ref = refs[2 * ng:]
    lses = [l_ref[...] for l_ref in l_refs]
    m = functools.reduce(jnp.maximum, lses)
    es = [jnp.exp(l - m) for l in lses]
    tot = functools.reduce(lambda a, b: a + b, es)
    o = functools.reduce(lambda a, b: a + b, [(e / tot) * o_ref[...] for e, o_ref in zip(es, o_refs)])
    y_ref[...] = h_ref[...] + _dot(o.astype(BF16), wo_ref[...])


def dil_combine_out(outs, lses, w_o, h, tm=512):
    m, w = outs[0].shape
    d = h.shape[1]
    tm = _row_tile(m, tm)
    ng = len(outs)
    rows = lambda width: pl.BlockSpec((tm, width), lambda i: (i, 0))
    return pl.pallas_call(
        functools.partial(_dil_combine_kernel, ng=ng),
        grid=(m // tm,),
        in_specs=[rows(w)] * (2 * ng) + [_resident(w_o.shape), rows(d)],
        out_specs=rows(d),
        out_shape=jax.ShapeDtypeStruct((m, d), F32),
        compiler_params=_params("parallel"),
        name="dil_combine_out",
    )(*outs, *lses, w_o, h)


def _conv_layer(hp, hs, g_mix, state, w_in, w_dw, b_dw, ln_g, ln_b, w_out, bp, s):
    d = hp.shape[1]
    w_in, w_out = w_in.astype(BF16), w_out.astype(BF16)
    hist = w_dw.shape[0] - 1
    glu_p = conv_in(hp, g_mix, w_in)
    glu_s = conv_in(hs, g_mix, w_in)
    glu_p3 = glu_p.reshape(bp, s, d)
    new_hp = conv_core_prompt(glu_p3, hp.reshape(bp, s, d), w_dw, b_dw, ln_g, ln_b, w_out).reshape(bp * s, d)
    new_hs, st_s = conv_core_sample(glu_s, state, hs, w_dw, b_dw, ln_g, ln_b, w_out)
    st_p = jnp.concatenate([jnp.zeros((bp, hist, d), F32), glu_p3], axis=1)[:, s:]
    return new_hp, new_hs, st_p, st_s


def _rope_tables(pos):
    inv = ROPE_THETA ** (-jnp.arange(0, MLA_ROPE, 2, dtype=F32) / MLA_ROPE)
    ang = pos.astype(F32)[:, None] * inv[None, :]
    cos, sin = jnp.cos(ang), jnp.sin(ang)
    reps = LANES // MLA_ROPE
    return jnp.tile(jnp.concatenate([cos, cos], axis=1), (1, reps)), jnp.tile(jnp.concatenate([-sin, sin], axis=1), (1, reps))


def _swap_halves(w):
    k = w.shape[0]
    w4 = w.reshape(k, -1, 2, MLA_ROPE // 2)
    return w4[:, :, ::-1, :].reshape(k, -1)


def _mla_layer(hp, hs, g_mix, ckv_pool, kpe_pool, page_table, w_dq, g_q, w_uq, w_dkv, g_kv, w_uk, w_uv, w_o, bp, s):
    nb = page_table.shape[0]
    t = hs.shape[0] // nb
    past = page_table.shape[1] * ckv_pool.shape[1]
    qr = w_dq.shape[1]
    uq = w_uq.reshape(qr, MLA_HEADS, MLA_NOPE + MLA_ROPE)
    uq_nope = uq[:, :, :MLA_NOPE].reshape(qr, -1)
    uq_rope = uq[:, :, MLA_NOPE:].reshape(qr, -1)
    dkv_rope = w_dkv[:, MLA_KV_RANK:]
    w = {"dq": w_dq.astype(BF16), "gq": g_q,
         "uq": jnp.concatenate([uq_nope, uq_rope, _swap_halves(uq_rope)], axis=1).astype(BF16),
         "dkv": jnp.concatenate([w_dkv[:, :MLA_KV_RANK], dkv_rope, _swap_halves(dkv_rope)], axis=1).astype(BF16),
         "gkv": g_kv, "uk": w_uk.astype(BF16)}
    w_uv, w_o = w_uv.astype(BF16), w_o.astype(BF16)

    tm_p = _row_tile(s, 256)
    cos_p, sin_p = _rope_tables(jnp.arange(s))
    q_p, kcat_p, ckv_p, kpe_p = mla_in(hp, g_mix, w, cos_p, sin_p, lambda i: i % (s // tm_p), tm=tm_p)
    o_p = mla_attn_prompt(q_p, kcat_p, bp, s)
    new_hp = mla_out(o_p, w_uv, w_o, hp)

    ms = hs.shape[0]
    tm_s = _row_tile(ms, 256)
    assert tm_s % t == 0
    cos_s, sin_s = _rope_tables(past + jnp.arange(t))
    cos_s, sin_s = jnp.tile(cos_s, (tm_s // t, 1)), jnp.tile(sin_s, (tm_s // t, 1))
    q_s, kcat_s, ckv_s, kpe_s = mla_in(hs, g_mix, w, cos_s, sin_s, lambda i: 0, tm=tm_s)
    q_rows = q_s.reshape(MLA_HEADS, nb, t, MLA_QK).transpose(1, 0, 2, 3).reshape(nb, MLA_HEADS * t, MLA_QK)
    o_s = mla_attn_sample(q_rows, kcat_s.reshape(nb, t, MLA_QK), ckv_pool, jnp.swapaxes(kpe_pool, 1, 2), page_table)
    o_s = o_s.reshape(nb, MLA_HEADS, t, MLA_KV_RANK).transpose(1, 0, 2, 3).reshape(MLA_HEADS, ms, MLA_KV_RANK)
    new_hs = mla_out(o_s, w_uv, w_o, hs)
    return (new_hp, new_hs, ckv_p.reshape(bp, s, -1), kpe_p.reshape(bp, s, -1),
            ckv_s.reshape(nb, t, -1), kpe_s.reshape(nb, t, -1))


def _ssd_layer(hp, hs, g_mix, conv_state, ssm_state, w_in, w_conv, b_conv, dt_bias, a_log, d_skip, g_norm, w_out,
               bp, s):
    nb = conv_state.shape[0]
    t = hs.shape[0] // nb
    inner = g_norm.shape[0]
    conv_dim = w_conv.shape[1]
    nh = dt_bias.shape[0]
    hist = SSD_CONV - 1
    ws = [w_in[:, :inner].astype(BF16), w_in[:, inner:inner + conv_dim].astype(BF16),
          jnp.pad(w_in[:, inner + conv_dim:], ((0, 0), (0, LANES - nh))).astype(BF16)]
    w_out = w_out.astype(BF16)
    prm = (w_conv, b_conv, dt_bias, a_log, d_skip, g_norm)

    def run(h, b, l, h0, hist_rows):
        z, xbc, dt = norm_matmul(h, g_mix, ws)
        xbc3 = xbc.reshape(b, l, conv_dim)
        hist8 = jnp.pad(hist_rows, ((0, 0), (SSD_TAIL - hist, 0), (0, 0)))
        y, h_fin = ssd_core(xbc3, z.reshape(b, l, inner), dt.reshape(b, l, LANES), h0, hist8, *prm)
        new_h = matmul_residual(y.reshape(b * l, inner), w_out, h)
        new_hist = jnp.concatenate([hist_rows, xbc3], axis=1)[:, l:]
        return new_h, new_hist, h_fin.reshape(b, nh, SSD_HEAD_DIM, SSD_STATE)

    zero_state = jnp.zeros((bp, nh * SSD_HEAD_DIM, SSD_STATE), F32)
    new_hp, hc_p, hh_p = run(hp, bp, s, zero_state, jnp.zeros((bp, hist, conv_dim), F32))
    new_hs, hc_s, hh_s = run(hs, nb, t, ssm_state.reshape(nb, nh * SSD_HEAD_DIM, SSD_STATE), conv_state)
    return new_hp, new_hs, hc_p, hh_p, hc_s, hh_s


def _dil_layer(hp, hs, g_mix, bufs_in, w_qkv, w_o, rel_bias, bp, s):
    nb = bufs_in[0].shape[0]
    t = hs.shape[0] // nb
    ng = len(DIL_PATTERNS)
    half = DIL_HEADS_PER_GROUP * DIL_HEAD_DIM
    w_qkv, w_o = w_qkv.astype(BF16), w_o.astype(BF16)
    (qkv_p,) = norm_matmul(hp, g_mix, [w_qkv])
    (qkv_s,) = norm_matmul(hs, g_mix, [w_qkv])
    qkv_p3 = qkv_p.reshape(bp, s, 3 * ng * half)
    qkv_s3 = qkv_s.reshape(nb, t, 3 * ng * half)
    nh, hd = DIL_HEADS_PER_GROUP, DIL_HEAD_DIM

    outs_p, lses_p, bufs_p, outs_s, lses_s, bufs_s = [], [], [], [], [], []
    for g, (win, r) in enumerate(DIL_PATTERNS):
        nk = win // r
        bias = _group_bias(rel_bias, g, r, nk)
        o, lse = dil_attn_prompt(qkv_p3, g, ng, r, _band_bias(bias, nk), nk, half)
        outs_p.append(o)
        lses_p.append(lse)
        keep = min(win, s)
        st = dil_state_prompt(qkv_p3, g, ng, keep, half)
        bufs_p.append(st.reshape(bp, 2, nh, hd, keep).transpose(0, 4, 1, 2, 3))
        buf = bufs_in[g]
        wb = buf.shape[1]
        buf_t = buf.transpose(0, 2, 3, 4, 1).reshape(nb, 2 * half, wb)
        nbuf_t, o, lse = dil_attn_sample(buf_t, qkv_s3, g, bias, r, nk)
        outs_s.append(o.reshape(nb * t, half))
        lses_s.append(lse.reshape(nb * t, half))
        bufs_s.append(nbuf_t.reshape(nb, 2, nh, hd, wb).transpose(0, 4, 1, 2, 3))
    new_hp = dil_combine_out(outs_p, lses_p, w_o, hp)
    new_hs = dil_combine_out(outs_s, lses_s, w_o, hs)
    return new_hp, new_hs, bufs_p, bufs_s


def kernel(x_prompt, x_sample, state_conv, cache_mla_ckv, cache_mla_kpe, state_ssd_conv, state_ssd, state_dil0_kv, state_dil1_kv, state_dil2_kv, page_table, p_prompt, p_sample, norm_mix, norm_ffn, norm_ple, norm_final, conv_w_in, conv_w_dw, conv_b_dw, conv_ln_g, conv_ln_b, conv_w_out, mla_w_dq, mla_g_q, mla_w_uq, mla_w_dkv, mla_g_kv, mla_w_uk, mla_w_uv, mla_w_o, ssd_w_in, ssd_w_conv, ssd_b_conv, ssd_dt_bias, ssd_a_log, ssd_d, ssd_g_norm, ssd_w_out, dil_w_qkv, dil_w_o, rel_bias, ffn_w1, ffn_w2, ple_w_gate, ple_w_proj):
    bp, s, d = x_prompt.shape
    nb, t, _ = x_sample.shape
    depth = norm_mix.shape[0]
    hp = x_prompt.reshape(bp * s, d)
    hs = x_sample.reshape(nb * t, d)
    conv_p, conv_s = [], []
    ckv_p, kpe_p, ckv_s, kpe_s = [], [], [], []
    ssdc_p, ssdh_p, ssdc_s, ssdh_s = [], [], [], []
    dil_p, dil_s = [[], [], []], [[], [], []]
    dil_in = (state_dil0_kv, state_dil1_kv, state_dil2_kv)
    for i in range(depth):
        kind, j = i % 4, i // 4
        if kind == 0:
            hp, hs, st_p, st_s = _conv_layer(hp, hs, norm_mix[i], state_conv[j], conv_w_in[j], conv_w_dw[j],
                                             conv_b_dw[j], conv_ln_g[j], conv_ln_b[j], conv_w_out[j], bp, s)
            conv_p.append(st_p)
            conv_s.append(st_s)
        elif kind == 1:
            hp, hs, c_p, r_p, c_s, r_s = _mla_layer(hp, hs, norm_mix[i], cache_mla_ckv[j], cache_mla_kpe[j], page_table,
                                                    mla_w_dq[j], mla_g_q[j], mla_w_uq[j], mla_w_dkv[j], mla_g_kv[j],
                                                    mla_w_uk[j], mla_w_uv[j], mla_w_o[j], bp, s)
            ckv_p.append(c_p)
            kpe_p.append(r_p)
            ckv_s.append(c_s)
            kpe_s.append(r_s)
        elif kind == 2:
            hp, hs, hc_p, hh_p, hc_s, hh_s = _ssd_layer(hp, hs, norm_mix[i], state_ssd_conv[j], state_ssd[j], ssd_w_in[j],
                                                        ssd_w_conv[j], ssd_b_conv[j], ssd_dt_bias[j], ssd_a_log[j],
                                                        ssd_d[j], ssd_g_norm[j], ssd_w_out[j], bp, s)
            ssdc_p.append(hc_p)
            ssdh_p.append(hh_p)
            ssdc_s.append(hc_s)
            ssdh_s.append(hh_s)
        else:
            hp, hs, bufs_p, bufs_s = _dil_layer(hp, hs, norm_mix[i], [b[j] for b in dil_in], dil_w_qkv[j], dil_w_o[j],
                                                rel_bias, bp, s)
            for g in range(len(DIL_PATTERNS)):
                dil_p[g].append(bufs_p[g])
                dil_s[g].append(bufs_s[g])
        final = i == depth - 1
        w1, w2 = ffn_w1[i].astype(BF16), ffn_w2[i].astype(BF16)
        wg, wp = ple_w_gate[i].astype(BF16), ple_w_proj[i].astype(BF16)
        hp = ffn(hp, norm_ffn[i], w1, w2)
        hs = ffn(hs, norm_ffn[i], w1, w2)
        hp = ple(hp, p_prompt[i].reshape(bp * s, -1), norm_ple[i], wg, wp, norm_final, final)
        hs = ple(hs, p_sample[i].reshape(nb * t, -1), norm_ple[i], wg, wp, norm_final, final)
    return (hp.reshape(bp, s, d), hs.reshape(nb, t, d),
            jnp.stack(conv_p), jnp.stack(conv_s),
            jnp.stack(ckv_p), jnp.stack(kpe_p), jnp.stack(ckv_s), jnp.stack(kpe_s),
            jnp.stack(ssdc_p), jnp.stack(ssdh_p), jnp.stack(ssdc_s), jnp.stack(ssdh_s),
            jnp.stack(dil_p[0]), jnp.stack(dil_p[1]), jnp.stack(dil_p[2]),
            jnp.stack(dil_s[0]), jnp.stack(dil_s[1]), jnp.stack(dil_s[2]))
```

```python
import functools
import math

import jax
import jax.numpy as jnp
from jax import lax
from jax.experimental import pallas as pl
from jax.experimental.pallas import tpu as pltpu

F32 = jnp.float32
BF16 = jnp.bfloat16
EPS = 1e-6

LANES = 128
SUBLANES = 8
VMEM_LIMIT_BYTES = 56 * 1024 * 1024

PAGE_SIZE = 128
CONV_WIDTH = 31
MLA_HEADS = 8
MLA_NOPE = 128
MLA_ROPE = 64
MLA_KV_RANK = 256
MLA_SCALE = (MLA_NOPE + MLA_ROPE) ** -0.5
ROPE_THETA = 10000.0
SSD_HEAD_DIM = 64
SSD_GROUPS = 4
SSD_STATE = 128
SSD_CONV = 4
SSD_CHUNK = 128
DIL_PATTERNS = ((128, 1), (512, 4), (2048, 16))
DIL_HEADS_PER_GROUP = 8
DIL_HEAD_DIM = 64
DIL_SCALE = DIL_HEAD_DIM ** -0.5
REL_BUCKETS = 32
REL_MAX_DIST = 2048


def _params(*sem):
    return pltpu.CompilerParams(dimension_semantics=sem, vmem_limit_bytes=VMEM_LIMIT_BYTES)


def _resident(shape):
    zeros = (0,) * len(shape)
    return pl.BlockSpec(shape, lambda *_: zeros)


def _rms(x, g):
    return x * lax.rsqrt(jnp.mean(x * x, axis=-1, keepdims=True) + EPS) * g


def _silu(x):
    return x * jax.nn.sigmoid(x)


def _dot(a, b):
    return jnp.dot(a, b, preferred_element_type=F32)


def _dot_nt(a, b):
    return lax.dot_general(a, b, (((1,), (1,)), ((), ())), preferred_element_type=F32)


def _split3(x):
    p1 = x.astype(BF16)
    r1 = x - p1.astype(F32)
    p2 = r1.astype(BF16)
    p3 = (r1 - p2.astype(F32)).astype(BF16)
    return p1, p2, p3


def _dot_sel_lhs(sel, x):
    p1, p2, p3 = _split3(x)
    return _dot(sel, p1) + _dot(sel, p2) + _dot(sel, p3)


def _dot_sel_rhs(x, sel):
    p1, p2, p3 = _split3(x)
    return _dot(p1, sel) + _dot(p2, sel) + _dot(p3, sel)


def _row_tile(m, want):
    t = min(m, want)
    assert m % t == 0, (m, t)
    return t


def _norm_matmul_kernel(x_ref, g_ref, *refs, n_w, chunk):
    w_refs, o_refs = refs[:n_w], refs[n_w:]
    xn = _rms(x_ref[...], g_ref[...]).astype(BF16)
    for w_ref, o_ref in zip(w_refs, o_refs):
        n = w_ref.shape[1]
        for c0 in range(0, n, chunk):
            c1 = min(c0 + chunk, n)
            o_ref[:, c0:c1] = _dot(xn, w_ref[:, c0:c1]).astype(o_ref.dtype)


def norm_matmul(x, g, ws, tm=256, chunk=512):
    m, k = x.shape
    tm = _row_tile(m, tm)
    return pl.pallas_call(
        functools.partial(_norm_matmul_kernel, n_w=len(ws), chunk=chunk),
        grid=(m // tm,),
        in_specs=[pl.BlockSpec((tm, k), lambda i: (i, 0)), _resident((1, k))]
        + [_resident(w.shape) for w in ws],
        out_specs=[pl.BlockSpec((tm, w.shape[1]), lambda i: (i, 0)) for w in ws],
        out_shape=[jax.ShapeDtypeStruct((m, w.shape[1]), F32) for w in ws],
        compiler_params=_params("parallel"),
        name="norm_matmul",
    )(x, g.reshape(1, k), *ws)


def _matmul_residual_kernel(a_ref, w_ref, h_ref, o_ref):
    o_ref[...] = h_ref[...] + _dot(a_ref[...].astype(BF16), w_ref[...])


def matmul_residual(a, w, h, tm=512):
    m, k = a.shape
    n = w.shape[1]
    tm = _row_tile(m, tm)
    return pl.pallas_call(
        _matmul_residual_kernel,
        grid=(m // tm,),
        in_specs=[pl.BlockSpec((tm, k), lambda i: (i, 0)), _resident(w.shape),
                  pl.BlockSpec((tm, n), lambda i: (i, 0))],
        out_specs=pl.BlockSpec((tm, n), lambda i: (i, 0)),
        out_shape=jax.ShapeDtypeStruct((m, n), F32),
        compiler_params=_params("parallel"),
        name="matmul_residual",
    )(a, w, h)


def _ffn_kernel(x_ref, g_ref, w1_ref, w2_ref, o_ref, xn_scr, acc_scr):
    j = pl.program_id(1)

    @pl.when(j == 0)
    def _():
        xn_scr[...] = _rms(x_ref[...], g_ref[...]).astype(BF16)
        acc_scr[...] = jnp.zeros_like(acc_scr)

    a = _dot(xn_scr[...], w1_ref[...])
    a = jnp.square(jnp.maximum(a, 0.0)).astype(BF16)
    acc_scr[...] += _dot(a, w2_ref[...])

    @pl.when(j == pl.num_programs(1) - 1)
    def _():
        o_ref[...] = x_ref[...] + acc_scr[...]


def ffn(x, g, w1, w2, tm=1024, tf=1024):
    m, d = x.shape
    f = w1.shape[1]
    tm = _row_tile(m, tm)
    return pl.pallas_call(
        _ffn_kernel,
        grid=(m // tm, f // tf),
        in_specs=[pl.BlockSpec((tm, d), lambda i, j: (i, 0)), _resident((1, d)),
                  pl.BlockSpec((d, tf), lambda i, j: (0, j)),
                  pl.BlockSpec((tf, d), lambda i, j: (j, 0))],
        out_specs=pl.BlockSpec((tm, d), lambda i, j: (i, 0)),
        out_shape=jax.ShapeDtypeStruct((m, d), F32),
        scratch_shapes=[pltpu.VMEM((tm, d), BF16), pltpu.VMEM((tm, d), F32)],
        compiler_params=_params("parallel", "arbitrary"),
        name="ffn",
    )(x, g.reshape(1, d), w1, w2)


def _ple_kernel(x_ref, p_ref, g_ref, wg_ref, wp_ref, gf_ref, o_ref, *, final):
    x = x_ref[...]
    xn = _rms(x, g_ref[...]).astype(BF16)
    gate = jax.nn.sigmoid(_dot(xn, wg_ref[...]))
    y = x + gate * _dot(p_ref[...].astype(BF16), wp_ref[...])
    if final:
        y = _rms(y, gf_ref[...])
    o_ref[...] = y


def ple(x, p, g, wg, wp, g_final, final, tm=512):
    m, d = x.shape
    pd = p.shape[1]
    tm = _row_tile(m, tm)
    return pl.pallas_call(
        functools.partial(_ple_kernel, final=final),
        grid=(m // tm,),
        in_specs=[pl.BlockSpec((tm, d), lambda i: (i, 0)), pl.BlockSpec((tm, pd), lambda i: (i, 0)),
                  _resident((1, d)), _resident(wg.shape), _resident(wp.shape), _resident((1, d))],
        out_specs=pl.BlockSpec((tm, d), lambda i: (i, 0)),
        out_shape=jax.ShapeDtypeStruct((m, d), F32),
        compiler_params=_params("parallel"),
        name="ple",
    )(x, p, g.reshape(1, d), wg, wp, g_final.reshape(1, d))


def _conv_in_kernel(x_ref, g_ref, w_ref, o_ref):
    xn = _rms(x_ref[...], g_ref[...]).astype(BF16)
    d = o_ref.shape[1]
    o_ref[...] = _dot(xn, w_ref[:, :d]) * jax.nn.sigmoid(_dot(xn, w_ref[:, d:]))


def conv_in(x, g, w_in, tm=512):
    m, d = x.shape
    tm = _row_tile(m, tm)
    return pl.pallas_call(
        _conv_in_kernel,
        grid=(m // tm,),
        in_specs=[pl.BlockSpec((tm, d), lambda i: (i, 0)), _resident((1, d)), _resident(w_in.shape)],
        out_specs=pl.BlockSpec((tm, d), lambda i: (i, 0)),
        out_shape=jax.ShapeDtypeStruct((m, d), F32),
        compiler_params=_params("parallel"),
        name="conv_in",
    )(x, g.reshape(1, d), w_in)


def _ln_silu_out(c, h, lng_ref, lnb_ref, wo_ref):
    xc = c - jnp.mean(c, axis=-1, keepdims=True)
    y = xc * lax.rsqrt(jnp.mean(xc * xc, axis=-1, keepdims=True) + EPS) * lng_ref[...] + lnb_ref[...]
    return h + _dot(_silu(y).astype(BF16), wo_ref[...])


CONV_HALO = 32


def _conv_prompt_kernel(cur_ref, halo_ref, h_ref, wdw_ref, bdw_ref, lng_ref, lnb_ref, wo_ref, o_ref,
                        full_scr, *, ts, width):
    i = pl.program_id(1)
    d = cur_ref.shape[2]
    full_scr[0:CONV_HALO] = jnp.where(i > 0, halo_ref[0], 0.0)
    full_scr[CONV_HALO:CONV_HALO + ts] = cur_ref[0]
    off = CONV_HALO - (width - 1)
    cols = []
    for c in range(d // LANES):
        cs = slice(c * LANES, (c + 1) * LANES)
        fc = full_scr[:, cs]
        acc = jnp.broadcast_to(bdw_ref[:, cs], (ts, LANES))
        for phase in range(SUBLANES):
            taps = [k for k in range(width) if (off + k) % SUBLANES == phase]
            if not taps:
                continue
            shifted = pltpu.roll(fc, fc.shape[0] - phase, axis=0) if phase else fc
            for k in taps:
                base = off + k - phase
                acc = acc + shifted[base:base + ts] * wdw_ref[k:k + 1, cs]
        cols.append(acc)
    o_ref[0] = _ln_silu_out(jnp.concatenate(cols, axis=-1), h_ref[0], lng_ref, lnb_ref, wo_ref)


def conv_core_prompt(glu, h, w_dw, b_dw, ln_g, ln_b, w_out, ts=256):
    b, s, d = glu.shape
    ts = _row_tile(s, ts)
    width = w_dw.shape[0]
    assert width - 1 <= CONV_HALO and ts % CONV_HALO == 0
    per = ts // CONV_HALO
    vec = lambda v: v.reshape(1, d)
    return pl.pallas_call(
        functools.partial(_conv_prompt_kernel, ts=ts, width=width),
        grid=(b, s // ts),
        in_specs=[pl.BlockSpec((1, ts, d), lambda bi, i: (bi, i, 0)),
                  pl.BlockSpec((1, CONV_HALO, d), lambda bi, i: (bi, jnp.maximum(i * per - 1, 0), 0)),
                  pl.BlockSpec((1, ts, d), lambda bi, i: (bi, i, 0)),
                  _resident(w_dw.shape), _resident((1, d)), _resident((1, d)), _resident((1, d)),
                  _resident(w_out.shape)],
        out_specs=pl.BlockSpec((1, ts, d), lambda bi, i: (bi, i, 0)),
        out_shape=jax.ShapeDtypeStruct((b, s, d), F32),
        scratch_shapes=[pltpu.VMEM((CONV_HALO + ts, d), F32)],
        compiler_params=_params("parallel", "arbitrary"),
        name="conv_core_prompt",
    )(glu, glu, h, w_dw, vec(b_dw), vec(ln_g), vec(ln_b), w_out)


def _conv_sample_kernel(glu_ref, st_ref, h_ref, wdw_ref, bdw_ref, lng_ref, lnb_ref, wo_ref, o_ref, nst_ref,
                        full_scr, c_scr, *, bb, t, width):
    hist = width - 1
    d = glu_ref.shape[1]
    for bi in range(bb):
        full_scr[0:hist] = st_ref[bi]
        full_scr[hist:hist + t] = glu_ref[bi * t:(bi + 1) * t, :]
        acc = jnp.broadcast_to(bdw_ref[...], (t, d))
        for k in range(width):
            acc = acc + full_scr[k:k + t, :] * wdw_ref[k:k + 1, :]
        c_scr[bi * t:(bi + 1) * t, :] = acc
        nst_ref[bi] = full_scr[t:t + hist]
    o_ref[...] = _ln_silu_out(c_scr[...], h_ref[...], lng_ref, lnb_ref, wo_ref)


def conv_core_sample(glu, state, h, w_dw, b_dw, ln_g, ln_b, w_out, bb=8):
    nb, hist, d = state.shape
    t = glu.shape[0] // nb
    width = w_dw.shape[0]
    assert hist == width - 1 and nb % bb == 0
    vec = lambda v: v.reshape(1, d)
    return pl.pallas_call(
        functools.partial(_conv_sample_kernel, bb=bb, t=t, width=width),
        grid=(nb // bb,),
        in_specs=[pl.BlockSpec((bb * t, d), lambda i: (i, 0)),
                  pl.BlockSpec((bb, hist, d), lambda i: (i, 0, 0)),
                  pl.BlockSpec((bb * t, d), lambda i: (i, 0)),
                  _resident(w_dw.shape), _resident((1, d)), _resident((1, d)), _resident((1, d)),
                  _resident(w_out.shape)],
        out_specs=[pl.BlockSpec((bb * t, d), lambda i: (i, 0)),
                   pl.BlockSpec((bb, hist, d), lambda i: (i, 0, 0))],
        out_shape=[jax.ShapeDtypeStruct((nb * t, d), F32), jax.ShapeDtypeStruct((nb, hist, d), F32)],
        scratch_shapes=[pltpu.VMEM((hist + t + SUBLANES, d), F32), pltpu.VMEM((bb * t, d), F32)],
        compiler_params=_params("parallel"),
        name="conv_core_sample",
    )(glu, state, h, w_dw, vec(b_dw), vec(ln_g), vec(ln_b), w_out)


MLA_QK = MLA_KV_RANK + MLA_ROPE


def _mla_in_kernel(x_ref, g_ref, wdq_ref, gq_ref, wuq_ref, wdkv_ref, gkv_ref, wuk_ref, cos_ref, sin_ref,
                   q_ref, kcat_ref, ckv_ref, kpe_ref):
    nope_w = MLA_HEADS * MLA_NOPE
    rope_w = MLA_HEADS * MLA_ROPE
    xn = _rms(x_ref[...], g_ref[...]).astype(BF16)
    cq = _rms(_dot(xn, wdq_ref[...]), gq_ref[...]).astype(BF16)
    kv = _dot(xn, wdkv_ref[...])
    ckv = _rms(kv[:, :MLA_KV_RANK], gkv_ref[...])
    cos, sin = cos_ref[...], sin_ref[...]
    kpe = (kv[:, MLA_KV_RANK:MLA_QK] * cos[:, :MLA_ROPE]
           + kv[:, MLA_QK:MLA_QK + MLA_ROPE] * sin[:, :MLA_ROPE])
    qp = _dot(cq, wuq_ref[:, nope_w:nope_w + rope_w])
    qps = _dot(cq, wuq_ref[:, nope_w + rope_w:])
    qn = _dot(cq, wuq_ref[:, :nope_w]).astype(BF16)
    per = LANES // MLA_ROPE
    q_rope, q_lat = [], []
    for c in range(rope_w // LANES):
        cs = slice(c * LANES, (c + 1) * LANES)
        roped = ((qp[:, cs] * cos + qps[:, cs] * sin) * MLA_SCALE).astype(BF16)
        q_rope += [roped[:, hh * MLA_ROPE:(hh + 1) * MLA_ROPE] for hh in range(per)]
    for h in range(MLA_HEADS):
        q_lat.append((_dot(qn[:, h * MLA_NOPE:(h + 1) * MLA_NOPE], wuk_ref[h]) * MLA_SCALE).astype(BF16))
    ckv_ref[...] = ckv
    kpe_ref[...] = kpe
    kcat_ref[:, :MLA_KV_RANK] = ckv.astype(BF16)
    kcat_ref[:, MLA_KV_RANK:] = kpe.astype(BF16)
    for h in range(MLA_HEADS):
        q_ref[h, :, :MLA_KV_RANK] = q_lat[h]
        q_ref[h, :, MLA_KV_RANK:] = q_rope[h]


def mla_in(x, g, w, cos_tab, sin_tab, tab_index, tm=256):
    m, d = x.shape
    tm = _row_tile(m, tm)
    assert cos_tab.shape[0] % tm == 0 or cos_tab.shape[0] == tm
    return pl.pallas_call(
        _mla_in_kernel,
        grid=(m // tm,),
        in_specs=[pl.BlockSpec((tm, d), lambda i: (i, 0)), _resident((1, d)),
                  _resident(w["dq"].shape), _resident((1, w["dq"].shape[1])), _resident(w["uq"].shape),
                  _resident(w["dkv"].shape), _resident((1, MLA_KV_RANK)), _resident(w["uk"].shape),
                  pl.BlockSpec((tm, LANES), lambda i: (tab_index(i), 0)),
                  pl.BlockSpec((tm, LANES), lambda i: (tab_index(i), 0))],
        out_specs=[pl.BlockSpec((MLA_HEADS, tm, MLA_QK), lambda i: (0, i, 0)),
                   pl.BlockSpec((tm, MLA_QK), lambda i: (i, 0)),
                   pl.BlockSpec((tm, MLA_KV_RANK), lambda i: (i, 0)),
                   pl.BlockSpec((tm, MLA_ROPE), lambda i: (i, 0))],
        out_shape=[jax.ShapeDtypeStruct((MLA_HEADS, m, MLA_QK), BF16),
                   jax.ShapeDtypeStruct((m, MLA_QK), BF16),
                   jax.ShapeDtypeStruct((m, MLA_KV_RANK), F32),
                   jax.ShapeDtypeStruct((m, MLA_ROPE), F32)],
        compiler_params=_params("parallel"),
        name="mla_in",
    )(x, g.reshape(1, d), w["dq"], w["gq"].reshape(1, -1), w["uq"], w["dkv"], w["gkv"].reshape(1, -1),
      w["uk"], cos_tab, sin_tab)


def _mla_attn_kernel(q_ref, k_ref, o_ref, m_scr, l_scr, acc_scr, *, tq, tk, hg):
    i, j = pl.program_id(1), pl.program_id(2)
    nh = q_ref.shape[0]
    last_j = ((i + 1) * tq - 1) // tk

    @pl.when(j == 0)
    def _():
        m_scr[...] = jnp.full_like(m_scr, -jnp.inf)
        l_scr[...] = jnp.zeros_like(l_scr)
        acc_scr[...] = jnp.zeros_like(acc_scr)

    def step(masked):
        k = k_ref[...]
        kv = k[:, :MLA_KV_RANK]
        state = [(m_scr[h], l_scr[h], acc_scr[h]) for h in range(nh)]
        groups = [range(g0, g0 + hg) for g0 in range(0, nh, hg)]
        scores = [_dot_nt(q_ref[grp[0]:grp[0] + hg].reshape(hg * tq, MLA_QK), k) for grp in groups]
        if masked:
            row = lax.broadcasted_iota(jnp.int32, (tq, tk), 0)
            col = lax.broadcasted_iota(jnp.int32, (tq, tk), 1)
            keep = col + j * tk <= row + i * tq
        new_state = [None] * nh
        for grp, s_grp in zip(groups, scores):
            probs, partial = [], []
            for n, h in enumerate(grp):
                s = s_grp[n * tq:(n + 1) * tq]
                if masked:
                    s = jnp.where(keep, s, -jnp.inf)
                m_prev, l_prev, acc_prev = state[h]
                m_new = jnp.maximum(m_prev, jnp.max(s, axis=-1, keepdims=True))
                alpha = jnp.exp(m_prev - m_new)
                p = jnp.exp(s - m_new)
                probs.append(p.astype(BF16))
                partial.append((m_new, alpha * l_prev + jnp.sum(p, axis=-1, keepdims=True), alpha * acc_prev))
            pv = _dot(jnp.concatenate(probs, axis=0), kv)
            for n, h in enumerate(grp):
                m_new, l_new, acc_scaled = partial[n]
                new_state[h] = (m_new, l_new, acc_scaled + pv[n * tq:(n + 1) * tq])
        for h in range(nh):
            m_scr[h], l_scr[h], acc_scr[h] = new_state[h]

    crosses_diagonal = j * tk + tk - 1 > i * tq

    @pl.when((j <= last_j) & jnp.logical_not(crosses_diagonal))
    def _():
        step(False)

    @pl.when((j <= last_j) & crosses_diagonal)
    def _():
        step(True)

    @pl.when(j == last_j)
    def _():
        o_ref[...] = (acc_scr[...] / l_scr[...]).astype(o_ref.dtype)


def mla_attn_prompt(q, kcat, b, s, tq=512, tk=1024, hg=1):
    nh = q.shape[0]
    tq, tk = _row_tile(s, tq), _row_tile(s, tk)
    assert tq & (tq - 1) == 0
    nq, nk = s // tq, s // tk

    def k_index(bi, i, j):
        return (bi * nk + jnp.minimum(j, ((i + 1) * tq - 1) // tk), 0)

    return pl.pallas_call(
        functools.partial(_mla_attn_kernel, tq=tq, tk=tk, hg=hg),
        grid=(b, nq, nk),
        in_specs=[pl.BlockSpec((nh, tq, MLA_QK), lambda bi, i, j: (0, bi * nq + i, 0)),
                  pl.BlockSpec((tk, MLA_QK), k_index)],
        out_specs=pl.BlockSpec((nh, tq, MLA_KV_RANK), lambda bi, i, j: (0, bi * nq + i, 0)),
        out_shape=jax.ShapeDtypeStruct((nh, b * s, MLA_KV_RANK), BF16),
        scratch_shapes=[pltpu.VMEM((nh, tq, 1), F32), pltpu.VMEM((nh, tq, 1), F32),
                        pltpu.VMEM((nh, tq, MLA_KV_RANK), F32)],
        compiler_params=_params("parallel", "parallel", "arbitrary"),
        name="mla_attn_prompt",
    )(q, kcat)


def _mla_sample_kernel(pt_ref, q_ref, knew_ref, ckv_hbm, kpe_hbm, o_ref, ckv_buf, kpe_buf, sems, *, n_pages, t):
    b = pl.program_id(0)
    nb = pl.num_programs(0)
    page = ckv_hbm.shape[1]
    slot = b % 2

    def page_copies(elem, sl, p):
        idx = pt_ref[elem * n_pages + p]
        return (pltpu.make_async_copy(ckv_hbm.at[idx], ckv_buf.at[sl, pl.ds(p * page, page), :], sems.at[sl]),
                pltpu.make_async_copy(kpe_hbm.at[idx], kpe_buf.at[sl, :, pl.ds(p * page, page)], sems.at[sl]))

    def start_fetch(elem, sl):
        for p in range(n_pages):
            for cp in page_copies(elem, sl, p):
                cp.start()

    def wait_fetch(elem, sl):
        for p in range(n_pages):
            for cp in page_copies(elem, sl, p):
                cp.wait()

    @pl.when(b == 0)
    def _():
        start_fetch(0, 0)

    wait_fetch(b, slot)
    nxt = jnp.minimum(b + 1, nb - 1)
    start_fetch(nxt, 1 - slot)

    q = q_ref[0]
    kc = ckv_buf[slot].astype(BF16)
    kr = kpe_buf[slot].astype(BF16)
    s = _dot_nt(q[:, :MLA_KV_RANK], kc) + _dot(q[:, MLA_KV_RANK:], kr)
    m = jnp.max(s, axis=-1, keepdims=True)
    p = jnp.exp(s - m)
    l = jnp.sum(p, axis=-1, keepdims=True)
    acc = _dot(p.astype(BF16), kc)
    qf = q.astype(F32)
    kn = knew_ref[0].astype(F32)
    row_t = lax.broadcasted_iota(jnp.int32, (q.shape[0], 1), 0) % t
    for tk in range(t):
        krow = kn[tk:tk + 1, :]
        s_t = jnp.where(row_t >= tk, jnp.sum(qf * krow, axis=-1, keepdims=True), -jnp.inf)
        m_new = jnp.maximum(m, s_t)
        alpha = jnp.exp(m - m_new)
        p_t = jnp.exp(s_t - m_new)
        l = alpha * l + p_t
        acc = alpha * acc + p_t * krow[:, :MLA_KV_RANK]
        m = m_new
    o_ref[0] = (acc / l).astype(o_ref.dtype)

    @pl.when(b == nb - 1)
    def _():
        wait_fetch(nxt, 1 - slot)


def mla_attn_sample(q, knew, ckv_pool, kpe_pool_t, page_table):
    nb, rows, _ = q.shape
    t = knew.shape[1]
    n_pages = page_table.shape[1]
    page = ckv_pool.shape[1]
    grid_spec = pltpu.PrefetchScalarGridSpec(
        num_scalar_prefetch=1,
        grid=(nb,),
        in_specs=[pl.BlockSpec((1, rows, MLA_QK), lambda bi, pt: (bi, 0, 0)),
                  pl.BlockSpec((1, t, MLA_QK), lambda bi, pt: (bi, 0, 0)),
                  pl.BlockSpec(memory_space=pl.ANY), pl.BlockSpec(memory_space=pl.ANY)],
        out_specs=pl.BlockSpec((1, rows, MLA_KV_RANK), lambda bi, pt: (bi, 0, 0)),
        scratch_shapes=[pltpu.VMEM((2, n_pages * page, MLA_KV_RANK), F32),
                        pltpu.VMEM((2, MLA_ROPE, n_pages * page), F32),
                        pltpu.SemaphoreType.DMA((2,))],
    )
    return pl.pallas_call(
        functools.partial(_mla_sample_kernel, n_pages=n_pages, t=t),
        grid_spec=grid_spec,
        out_shape=jax.ShapeDtypeStruct((nb, rows, MLA_KV_RANK), BF16),
        compiler_params=_params("arbitrary"),
        name="mla_attn_sample",
    )(page_table.reshape(-1), q, knew, ckv_pool, kpe_pool_t)


def _mla_out_kernel(o_ref, wuv_ref, wo_ref, h_ref, y_ref):
    parts = [_dot(o_ref[h], wuv_ref[h]).astype(BF16) for h in range(o_ref.shape[0])]
    y_ref[...] = h_ref[...] + _dot(jnp.concatenate(parts, axis=-1), wo_ref[...])


def mla_out(o, w_uv, w_o, h, tm=512):
    nh, m, c = o.shape
    d = h.shape[1]
    tm = _row_tile(m, tm)
    return pl.pallas_call(
        _mla_out_kernel,
        grid=(m // tm,),
        in_specs=[pl.BlockSpec((nh, tm, c), lambda i: (0, i, 0)), _resident(w_uv.shape), _resident(w_o.shape),
                  pl.BlockSpec((tm, d), lambda i: (i, 0))],
        out_specs=pl.BlockSpec((tm, d), lambda i: (i, 0)),
        out_shape=jax.ShapeDtypeStruct((m, d), F32),
        compiler_params=_params("parallel"),
        name="mla_out",
    )(o, w_uv, w_o, h)


SSD_TAIL = SUBLANES


def _ssd_kernel(xbc_ref, z_ref, dt_ref, h0_ref, hist_ref, wc_ref, bc_ref, dtb_ref, alog_ref, dexp_ref, gn_ref,
                e_ref, et_ref, y_ref, hfin_ref, state_scr, tail_scr, full_scr, z_scr, dt_scr,
                *, q, qj, qb, nh, hd, ng, ns):
    c = pl.program_id(1)
    inner = nh * hd
    hpg = nh // ng
    gw = hpg * hd

    @pl.when(c == 0)
    def _():
        state_scr[...] = h0_ref[0]
        tail_scr[...] = hist_ref[0]

    full_scr[0:SSD_TAIL] = tail_scr[...]
    full_scr[SSD_TAIL:SSD_TAIL + qb] = xbc_ref[0]
    if qb < q:
        full_scr[SSD_TAIL + qb:SSD_TAIL + q] = jnp.zeros((q - qb, full_scr.shape[1]), F32)
        z_scr[0:qb] = z_ref[0]
        z_scr[qb:q] = jnp.zeros((q - qb, inner), F32)
        dt_scr[0:qb] = dt_ref[0]
        dt_scr[qb:q] = jnp.zeros((q - qb, LANES), F32)
        z, dt_raw = z_scr[...], dt_scr[...]
    else:
        tail_scr[...] = full_scr[q:q + SSD_TAIL]
        z, dt_raw = z_ref[0], dt_ref[0]

    off = SSD_TAIL - (SSD_CONV - 1)
    conv = bc_ref[...] + full_scr[off:off + q] * wc_ref[0:1]
    for k in range(1, SSD_CONV):
        conv = conv + full_scr[off + k:off + k + q] * wc_ref[k:k + 1]
    xc = _silu(conv)
    dt = jnp.maximum(dt_raw + dtb_ref[...], 0.0) + jnp.log1p(jnp.exp(-jnp.abs(dt_raw + dtb_ref[...])))
    if qb < q:
        live = lax.broadcasted_iota(jnp.int32, (q, 1), 0) < qb
        xc = jnp.where(live, xc, 0.0)
        dt = jnp.where(live, dt, 0.0)

    def pad_j(a):
        return a if qj == q else jnp.concatenate([a, jnp.zeros((qj - q,) + a.shape[1:], a.dtype)], axis=0)

    la = dt * (-jnp.exp(alog_ref[...]))
    ri = lax.broadcasted_iota(jnp.int32, (q, qj), 0)
    ci = lax.broadcasted_iota(jnp.int32, (q, qj), 1)
    causal = ri >= ci
    cs = _dot_sel_lhs(causal.astype(BF16), pad_j(la))
    cs_t = pad_j(cs).T
    cs_last = cs[q - 1:q, :]
    per_head = jnp.concatenate([dt, jnp.exp(cs), jnp.exp(cs_last - cs)], axis=0)
    spread = _dot_sel_rhs(per_head, e_ref[...])
    dt_x, ecs_x, edec_x = spread[0:q], spread[q:2 * q], spread[2 * q:3 * q]
    chunk_decay = jnp.broadcast_to(jnp.exp(cs_t[:, q - 1:q]), (LANES, LANES))
    decay_rows = _dot_sel_lhs(et_ref[...], chunk_decay)

    xs = xc[:, :inner]
    xdt = xs * dt_x
    xw_j = pad_j(xdt * edec_x)
    xdt_j = pad_j(xdt)
    states = [state_scr[g * gw:(g + 1) * gw, :] for g in range(ng)]
    per = LANES // hd
    lane_head = lax.broadcasted_iota(jnp.int32, (1, LANES), 1) // hd
    new_states, y_groups = [], []
    for g in range(ng):
        bg = pad_j(xc[:, inner + g * ns:inner + (g + 1) * ns]).astype(BF16)
        cg = xc[:, inner + ng * ns + g * ns:inner + ng * ns + (g + 1) * ns].astype(BF16)
        cb = _dot_nt(cg, bg)
        rows = slice(g * gw, (g + 1) * gw)
        st = states[g]
        y_off = _dot_nt(cg, st.astype(BF16)) * ecs_x[:, rows]
        new_states.append(decay_rows[rows, :] * st + _dot(xw_j[:, rows].T.astype(BF16), bg))
        y_diag = []
        for blk in range(gw // LANES):
            lanes = slice(g * gw + blk * LANES, g * gw + (blk + 1) * LANES)
            x_blk = xdt_j[:, lanes]
            acc = None
            for hh in range(per):
                h = (g * gw + blk * LANES) // hd + hh
                seg = cs[:, h:h + 1] - cs_t[h:h + 1, :]
                mat = (cb * jnp.exp(jnp.where(causal, seg, -jnp.inf))).astype(BF16)
                part = _dot(mat, jnp.where(lane_head == hh, x_blk, 0.0).astype(BF16))
                acc = part if acc is None else acc + part
            y_diag.append(acc)
        y_groups.append(y_off + jnp.concatenate(y_diag, axis=-1))
    for g in range(ng):
        state_scr[g * gw:(g + 1) * gw, :] = new_states[g]

    gated = (jnp.concatenate(y_groups, axis=-1) + xs * dexp_ref[...]) * _silu(z)
    normed = []
    for g in range(ng):
        grp = gated[:, g * gw:(g + 1) * gw]
        normed.append(grp * lax.rsqrt(jnp.mean(grp * grp, axis=-1, keepdims=True) + EPS))
    out = jnp.concatenate(normed, axis=-1) * gn_ref[...]
    y_ref[0] = out[0:qb]

    @pl.when(c == pl.num_programs(1) - 1)
    def _():
        hfin_ref[0] = state_scr[...]


def ssd_core(xbc, z, dt, h0, hist, w_conv, b_conv, dt_bias, a_log, d_skip, g_norm):
    b, l, conv_dim = xbc.shape
    inner = z.shape[2]
    nh = inner // SSD_HEAD_DIM
    q = SSD_CHUNK if l >= SSD_CHUNK else -(-l // SUBLANES) * SUBLANES
    qj = max(q, LANES)
    qb = min(l, q)
    assert l % qb == 0 and (qb == q or l == qb)
    nc = l // qb
    hp = nh * SSD_HEAD_DIM
    head_of_lane = jnp.arange(inner) // SSD_HEAD_DIM
    e = (jnp.arange(LANES)[:, None] == head_of_lane[None, :]).astype(BF16)
    pad = lambda v: jnp.pad(v.astype(F32), (0, LANES - nh)).reshape(1, LANES)
    row = lambda v: v.astype(F32).reshape(1, -1)
    blk = lambda w: pl.BlockSpec((1, qb, w), lambda bi, ci: (bi, ci, 0))
    per_b = lambda r, w: pl.BlockSpec((1, r, w), lambda bi, ci: (bi, 0, 0))
    return pl.pallas_call(
        functools.partial(_ssd_kernel, q=q, qj=qj, qb=qb, nh=nh, hd=SSD_HEAD_DIM, ng=SSD_GROUPS, ns=SSD_STATE),
        grid=(b, nc),
        in_specs=[blk(conv_dim), blk(inner), blk(LANES), per_b(hp, SSD_STATE), per_b(SSD_TAIL, conv_dim),
                  _resident((SSD_CONV, conv_dim)), _resident((1, conv_dim)), _resident((1, LANES)),
                  _resident((1, LANES)), _resident((1, inner)), _resident((1, inner)),
                  _resident((LANES, inner)), _resident((inner, LANES))],
        out_specs=[blk(inner), per_b(hp, SSD_STATE)],
        out_shape=[jax.ShapeDtypeStruct((b, l, inner), F32), jax.ShapeDtypeStruct((b, hp, SSD_STATE), F32)],
        scratch_shapes=[pltpu.VMEM((hp, SSD_STATE), F32), pltpu.VMEM((SSD_TAIL, conv_dim), F32),
                        pltpu.VMEM((SSD_TAIL + q, conv_dim), F32), pltpu.VMEM((q, inner), F32),
                        pltpu.VMEM((q, LANES), F32)],
        compiler_params=_params("parallel", "arbitrary"),
        name="ssd_core",
    )(xbc, z, dt, h0, hist, w_conv.astype(F32), row(b_conv), pad(dt_bias), pad(a_log),
      row(jnp.repeat(d_skip, SSD_HEAD_DIM)), row(g_norm), e, e.T)


def _t5_bucket(dist):
    max_exact = REL_BUCKETS // 2
    d = jnp.maximum(dist, 1).astype(F32)
    large = max_exact + (jnp.log(d / max_exact) / math.log(REL_MAX_DIST / max_exact)
                         * (REL_BUCKETS - max_exact)).astype(jnp.int32)
    return jnp.where(dist < max_exact, dist, jnp.minimum(large, REL_BUCKETS - 1))


def _group_bias(rel_bias, g, r, nk):
    tab = rel_bias[_t5_bucket(r * jnp.arange(nk + 1))]
    return tab[:, g * DIL_HEADS_PER_GROUP:(g + 1) * DIL_HEADS_PER_GROUP].T.astype(F32)


def _dil_prompt_kernel(q_ref, kc_ref, kp_ref, vc_ref, vp_ref, bias_ref, o_ref, lse_ref, *, blk, r, hd):
    i = pl.program_id(1)
    per = q_ref.shape[2] // hd
    qi = lax.broadcasted_iota(jnp.int32, (blk, 2 * blk), 0)
    ki = lax.broadcasted_iota(jnp.int32, (blk, 2 * blk), 1)
    dm = qi + blk - ki
    valid = (dm >= 0) & (dm <= blk) & ((ki >= blk) | (i > 0))
    biases = [bias_ref[hh] for hh in range(per)]
    results = []
    for c in range(r):
        rows = pl.ds(c, blk, stride=r) if r > 1 else slice(None)
        q = q_ref[0, rows, :].astype(BF16)
        k = jnp.concatenate([kp_ref[0, rows, :], kc_ref[0, rows, :]], axis=0).astype(BF16)
        v = jnp.concatenate([vp_ref[0, rows, :], vc_ref[0, rows, :]], axis=0).astype(BF16)
        outs, lses = [], []
        for hh in range(per):
            hs = slice(hh * hd, (hh + 1) * hd)
            s = _dot_nt(q[:, hs], k[:, hs]) * DIL_SCALE + biases[hh]
            s = jnp.where(valid, s, -jnp.inf)
            m = jnp.max(s, axis=-1, keepdims=True)
            p = jnp.exp(s - m)
            l = jnp.sum(p, axis=-1, keepdims=True)
            outs.append(_dot(p.astype(BF16), v[:, hs]) / l)
            lses.append(jnp.broadcast_to(m + jnp.log(l), (blk, hd)))
        results.append((rows, jnp.concatenate(outs, axis=-1), jnp.concatenate(lses, axis=-1)))
    for rows, o, lse in results:
        o_ref[0, rows, :] = o
        lse_ref[0, rows, :] = lse


def dil_attn_prompt(qkv, g, ng, r, bias_mat, blk, half):
    b, s, cols = qkv.shape
    nh = bias_mat.shape[0]
    hd = half // nh
    width = LANES if r > 1 else half
    per = width // hd
    tile = blk * r
    assert s % tile == 0 and cols == 3 * ng * half and half % width == 0
    lane_blocks = half // width

    def spec(kind, prev):
        def index(bi, i, hb):
            return (bi, jnp.maximum(i - 1, 0) if prev else i, (kind * ng + g) * lane_blocks + hb)
        return pl.BlockSpec((1, tile, width), index)

    out = pl.BlockSpec((1, tile, width), lambda bi, i, hb: (bi, i, hb))
    o, lse = pl.pallas_call(
        functools.partial(_dil_prompt_kernel, blk=blk, r=r, hd=hd),
        grid=(b, s // tile, lane_blocks),
        in_specs=[spec(0, False), spec(1, False), spec(1, True), spec(2, False), spec(2, True),
                  pl.BlockSpec((per, blk, 2 * blk), lambda bi, i, hb: (hb, 0, 0))],
        out_specs=[out, out],
        out_shape=[jax.ShapeDtypeStruct((b, s, half), F32)] * 2,
        compiler_params=_params("parallel", "arbitrary", "arbitrary"),
        name="dil_attn_prompt",
    )(qkv, qkv, qkv, qkv, qkv, bias_mat)
    return o.reshape(b * s, half), lse.reshape(b * s, half)


def _transpose_rows_kernel(k_ref, v_ref, o_ref):
    half = k_ref.shape[2]
    for c in range(half // LANES):
        cs = slice(c * LANES, (c + 1) * LANES)
        o_ref[0, 0, cs, :] = k_ref[0, :, cs].T
        o_ref[0, 1, cs, :] = v_ref[0, :, cs].T


def dil_state_prompt(qkv, g, ng, keep, half):
    b, s, cols = qkv.shape
    assert keep % LANES == 0 and (s - keep) % LANES == 0
    first = (s - keep) // LANES
    return pl.pallas_call(
        _transpose_rows_kernel,
        grid=(b, keep // LANES),
        in_specs=[pl.BlockSpec((1, LANES, half), lambda bi, i: (bi, first + i, ng + g)),
                  pl.BlockSpec((1, LANES, half), lambda bi, i: (bi, first + i, 2 * ng + g))],
        out_specs=pl.BlockSpec((1, 2, half, LANES), lambda bi, i: (bi, 0, 0, i)),
        out_shape=jax.ShapeDtypeStruct((b, 2, half, keep), F32),
        compiler_params=_params("parallel", "parallel"),
        name="dil_state_prompt",
    )(qkv, qkv)


def _dil_sample_kernel(buf_ref, q_ref, k_ref, v_ref, bias_ref, nbias_ref, hmask_ref, nbuf_ref, o_ref, lse_ref,
                       new_scr, *, eb, t, nh, hd):
    one = lambda ref, e: ref.at[pl.ds(e, 1)]
    for e in range(eb):
        _dil_shift_copy(one(buf_ref, e), one(k_ref, e), one(v_ref, e), one(nbuf_ref, e), new_scr.at[e], t=t)
    results = [_dil_decode_attn(one(buf_ref, e), one(q_ref, e), bias_ref, nbias_ref, hmask_ref, new_scr.at[e],
                                t=t, nh=nh, hd=hd) for e in range(eb)]
    for e, (o, lse) in enumerate(results):
        o_ref[e] = o
        lse_ref[e] = lse


def _dil_shift_copy(buf_ref, k_ref, v_ref, nbuf_ref, new_scr, *, t):
    w = buf_ref.shape[2]
    half = k_ref.shape[2]
    first_new = LANES - t
    new_scr[...] = jnp.zeros(new_scr.shape, F32)
    new_scr[first_new:, :half] = k_ref[0]
    new_scr[first_new:, half:] = v_ref[0]
    lane = lax.broadcasted_iota(jnp.int32, (LANES, LANES), 1)
    for c in range(2 * half // LANES):
        blk = slice(c * LANES, (c + 1) * LANES)
        rolled = pltpu.roll(buf_ref[0, blk, :], w - t, axis=1)
        nbuf_ref[0, blk, :] = rolled
        nbuf_ref[0, blk, w - LANES:] = jnp.where(lane >= first_new, new_scr[:, blk].T, rolled[:, w - LANES:])


def _dil_decode_attn(buf_ref, q_ref, bias_ref, nbias_ref, hmask_ref, new_scr, *, t, nh, hd):
    half = nh * hd
    q8 = jnp.concatenate([q_ref[0], jnp.zeros((SUBLANES - t, half), F32)], axis=0)
    new8 = new_scr[LANES - SUBLANES:, :]
    hmask = hmask_ref[...]
    q_heads = jnp.concatenate([q8] * nh, axis=0) * hmask
    k_t = buf_ref[0, 0:half, :].astype(BF16)
    v_t = buf_ref[0, half:, :].astype(BF16)
    s = _dot(q_heads.astype(BF16), k_t) * DIL_SCALE + bias_ref[...]
    m = jnp.max(s, axis=-1, keepdims=True)
    s_new = []
    for i in range(t):
        row = SUBLANES - t + i
        s_i = (jnp.sum(q_heads * new8[row:row + 1, :half], axis=-1, keepdims=True) * DIL_SCALE
               + nbias_ref[:, i:i + 1])
        s_new.append(s_i)
        m = jnp.maximum(m, s_i)
    p = jnp.exp(s - m)
    l = jnp.sum(p, axis=-1, keepdims=True)
    o = _dot_nt(p.astype(BF16), v_t)
    for i in range(t):
        row = SUBLANES - t + i
        p_i = jnp.exp(s_new[i] - m)
        l = l + p_i
        o = o + p_i * new8[row:row + 1, half:]
    o = (o / l) * hmask
    lse = (m + jnp.log(l)) * hmask
    blocks = lambda a: functools.reduce(lambda x, y: x + y, [a[h * SUBLANES:(h + 1) * SUBLANES] for h in range(nh)])
    return blocks(o)[0:t], blocks(lse)[0:t]


DIL_SAMPLE_BLOCK_BYTES = 4 * 1024 * 1024


def _sample_bias_tables(bias, r, nk, t):
    nh = bias.shape[0]
    w = r * nk
    neg = lambda *shape: jnp.full(shape, -jnp.inf, F32)
    rev = bias[:, nk:0:-1]
    up = rev if r == 1 else jnp.concatenate([rev[:, :, None], neg(nh, nk, r - 1)], axis=2).reshape(nh, w)
    rows = []
    for tt in range(SUBLANES):
        if tt >= t:
            rows.append(neg(nh, w))
        elif tt == 0:
            rows.append(up)
        else:
            rows.append(jnp.concatenate([neg(nh, tt), up[:, :w - tt]], axis=1))
    old = jnp.stack(rows, axis=1)
    cols = []
    for i in range(t):
        col = []
        for tt in range(SUBLANES):
            if tt >= t:
                col.append(jnp.zeros((nh, 1), F32))
            elif i <= tt and (tt - i) % r == 0:
                d = (tt - i) // r
                col.append(bias[:, d:d + 1])
            else:
                col.append(neg(nh, 1))
        cols.append(jnp.concatenate(col, axis=1))
    new = jnp.stack(cols, axis=2)
    return old, jnp.pad(new, ((0, 0), (0, 0), (0, LANES - t)))


def dil_attn_sample(buf_t, qkv, g, bias, r, nk):
    nb, kvw, w = buf_t.shape
    half = kvw // 2
    t = qkv.shape[1]
    n_groups = qkv.shape[2] // (3 * half)
    nh = bias.shape[0]
    assert w == r * nk, "the buffer holds exactly one window"
    assert t <= SUBLANES and w % LANES == 0
    bias_old, bias_new = _sample_bias_tables(bias, r, nk, t)
    bias_old = bias_old.reshape(nh * SUBLANES, w)
    bias_new = bias_new.reshape(nh * SUBLANES, LANES)
    hd = half // nh
    hmask = (jnp.arange(nh * SUBLANES)[:, None] // SUBLANES == jnp.arange(half)[None, :] // hd).astype(F32)
    eb = max(1, min(nb, DIL_SAMPLE_BLOCK_BYTES // (kvw * w * 4)))
    while nb % eb:
        eb -= 1
    col = lambda c: pl.BlockSpec((eb, t, half), lambda bi: (bi, 0, c))
    whole = pl.BlockSpec((eb, kvw, w), lambda bi: (bi, 0, 0))
    return pl.pallas_call(
        functools.partial(_dil_sample_kernel, eb=eb, t=t, nh=nh, hd=hd),
        grid=(nb // eb,),
        in_specs=[whole, col(g), col(n_groups + g), col(2 * n_groups + g),
                  _resident(bias_old.shape), _resident(bias_new.shape), _resident(hmask.shape)],
        out_specs=[whole, pl.BlockSpec((eb, t, half), lambda bi: (bi, 0, 0)),
                   pl.BlockSpec((eb, t, half), lambda bi: (bi, 0, 0))],
        out_shape=[jax.ShapeDtypeStruct((nb, kvw, w), F32), jax.ShapeDtypeStruct((nb, t, half), F32),
                   jax.ShapeDtypeStruct((nb, t, half), F32)],
        scratch_shapes=[pltpu.VMEM((eb, LANES, kvw), F32)],
        compiler_params=_params("parallel"),
        name="dil_attn_sample",
    )(buf_t, qkv, qkv, qkv, bias_old, bias_new, hmask)


def _band_bias(bias, nk):
    nh = bias.shape[0]
    period = 3 * nk
    v = jnp.concatenate([bias[:, ::-1], jnp.broadcast_to(bias[:, :1], (nh, nk - 1)),
                         jnp.broadcast_to(bias[:, nk:], (nh, nk))], axis=1)
    assert v.shape[1] == period
    skew = jnp.broadcast_to(v[:, None, :], (nh, nk, period)).reshape(nh, nk * period)
    return skew[:, :nk * (period - 1)].reshape(nh, nk, period - 1)[:, :, :2 * nk]


def _dil_combine_kernel(*refs, ng):
    o_refs, l_refs = refs[:ng], refs[ng:2 * ng]
    wo_ref, h_ref, y_ref = refs[2 * ng:]
    lses = [l_ref[...] for l_ref in l_refs]
    m = functools.reduce(jnp.maximum, lses)
    es = [jnp.exp(l - m) for l in lses]
    tot = functools.reduce(lambda a, b: a + b, es)
    o = functools.reduce(lambda a, b: a + b, [(e / tot) * o_ref[...] for e, o_ref in zip(es, o_refs)])
    y_ref[...] = h_ref[...] + _dot(o.astype(BF16), wo_ref[...])


def dil_combine_out(outs, lses, w_o, h, tm=512):
    m, w = outs[0].shape
    d = h.shape[1]
    tm = _row_tile(m, tm)
    ng = len(outs)
    rows = lambda width: pl.BlockSpec((tm, width), lambda i: (i, 0))
    return pl.pallas_call(
        functools.partial(_dil_combine_kernel, ng=ng),
        grid=(m // tm,),
        in_specs=[rows(w)] * (2 * ng) + [_resident(w_o.shape), rows(d)],
        out_specs=rows(d),
        out_shape=jax.ShapeDtypeStruct((m, d), F32),
        compiler_params=_params("parallel"),
        name="dil_combine_out",
    )(*outs, *lses, w_o, h)


def _conv_layer(hp, hs, g_mix, state, w_in, w_dw, b_dw, ln_g, ln_b, w_out, bp, s):
    d = hp.shape[1]
    w_in, w_out = w_in.astype(BF16), w_out.astype(BF16)
    hist = w_dw.shape[0] - 1
    glu_p = conv_in(hp, g_mix, w_in)
    glu_s = conv_in(hs, g_mix, w_in)
    glu_p3 = glu_p.reshape(bp, s, d)
    new_hp = conv_core_prompt(glu_p3, hp.reshape(bp, s, d), w_dw, b_dw, ln_g, ln_b, w_out).reshape(bp * s, d)
    new_hs, st_s = conv_core_sample(glu_s, state, hs, w_dw, b_dw, ln_g, ln_b, w_out)
    st_p = jnp.concatenate([jnp.zeros((bp, hist, d), F32), glu_p3], axis=1)[:, s:]
    return new_hp, new_hs, st_p, st_s


def _rope_tables(pos):
    inv = ROPE_THETA ** (-jnp.arange(0, MLA_ROPE, 2, dtype=F32) / MLA_ROPE)
    ang = pos.astype(F32)[:, None] * inv[None, :]
    cos, sin = jnp.cos(ang), jnp.sin(ang)
    reps = LANES // MLA_ROPE
    return jnp.tile(jnp.concatenate([cos, cos], axis=1), (1, reps)), jnp.tile(jnp.concatenate([-sin, sin], axis=1), (1, reps))


def _swap_halves(w):
    k = w.shape[0]
    w4 = w.reshape(k, -1, 2, MLA_ROPE // 2)
    return w4[:, :, ::-1, :].reshape(k, -1)


def _mla_layer(hp, hs, g_mix, ckv_pool, kpe_pool, page_table, w_dq, g_q, w_uq, w_dkv, g_kv, w_uk, w_uv, w_o, bp, s):
    nb = page_table.shape[0]
    t = hs.shape[0] // nb
    past = page_table.shape[1] * ckv_pool.shape[1]
    qr = w_dq.shape[1]
    uq = w_uq.reshape(qr, MLA_HEADS, MLA_NOPE + MLA_ROPE)
    uq_nope = uq[:, :, :MLA_NOPE].reshape(qr, -1)
    uq_rope = uq[:, :, MLA_NOPE:].reshape(qr, -1)
    dkv_rope = w_dkv[:, MLA_KV_RANK:]
    w = {"dq": w_dq.astype(BF16), "gq": g_q,
         "uq": jnp.concatenate([uq_nope, uq_rope, _swap_halves(uq_rope)], axis=1).astype(BF16),
         "dkv": jnp.concatenate([w_dkv[:, :MLA_KV_RANK], dkv_rope, _swap_halves(dkv_rope)], axis=1).astype(BF16),
         "gkv": g_kv, "uk": w_uk.astype(BF16)}
    w_uv, w_o = w_uv.astype(BF16), w_o.astype(BF16)

    tm_p = _row_tile(s, 256)
    cos_p, sin_p = _rope_tables(jnp.arange(s))
    q_p, kcat_p, ckv_p, kpe_p = mla_in(hp, g_mix, w, cos_p, sin_p, lambda i: i % (s // tm_p), tm=tm_p)
    o_p = mla_attn_prompt(q_p, kcat_p, bp, s)
    new_hp = mla_out(o_p, w_uv, w_o, hp)

    ms = hs.shape[0]
    tm_s = _row_tile(ms, 256)
    assert tm_s % t == 0
    cos_s, sin_s = _rope_tables(past + jnp.arange(t))
    cos_s, sin_s = jnp.tile(cos_s, (tm_s // t, 1)), jnp.tile(sin_s, (tm_s // t, 1))
    q_s, kcat_s, ckv_s, kpe_s = mla_in(hs, g_mix, w, cos_s, sin_s, lambda i: 0, tm=tm_s)
    q_rows = q_s.reshape(MLA_HEADS, nb, t, MLA_QK).transpose(1, 0, 2, 3).reshape(nb, MLA_HEADS * t, MLA_QK)
    o_s = mla_attn_sample(q_rows, kcat_s.reshape(nb, t, MLA_QK), ckv_pool, jnp.swapaxes(kpe_pool, 1, 2), page_table)
    o_s = o_s.reshape(nb, MLA_HEADS, t, MLA_KV_RANK).transpose(1, 0, 2, 3).reshape(MLA_HEADS, ms, MLA_KV_RANK)
    new_hs = mla_out(o_s, w_uv, w_o, hs)
    return (new_hp, new_hs, ckv_p.reshape(bp, s, -1), kpe_p.reshape(bp, s, -1),
            ckv_s.reshape(nb, t, -1), kpe_s.reshape(nb, t, -1))


def _ssd_layer(hp, hs, g_mix, conv_state, ssm_state, w_in, w_conv, b_conv, dt_bias, a_log, d_skip, g_norm, w_out,
               bp, s):
    nb = conv_state.shape[0]
    t = hs.shape[0] // nb
    inner = g_norm.shape[0]
    conv_dim = w_conv.shape[1]
    nh = dt_bias.shape[0]
    hist = SSD_CONV - 1
    ws = [w_in[:, :inner].astype(BF16), w_in[:, inner:inner + conv_dim].astype(BF16),
          jnp.pad(w_in[:, inner + conv_dim:], ((0, 0), (0, LANES - nh))).astype(BF16)]
    w_out = w_out.astype(BF16)
    prm = (w_conv, b_conv, dt_bias, a_log, d_skip, g_norm)

    def run(h, b, l, h0, hist_rows):
        z, xbc, dt = norm_matmul(h, g_mix, ws)
        xbc3 = xbc.reshape(b, l, conv_dim)
        hist8 = jnp.pad(hist_rows, ((0, 0), (SSD_TAIL - hist, 0), (0, 0)))
        y, h_fin = ssd_core(xbc3, z.reshape(b, l, inner), dt.reshape(b, l, LANES), h0, hist8, *prm)
        new_h = matmul_residual(y.reshape(b * l, inner), w_out, h)
        new_hist = jnp.concatenate([hist_rows, xbc3], axis=1)[:, l:]
        return new_h, new_hist, h_fin.reshape(b, nh, SSD_HEAD_DIM, SSD_STATE)

    zero_state = jnp.zeros((bp, nh * SSD_HEAD_DIM, SSD_STATE), F32)
    new_hp, hc_p, hh_p = run(hp, bp, s, zero_state, jnp.zeros((bp, hist, conv_dim), F32))
    new_hs, hc_s, hh_s = run(hs, nb, t, ssm_state.reshape(nb, nh * SSD_HEAD_DIM, SSD_STATE), conv_state)
    return new_hp, new_hs, hc_p, hh_p, hc_s, hh_s


def _dil_layer(hp, hs, g_mix, bufs_in, w_qkv, w_o, rel_bias, bp, s):
    nb = bufs_in[0].shape[0]
    t = hs.shape[0] // nb
    ng = len(DIL_PATTERNS)
    half = DIL_HEADS_PER_GROUP * DIL_HEAD_DIM
    w_qkv, w_o = w_qkv.astype(BF16), w_o.astype(BF16)
    (qkv_p,) = norm_matmul(hp, g_mix, [w_qkv])
    (qkv_s,) = norm_matmul(hs, g_mix, [w_qkv])
    qkv_p3 = qkv_p.reshape(bp, s, 3 * ng * half)
    qkv_s3 = qkv_s.reshape(nb, t, 3 * ng * half)
    nh, hd = DIL_HEADS_PER_GROUP, DIL_HEAD_DIM

    outs_p, lses_p, bufs_p, outs_s, lses_s, bufs_s = [], [], [], [], [], []
    for g, (win, r) in enumerate(DIL_PATTERNS):
        nk = win // r
        bias = _group_bias(rel_bias, g, r, nk)
        o, lse = dil_attn_prompt(qkv_p3, g, ng, r, _band_bias(bias, nk), nk, half)
        outs_p.append(o)
        lses_p.append(lse)
        keep = min(win, s)
        st = dil_state_prompt(qkv_p3, g, ng, keep, half)
        bufs_p.append(st.reshape(bp, 2, nh, hd, keep).transpose(0, 4, 1, 2, 3))
        buf = bufs_in[g]
        wb = buf.shape[1]
        buf_t = buf.transpose(0, 2, 3, 4, 1).reshape(nb, 2 * half, wb)
        nbuf_t, o, lse = dil_attn_sample(buf_t, qkv_s3, g, bias, r, nk)
        outs_s.append(o.reshape(nb * t, half))
        lses_s.append(lse.reshape(nb * t, half))
        bufs_s.append(nbuf_t.reshape(nb, 2, nh, hd, wb).transpose(0, 4, 1, 2, 3))
    new_hp = dil_combine_out(outs_p, lses_p, w_o, hp)
    new_hs = dil_combine_out(outs_s, lses_s, w_o, hs)
    return new_hp, new_hs, bufs_p, bufs_s


def kernel(x_prompt, x_sample, state_conv, cache_mla_ckv, cache_mla_kpe, state_ssd_conv, state_ssd, state_dil0_kv, state_dil1_kv, state_dil2_kv, page_table, p_prompt, p_sample, norm_mix, norm_ffn, norm_ple, norm_final, conv_w_in, conv_w_dw, conv_b_dw, conv_ln_g, conv_ln_b, conv_w_out, mla_w_dq, mla_g_q, mla_w_uq, mla_w_dkv, mla_g_kv, mla_w_uk, mla_w_uv, mla_w_o, ssd_w_in, ssd_w_conv, ssd_b_conv, ssd_dt_bias, ssd_a_log, ssd_d, ssd_g_norm, ssd_w_out, dil_w_qkv, dil_w_o, rel_bias, ffn_w1, ffn_w2, ple_w_gate, ple_w_proj):
    bp, s, d = x_prompt.shape
    nb, t, _ = x_sample.shape
    depth = norm_mix.shape[0]
    hp = x_prompt.reshape(bp * s, d)
    hs = x_sample.reshape(nb * t, d)
    conv_p, conv_s = [], []
    ckv_p, kpe_p, ckv_s, kpe_s = [], [], [], []
    ssdc_p, ssdh_p, ssdc_s, ssdh_s = [], [], [], []
    dil_p, dil_s = [[], [], []], [[], [], []]
    dil_in = (state_dil0_kv, state_dil1_kv, state_dil2_kv)
    for i in range(depth):
        kind, j = i % 4, i // 4
        if kind == 0:
            hp, hs, st_p, st_s = _conv_layer(hp, hs, norm_mix[i], state_conv[j], conv_w_in[j], conv_w_dw[j],
                                             conv_b_dw[j], conv_ln_g[j], conv_ln_b[j], conv_w_out[j], bp, s)
            conv_p.append(st_p)
            conv_s.append(st_s)
        elif kind == 1:
            hp, hs, c_p, r_p, c_s, r_s = _mla_layer(hp, hs, norm_mix[i], cache_mla_ckv[j], cache_mla_kpe[j], page_table,
                                                    mla_w_dq[j], mla_g_q[j], mla_w_uq[j], mla_w_dkv[j], mla_g_kv[j],
                                                    mla_w_uk[j], mla_w_uv[j], mla_w_o[j], bp, s)
            ckv_p.append(c_p)
            kpe_p.append(r_p)
            ckv_s.append(c_s)
            kpe_s.append(r_s)
        elif kind == 2:
            hp, hs, hc_p, hh_p, hc_s, hh_s = _ssd_layer(hp, hs, norm_mix[i], state_ssd_conv[j], state_ssd[j], ssd_w_in[j],
                                                        ssd_w_conv[j], ssd_b_conv[j], ssd_dt_bias[j], ssd_a_log[j],
                                                        ssd_d[j], ssd_g_norm[j], ssd_w_out[j], bp, s)
            ssdc_p.append(hc_p)
            ssdh_p.append(hh_p)
            ssdc_s.append(hc_s)
            ssdh_s.append(hh_s)
        else:
            hp, hs, bufs_p, bufs_s = _dil_layer(hp, hs, norm_mix[i], [b[j] for b in dil_in], dil_w_qkv[j], dil_w_o[j],
                                                rel_bias, bp, s)
            for g in range(len(DIL_PATTERNS)):
                dil_p[g].append(bufs_p[g])
                dil_s[g].append(bufs_s[g])
        final = i == depth - 1
        w1, w2 = ffn_w1[i].astype(BF16), ffn_w2[i].astype(BF16)
        wg, wp = ple_w_gate[i].astype(BF16), ple_w_proj[i].astype(BF16)
        hp = ffn(hp, norm_ffn[i], w1, w2)
        hs = ffn(hs, norm_ffn[i], w1, w2)
        hp = ple(hp, p_prompt[i].reshape(bp * s, -1), norm_ple[i], wg, wp, norm_final, final)
        hs = ple(hs, p_sample[i].reshape(nb * t, -1), norm_ple[i], wg, wp, norm_final, final)
    return (hp.reshape(bp, s, d), hs.reshape(nb, t, d),
            jnp.stack(conv_p), jnp.stack(conv_s),
            jnp.stack(ckv_p), jnp.stack(kpe_p), jnp.stack(ckv_s), jnp.stack(kpe_s),
            jnp.stack(ssdc_p), jnp.stack(ssdh_p), jnp.stack(ssdc_s), jnp.stack(ssdh_s),
            jnp.stack(dil_p[0]), jnp.stack(dil_p[1]), jnp.stack(dil_p[2]),
            jnp.stack(dil_s[0]), jnp.stack(dil_s[1]), jnp.stack(dil_s[2]))
```

```python
import functools
import math

import jax
import jax.numpy as jnp
from jax import lax
from jax.experimental import pallas as pl
from jax.experimental.pallas import tpu as pltpu

F32 = jnp.float32
BF16 = jnp.bfloat16
EPS = 1e-6

LANES = 128
SUBLANES = 8
VMEM_LIMIT_BYTES = 56 * 1024 * 1024

PAGE_SIZE = 128
CONV_WIDTH = 31
MLA_HEADS = 8
MLA_NOPE = 128
MLA_ROPE = 64
MLA_KV_RANK = 256
MLA_SCALE = (MLA_NOPE + MLA_ROPE) ** -0.5
ROPE_THETA = 10000.0
SSD_HEAD_DIM = 64
SSD_GROUPS = 4
SSD_STATE = 128
SSD_CONV = 4
SSD_CHUNK = 128
DIL_PATTERNS = ((128, 1), (512, 4), (2048, 16))
DIL_HEADS_PER_GROUP = 8
DIL_HEAD_DIM = 64
DIL_SCALE = DIL_HEAD_DIM ** -0.5
REL_BUCKETS = 32
REL_MAX_DIST = 2048


def _params(*sem):
    return pltpu.CompilerParams(dimension_semantics=sem, vmem_limit_bytes=VMEM_LIMIT_BYTES)


def _resident(shape):
    zeros = (0,) * len(shape)
    return pl.BlockSpec(shape, lambda *_: zeros)


def _rms(x, g):
    return x * lax.rsqrt(jnp.mean(x * x, axis=-1, keepdims=True) + EPS) * g


def _silu(x):
    return x * jax.nn.sigmoid(x)


def _dot(a, b):
    return jnp.dot(a, b, preferred_element_type=F32)


def _dot_nt(a, b):
    return lax.dot_general(a, b, (((1,), (1,)), ((), ())), preferred_element_type=F32)


def _split3(x):
    p1 = x.astype(BF16)
    r1 = x - p1.astype(F32)
    p2 = r1.astype(BF16)
    p3 = (r1 - p2.astype(F32)).astype(BF16)
    return p1, p2, p3


def _dot_sel_lhs(sel, x):
    return _dot(jnp.concatenate([sel] * 3, axis=1), jnp.concatenate(_split3(x), axis=0))


def _dot_sel_rhs(x, sel):
    return _dot(jnp.concatenate(_split3(x), axis=1), jnp.concatenate([sel] * 3, axis=0))


def _row_tile(m, want):
    t = min(m, want)
    assert m % t == 0, (m, t)
    return t


def _norm_matmul_kernel(x_ref, g_ref, *refs, n_w, chunk):
    w_refs, o_refs = refs[:n_w], refs[n_w:]
    xn = _rms(x_ref[...], g_ref[...]).astype(BF16)
    for w_ref, o_ref in zip(w_refs, o_refs):
        n = w_ref.shape[1]
        for c0 in range(0, n, chunk):
            c1 = min(c0 + chunk, n)
            o_ref[:, c0:c1] = _dot(xn, w_ref[:, c0:c1]).astype(o_ref.dtype)


def norm_matmul(x, g, ws, tm=256, chunk=512):
    m, k = x.shape
    tm = _row_tile(m, tm)
    return pl.pallas_call(
        functools.partial(_norm_matmul_kernel, n_w=len(ws), chunk=chunk),
        grid=(m // tm,),
        in_specs=[pl.BlockSpec((tm, k), lambda i: (i, 0)), _resident((1, k))]
        + [_resident(w.shape) for w in ws],
        out_specs=[pl.BlockSpec((tm, w.shape[1]), lambda i: (i, 0)) for w in ws],
        out_shape=[jax.ShapeDtypeStruct((m, w.shape[1]), F32) for w in ws],
        compiler_params=_params("parallel"),
        name="norm_matmul",
    )(x, g.reshape(1, k), *ws)


def _matmul_residual_kernel(a_ref, w_ref, h_ref, o_ref):
    o_ref[...] = h_ref[...] + _dot(a_ref[...].astype(BF16), w_ref[...])


def matmul_residual(a, w, h, tm=512):
    m, k = a.shape
    n = w.shape[1]
    tm = _row_tile(m, tm)
    return pl.pallas_call(
        _matmul_residual_kernel,
        grid=(m // tm,),
        in_specs=[pl.BlockSpec((tm, k), lambda i: (i, 0)), _resident(w.shape),
                  pl.BlockSpec((tm, n), lambda i: (i, 0))],
        out_specs=pl.BlockSpec((tm, n), lambda i: (i, 0)),
        out_shape=jax.ShapeDtypeStruct((m, n), F32),
        compiler_params=_params("parallel"),
        name="matmul_residual",
    )(a, w, h)


def _ffn_kernel(x_ref, g_ref, w1_ref, w2_ref, o_ref, xn_scr, acc_scr):
    j = pl.program_id(1)

    @pl.when(j == 0)
    def _():
        xn_scr[...] = _rms(x_ref[...], g_ref[...]).astype(BF16)
        acc_scr[...] = jnp.zeros_like(acc_scr)

    a = _dot(xn_scr[...], w1_ref[...])
    a = jnp.square(jnp.maximum(a, 0.0)).astype(BF16)
    acc_scr[...] += _dot(a, w2_ref[...])

    @pl.when(j == pl.num_programs(1) - 1)
    def _():
        o_ref[...] = x_ref[...] + acc_scr[...]


def ffn(x, g, w1, w2, layer, tm=1024, tf=1024):
    m, d = x.shape
    f = w1.shape[2]
    tm = _row_tile(m, tm)
    return pl.pallas_call(
        _ffn_kernel,
        grid=(m // tm, f // tf),
        in_specs=[pl.BlockSpec((tm, d), lambda i, j: (i, 0)), _resident((1, d)),
                  pl.BlockSpec((None, d, tf), lambda i, j: (layer, 0, j)),
                  pl.BlockSpec((None, tf, d), lambda i, j: (layer, j, 0))],
        out_specs=pl.BlockSpec((tm, d), lambda i, j: (i, 0)),
        out_shape=jax.ShapeDtypeStruct((m, d), F32),
        scratch_shapes=[pltpu.VMEM((tm, d), BF16), pltpu.VMEM((tm, d), F32)],
        compiler_params=_params("parallel", "arbitrary"),
        name="ffn",
    )(x, g.reshape(1, d), w1, w2)


def _ple_kernel(x_ref, p_ref, g_ref, wg_ref, wp_ref, gf_ref, o_ref, *, final):
    x = x_ref[...]
    xn = _rms(x, g_ref[...]).astype(BF16)
    gate = jax.nn.sigmoid(_dot(xn, wg_ref[...]))
    y = x + gate * _dot(p_ref[...].astype(BF16), wp_ref[...])
    if final:
        y = _rms(y, gf_ref[...])
    o_ref[...] = y


def ple(x, p, g, wg, wp, g_final, final, layer, tm=1024):
    m, d = x.shape
    pd = p.shape[2]
    tm = _row_tile(m, tm)
    of_layer = lambda shape: pl.BlockSpec((None,) + shape, lambda i: (layer, 0, 0))
    return pl.pallas_call(
        functools.partial(_ple_kernel, final=final),
        grid=(m // tm,),
        in_specs=[pl.BlockSpec((tm, d), lambda i: (i, 0)), pl.BlockSpec((None, tm, pd), lambda i: (layer, i, 0)),
                  _resident((1, d)), of_layer(wg.shape[1:]), of_layer(wp.shape[1:]), _resident((1, d))],
        out_specs=pl.BlockSpec((tm, d), lambda i: (i, 0)),
        out_shape=jax.ShapeDtypeStruct((m, d), F32),
        compiler_params=_params("parallel"),
        name="ple",
    )(x, p, g.reshape(1, d), wg, wp, g_final.reshape(1, d))


def _conv_in_kernel(x_ref, g_ref, w_ref, o_ref):
    xn = _rms(x_ref[...], g_ref[...]).astype(BF16)
    d = o_ref.shape[1]
    o_ref[...] = _dot(xn, w_ref[:, :d]) * jax.nn.sigmoid(_dot(xn, w_ref[:, d:]))


def conv_in(x, g, w_in, tm=1024):
    m, d = x.shape
    tm = _row_tile(m, tm)
    return pl.pallas_call(
        _conv_in_kernel,
        grid=(m // tm,),
        in_specs=[pl.BlockSpec((tm, d), lambda i: (i, 0)), _resident((1, d)), _resident(w_in.shape)],
        out_specs=pl.BlockSpec((tm, d), lambda i: (i, 0)),
        out_shape=jax.ShapeDtypeStruct((m, d), F32),
        compiler_params=_params("parallel"),
        name="conv_in",
    )(x, g.reshape(1, d), w_in)


def _ln_silu_out(c, h, lng_ref, lnb_ref, wo_ref):
    xc = c - jnp.mean(c, axis=-1, keepdims=True)
    y = xc * lax.rsqrt(jnp.mean(xc * xc, axis=-1, keepdims=True) + EPS) * lng_ref[...] + lnb_ref[...]
    return h + _dot(_silu(y).astype(BF16), wo_ref[...])


CONV_HALO = 32


def _conv_prompt_kernel(cur_ref, halo_ref, h_ref, wdw_ref, bdw_ref, lng_ref, lnb_ref, wo_ref, o_ref,
                        full_scr, *, ts, width):
    i = pl.program_id(1)
    d = cur_ref.shape[2]
    full_scr[0:CONV_HALO] = jnp.where(i > 0, halo_ref[0], 0.0)
    full_scr[CONV_HALO:CONV_HALO + ts] = cur_ref[0]
    off = CONV_HALO - (width - 1)
    cols = []
    for c in range(d // LANES):
        cs = slice(c * LANES, (c + 1) * LANES)
        fc = full_scr[:, cs]
        acc = jnp.broadcast_to(bdw_ref[:, cs], (ts, LANES))
        for phase in range(SUBLANES):
            taps = [k for k in range(width) if (off + k) % SUBLANES == phase]
            if not taps:
                continue
            shifted = pltpu.roll(fc, fc.shape[0] - phase, axis=0) if phase else fc
            for k in taps:
                base = off + k - phase
                acc = acc + shifted[base:base + ts] * wdw_ref[k:k + 1, cs]
        cols.append(acc)
    o_ref[0] = _ln_silu_out(jnp.concatenate(cols, axis=-1), h_ref[0], lng_ref, lnb_ref, wo_ref)


def conv_core_prompt(glu, h, w_dw, b_dw, ln_g, ln_b, w_out, ts=256):
    b, s, d = glu.shape
    ts = _row_tile(s, ts)
    width = w_dw.shape[0]
    assert width - 1 <= CONV_HALO and ts % CONV_HALO == 0
    per = ts // CONV_HALO
    vec = lambda v: v.reshape(1, d)
    return pl.pallas_call(
        functools.partial(_conv_prompt_kernel, ts=ts, width=width),
        grid=(b, s // ts),
        in_specs=[pl.BlockSpec((1, ts, d), lambda bi, i: (bi, i, 0)),
                  pl.BlockSpec((1, CONV_HALO, d), lambda bi, i: (bi, jnp.maximum(i * per - 1, 0), 0)),
                  pl.BlockSpec((1, ts, d), lambda bi, i: (bi, i, 0)),
                  _resident(w_dw.shape), _resident((1, d)), _resident((1, d)), _resident((1, d)),
                  _resident(w_out.shape)],
        out_specs=pl.BlockSpec((1, ts, d), lambda bi, i: (bi, i, 0)),
        out_shape=jax.ShapeDtypeStruct((b, s, d), F32),
        scratch_shapes=[pltpu.VMEM((CONV_HALO + ts, d), F32)],
        compiler_params=_params("parallel", "arbitrary"),
        name="conv_core_prompt",
    )(glu, glu, h, w_dw, vec(b_dw), vec(ln_g), vec(ln_b), w_out)


def _conv_sample_kernel(glu_ref, st_ref, h_ref, wdw_ref, bdw_ref, lng_ref, lnb_ref, wo_ref, o_ref, nst_ref,
                        full_scr, c_scr, *, bb, t, width):
    hist = width - 1
    d = glu_ref.shape[1]
    for bi in range(bb):
        full_scr[0:hist] = st_ref[bi]
        full_scr[hist:hist + t] = glu_ref[bi * t:(bi + 1) * t, :]
        acc = jnp.broadcast_to(bdw_ref[...], (t, d))
        for k in range(width):
            acc = acc + full_scr[k:k + t, :] * wdw_ref[k:k + 1, :]
        c_scr[bi * t:(bi + 1) * t, :] = acc
        nst_ref[bi] = full_scr[t:t + hist]
    o_ref[...] = _ln_silu_out(c_scr[...], h_ref[...], lng_ref, lnb_ref, wo_ref)


def conv_core_sample(glu, state, h, w_dw, b_dw, ln_g, ln_b, w_out, bb=8):
    nb, hist, d = state.shape
    t = glu.shape[0] // nb
    width = w_dw.shape[0]
    assert hist == width - 1 and nb % bb == 0
    vec = lambda v: v.reshape(1, d)
    return pl.pallas_call(
        functools.partial(_conv_sample_kernel, bb=bb, t=t, width=width),
        grid=(nb // bb,),
        in_specs=[pl.BlockSpec((bb * t, d), lambda i: (i, 0)),
                  pl.BlockSpec((bb, hist, d), lambda i: (i, 0, 0)),
                  pl.BlockSpec((bb * t, d), lambda i: (i, 0)),
                  _resident(w_dw.shape), _resident((1, d)), _resident((1, d)), _resident((1, d)),
                  _resident(w_out.shape)],
        out_specs=[pl.BlockSpec((bb * t, d), lambda i: (i, 0)),
                   pl.BlockSpec((bb, hist, d), lambda i: (i, 0, 0))],
        out_shape=[jax.ShapeDtypeStruct((nb * t, d), F32), jax.ShapeDtypeStruct((nb, hist, d), F32)],
        scratch_shapes=[pltpu.VMEM((hist + t + SUBLANES, d), F32), pltpu.VMEM((bb * t, d), F32)],
        compiler_params=_params("parallel"),
        name="conv_core_sample",
    )(glu, state, h, w_dw, vec(b_dw), vec(ln_g), vec(ln_b), w_out)


MLA_QK = MLA_KV_RANK + MLA_ROPE


def _mla_in_kernel(x_ref, g_ref, wdq_ref, gq_ref, wuq_ref, wdkv_ref, gkv_ref, wuk_ref, cos_ref, sin_ref,
                   q_ref, kcat_ref, ckv_ref, kpe_ref):
    nope_w = MLA_HEADS * MLA_NOPE
    rope_w = MLA_HEADS * MLA_ROPE
    xn = _rms(x_ref[...], g_ref[...]).astype(BF16)
    cq = _rms(_dot(xn, wdq_ref[...]), gq_ref[...]).astype(BF16)
    kv = _dot(xn, wdkv_ref[...])
    ckv = _rms(kv[:, :MLA_KV_RANK], gkv_ref[...])
    cos, sin = cos_ref[...], sin_ref[...]
    kpe = (kv[:, MLA_KV_RANK:MLA_QK] * cos[:, :MLA_ROPE]
           + kv[:, MLA_QK:MLA_QK + MLA_ROPE] * sin[:, :MLA_ROPE])
    qp = _dot(cq, wuq_ref[:, nope_w:nope_w + rope_w])
    qps = _dot(cq, wuq_ref[:, nope_w + rope_w:])
    qn = _dot(cq, wuq_ref[:, :nope_w]).astype(BF16)
    per = LANES // MLA_ROPE
    q_rope, q_lat = [], []
    for c in range(rope_w // LANES):
        cs = slice(c * LANES, (c + 1) * LANES)
        roped = ((qp[:, cs] * cos + qps[:, cs] * sin) * MLA_SCALE).astype(BF16)
        q_rope += [roped[:, hh * MLA_ROPE:(hh + 1) * MLA_ROPE] for hh in range(per)]
    for h in range(MLA_HEADS):
        q_lat.append((_dot(qn[:, h * MLA_NOPE:(h + 1) * MLA_NOPE], wuk_ref[h]) * MLA_SCALE).astype(BF16))
    ckv_ref[...] = ckv
    kpe_ref[...] = kpe
    kcat_ref[:, :MLA_KV_RANK] = ckv.astype(BF16)
    kcat_ref[:, MLA_KV_RANK:] = kpe.astype(BF16)
    for h in range(MLA_HEADS):
        q_ref[h, :, :MLA_KV_RANK] = q_lat[h]
        q_ref[h, :, MLA_KV_RANK:] = q_rope[h]


def mla_in(x, g, w, cos_tab, sin_tab, tab_index, tm=256):
    m, d = x.shape
    tm = _row_tile(m, tm)
    assert cos_tab.shape[0] % tm == 0 or cos_tab.shape[0] == tm
    return pl.pallas_call(
        _mla_in_kernel,
        grid=(m // tm,),
        in_specs=[pl.BlockSpec((tm, d), lambda i: (i, 0)), _resident((1, d)),
                  _resident(w["dq"].shape), _resident((1, w["dq"].shape[1])), _resident(w["uq"].shape),
                  _resident(w["dkv"].shape), _resident((1, MLA_KV_RANK)), _resident(w["uk"].shape),
                  pl.BlockSpec((tm, LANES), lambda i: (tab_index(i), 0)),
                  pl.BlockSpec((tm, LANES), lambda i: (tab_index(i), 0))],
        out_specs=[pl.BlockSpec((MLA_HEADS, tm, MLA_QK), lambda i: (0, i, 0)),
                   pl.BlockSpec((tm, MLA_QK), lambda i: (i, 0)),
                   pl.BlockSpec((tm, MLA_KV_RANK), lambda i: (i, 0)),
                   pl.BlockSpec((tm, MLA_ROPE), lambda i: (i, 0))],
        out_shape=[jax.ShapeDtypeStruct((MLA_HEADS, m, MLA_QK), BF16),
                   jax.ShapeDtypeStruct((m, MLA_QK), BF16),
                   jax.ShapeDtypeStruct((m, MLA_KV_RANK), F32),
                   jax.ShapeDtypeStruct((m, MLA_ROPE), F32)],
        compiler_params=_params("parallel"),
        name="mla_in",
    )(x, g.reshape(1, d), w["dq"], w["gq"].reshape(1, -1), w["uq"], w["dkv"], w["gkv"].reshape(1, -1),
      w["uk"], cos_tab, sin_tab)


def _mla_attn_kernel(q_ref, k_ref, o_ref, m_scr, l_scr, acc_scr, *, tq, tk):
    i, j = pl.program_id(1), pl.program_id(2)
    half = tk // 2
    nh = q_ref.shape[0]
    last_j = ((i + 1) * tq - 1) // tk

    @pl.when(j == 0)
    def _():
        m_scr[...] = jnp.full_like(m_scr, -jnp.inf)
        l_scr[...] = jnp.zeros_like(l_scr)
        acc_scr[...] = jnp.zeros_like(acc_scr)

    def step(masked, cols):
        k = k_ref[0:cols, :]
        kv = k[:, :MLA_KV_RANK]
        state = [(m_scr[h], l_scr[h], acc_scr[h]) for h in range(nh)]
        scores = [_dot_nt(q_ref[h], k) for h in range(nh)]
        if masked:
            row = lax.broadcasted_iota(jnp.int32, (tq, cols), 0)
            col = lax.broadcasted_iota(jnp.int32, (tq, cols), 1)
            keep = col + j * tk <= row + i * tq
        new_state = []
        for h in range(nh):
            s = jnp.where(keep, scores[h], -jnp.inf) if masked else scores[h]
            m_prev, l_prev, acc_prev = state[h]
            m_new = jnp.maximum(m_prev, jnp.max(s, axis=-1, keepdims=True))
            alpha = jnp.exp(m_prev - m_new)
            p = jnp.exp(s - m_new)
            l_new = alpha * l_prev + jnp.sum(p, axis=-1, keepdims=True)
            new_state.append((m_new, l_new, alpha * acc_prev + _dot(p.astype(BF16), kv)))
        for h in range(nh):
            m_scr[h], l_scr[h], acc_scr[h] = new_state[h]

    crosses_diagonal = j * tk + tk - 1 > i * tq
    first_half_only = (i + 1) * tq <= j * tk + half

    @pl.when((j <= last_j) & jnp.logical_not(crosses_diagonal))
    def _():
        step(False, tk)

    @pl.when((j <= last_j) & crosses_diagonal & jnp.logical_not(first_half_only))
    def _():
        step(True, tk)

    @pl.when((j <= last_j) & crosses_diagonal & first_half_only)
    def _():
        step(True, half)

    @pl.when(j == last_j)
    def _():
        o_ref[...] = (acc_scr[...] / l_scr[...]).astype(o_ref.dtype)


def mla_attn_prompt(q, kcat, b, s, tq=512, tk=1024):
    nh = q.shape[0]
    tq, tk = _row_tile(s, tq), _row_tile(s, tk)
    assert tq & (tq - 1) == 0
    nq, nk = s // tq, s // tk

    def k_index(bi, i, j):
        return (bi * nk + jnp.minimum(j, ((i + 1) * tq - 1) // tk), 0)

    return pl.pallas_call(
        functools.partial(_mla_attn_kernel, tq=tq, tk=tk),
        grid=(b, nq, nk),
        in_specs=[pl.BlockSpec((nh, tq, MLA_QK), lambda bi, i, j: (0, bi * nq + i, 0)),
                  pl.BlockSpec((tk, MLA_QK), k_index)],
        out_specs=pl.BlockSpec((nh, tq, MLA_KV_RANK), lambda bi, i, j: (0, bi * nq + i, 0)),
        out_shape=jax.ShapeDtypeStruct((nh, b * s, MLA_KV_RANK), BF16),
        scratch_shapes=[pltpu.VMEM((nh, tq, 1), F32), pltpu.VMEM((nh, tq, 1), F32),
                        pltpu.VMEM((nh, tq, MLA_KV_RANK), F32)],
        compiler_params=_params("parallel", "parallel", "arbitrary"),
        name="mla_attn_prompt",
    )(q, kcat)


def _mla_sample_kernel(pt_ref, q_ref, knew_ref, ckv_hbm, kpe_hbm, o_ref, ckv_buf, kpe_buf, sems, *, n_pages, t):
    b = pl.program_id(0)
    nb = pl.num_programs(0)
    page = ckv_hbm.shape[1]
    slot = b % 2

    def page_copies(elem, sl, p):
        idx = pt_ref[elem * n_pages + p]
        return (pltpu.make_async_copy(ckv_hbm.at[idx], ckv_buf.at[sl, pl.ds(p * page, page), :], sems.at[sl]),
                pltpu.make_async_copy(kpe_hbm.at[idx], kpe_buf.at[sl, :, pl.ds(p * page, page)], sems.at[sl]))

    def start_fetch(elem, sl):
        for p in range(n_pages):
            for cp in page_copies(elem, sl, p):
                cp.start()

    def wait_fetch(elem, sl):
        for p in range(n_pages):
            for cp in page_copies(elem, sl, p):
                cp.wait()

    @pl.when(b == 0)
    def _():
        start_fetch(0, 0)

    wait_fetch(b, slot)
    nxt = jnp.minimum(b + 1, nb - 1)
    start_fetch(nxt, 1 - slot)

    q = q_ref[0]
    kc = ckv_buf[slot].astype(BF16)
    kr = kpe_buf[slot].astype(BF16)
    s = _dot_nt(q[:, :MLA_KV_RANK], kc) + _dot(q[:, MLA_KV_RANK:], kr)
    m = jnp.max(s, axis=-1, keepdims=True)
    p = jnp.exp(s - m)
    l = jnp.sum(p, axis=-1, keepdims=True)
    acc = _dot(p.astype(BF16), kc)
    qf = q.astype(F32)
    kn = knew_ref[0].astype(F32)
    row_t = lax.broadcasted_iota(jnp.int32, (q.shape[0], 1), 0) % t
    for tk in range(t):
        krow = kn[tk:tk + 1, :]
        s_t = jnp.where(row_t >= tk, jnp.sum(qf * krow, axis=-1, keepdims=True), -jnp.inf)
        m_new = jnp.maximum(m, s_t)
        alpha = jnp.exp(m - m_new)
        p_t = jnp.exp(s_t - m_new)
        l = alpha * l + p_t
        acc = alpha * acc + p_t * krow[:, :MLA_KV_RANK]
        m = m_new
    o_ref[0] = (acc / l).astype(o_ref.dtype)

    @pl.when(b == nb - 1)
    def _():
        wait_fetch(nxt, 1 - slot)


def mla_attn_sample(q, knew, ckv_pool, kpe_pool_t, page_table):
    nb, rows, _ = q.shape
    t = knew.shape[1]
    n_pages = page_table.shape[1]
    page = ckv_pool.shape[1]
    grid_spec = pltpu.PrefetchScalarGridSpec(
        num_scalar_prefetch=1,
        grid=(nb,),
        in_specs=[pl.BlockSpec((1, rows, MLA_QK), lambda bi, pt: (bi, 0, 0)),
                  pl.BlockSpec((1, t, MLA_QK), lambda bi, pt: (bi, 0, 0)),
                  pl.BlockSpec(memory_space=pl.ANY), pl.BlockSpec(memory_space=pl.ANY)],
        out_specs=pl.BlockSpec((1, rows, MLA_KV_RANK), lambda bi, pt: (bi, 0, 0)),
        scratch_shapes=[pltpu.VMEM((2, n_pages * page, MLA_KV_RANK), F32),
                        pltpu.VMEM((2, MLA_ROPE, n_pages * page), F32),
                        pltpu.SemaphoreType.DMA((2,))],
    )
    return pl.pallas_call(
        functools.partial(_mla_sample_kernel, n_pages=n_pages, t=t),
        grid_spec=grid_spec,
        out_shape=jax.ShapeDtypeStruct((nb, rows, MLA_KV_RANK), BF16),
        compiler_params=_params("arbitrary"),
        name="mla_attn_sample",
    )(page_table.reshape(-1), q, knew, ckv_pool, kpe_pool_t)


def _mla_out_kernel(o_ref, wuv_ref, wo_ref, h_ref, y_ref):
    parts = [_dot(o_ref[h], wuv_ref[h]).astype(BF16) for h in range(o_ref.shape[0])]
    y_ref[...] = h_ref[...] + _dot(jnp.concatenate(parts, axis=-1), wo_ref[...])


def mla_out(o, w_uv, w_o, h, tm=1024):
    nh, m, c = o.shape
    d = h.shape[1]
    tm = _row_tile(m, tm)
    return pl.pallas_call(
        _mla_out_kernel,
        grid=(m // tm,),
        in_specs=[pl.BlockSpec((nh, tm, c), lambda i: (0, i, 0)), _resident(w_uv.shape), _resident(w_o.shape),
                  pl.BlockSpec((tm, d), lambda i: (i, 0))],
        out_specs=pl.BlockSpec((tm, d), lambda i: (i, 0)),
        out_shape=jax.ShapeDtypeStruct((m, d), F32),
        compiler_params=_params("parallel"),
        name="mla_out",
    )(o, w_uv, w_o, h)


SSD_TAIL = SUBLANES


def _ssd_kernel(xbc_ref, z_ref, dt_ref, h0_ref, hist_ref, wc_ref, bc_ref, dtb_ref, alog_ref, dexp_ref, gn_ref,
                e_ref, et_ref, y_ref, hfin_ref, state_scr, tail_scr, full_scr, z_scr, dt_scr,
                *, q, qj, qb, nh, hd, ng, ns):
    c = pl.program_id(1)
    inner = nh * hd
    hpg = nh // ng
    gw = hpg * hd

    @pl.when(c == 0)
    def _():
        state_scr[...] = h0_ref[0]
        tail_scr[...] = hist_ref[0]

    full_scr[0:SSD_TAIL] = tail_scr[...]
    full_scr[SSD_TAIL:SSD_TAIL + qb] = xbc_ref[0]
    if qb < q:
        full_scr[SSD_TAIL + qb:SSD_TAIL + q] = jnp.zeros((q - qb, full_scr.shape[1]), F32)
        z_scr[0:qb] = z_ref[0]
        z_scr[qb:q] = jnp.zeros((q - qb, inner), F32)
        dt_scr[0:qb] = dt_ref[0]
        dt_scr[qb:q] = jnp.zeros((q - qb, LANES), F32)
        z, dt_raw = z_scr[...], dt_scr[...]
    else:
        tail_scr[...] = full_scr[q:q + SSD_TAIL]
        z, dt_raw = z_ref[0], dt_ref[0]

    off = SSD_TAIL - (SSD_CONV - 1)
    conv = bc_ref[...] + full_scr[off:off + q] * wc_ref[0:1]
    for k in range(1, SSD_CONV):
        conv = conv + full_scr[off + k:off + k + q] * wc_ref[k:k + 1]
    xc = _silu(conv)
    dt = jnp.maximum(dt_raw + dtb_ref[...], 0.0) + jnp.log1p(jnp.exp(-jnp.abs(dt_raw + dtb_ref[...])))
    if qb < q:
        live = lax.broadcasted_iota(jnp.int32, (q, 1), 0) < qb
        xc = jnp.where(live, xc, 0.0)
        dt = jnp.where(live, dt, 0.0)

    def pad_j(a):
        return a if qj == q else jnp.concatenate([a, jnp.zeros((qj - q,) + a.shape[1:], a.dtype)], axis=0)

    la = dt * (-jnp.exp(alog_ref[...]))
    ri = lax.broadcasted_iota(jnp.int32, (q, qj), 0)
    ci = lax.broadcasted_iota(jnp.int32, (q, qj), 1)
    causal = ri >= ci
    cs = _dot_sel_lhs(causal.astype(BF16), pad_j(la))
    cs_t = pad_j(cs).T
    cs_last = cs[q - 1:q, :]
    per_head = jnp.concatenate([dt, jnp.exp(cs), jnp.exp(cs_last - cs)], axis=0)
    spread = _dot_sel_rhs(per_head, e_ref[...])
    dt_x, ecs_x, edec_x = spread[0:q], spread[q:2 * q], spread[2 * q:3 * q]
    chunk_decay = jnp.broadcast_to(jnp.exp(cs_t[:, q - 1:q]), (LANES, LANES))
    decay_rows = _dot_sel_lhs(et_ref[...], chunk_decay)

    xs = xc[:, :inner]
    xdt = xs * dt_x
    xw_j = pad_j(xdt * edec_x)
    xdt_j = pad_j(xdt)
    states = [state_scr[g * gw:(g + 1) * gw, :] for g in range(ng)]
    per = LANES // hd
    lane_head = lax.broadcasted_iota(jnp.int32, (1, LANES), 1) // hd
    new_states, y_groups = [], []
    for g in range(ng):
        bg = pad_j(xc[:, inner + g * ns:inner + (g + 1) * ns]).astype(BF16)
        cg = xc[:, inner + ng * ns + g * ns:inner + ng * ns + (g + 1) * ns].astype(BF16)
        cb = _dot_nt(cg, bg)
        rows = slice(g * gw, (g + 1) * gw)
        st = states[g]
        y_off = _dot_nt(cg, st.astype(BF16)) * ecs_x[:, rows]
        new_states.append(decay_rows[rows, :] * st + _dot(xw_j[:, rows].T.astype(BF16), bg))
        y_diag = []
        for blk in range(gw // LANES):
            lanes = slice(g * gw + blk * LANES, g * gw + (blk + 1) * LANES)
            x_blk = xdt_j[:, lanes]
            acc = None
            for hh in range(per):
                h = (g * gw + blk * LANES) // hd + hh
                seg = cs[:, h:h + 1] - cs_t[h:h + 1, :]
                mat = (cb * jnp.exp(jnp.where(causal, seg, -jnp.inf))).astype(BF16)
                part = _dot(mat, jnp.where(lane_head == hh, x_blk, 0.0).astype(BF16))
                acc = part if acc is None else acc + part
            y_diag.append(acc)
        y_groups.append(y_off + jnp.concatenate(y_diag, axis=-1))
    for g in range(ng):
        state_scr[g * gw:(g + 1) * gw, :] = new_states[g]

    gated = (jnp.concatenate(y_groups, axis=-1) + xs * dexp_ref[...]) * _silu(z)
    normed = []
    for g in range(ng):
        grp = gated[:, g * gw:(g + 1) * gw]
        normed.append(grp * lax.rsqrt(jnp.mean(grp * grp, axis=-1, keepdims=True) + EPS))
    out = jnp.concatenate(normed, axis=-1) * gn_ref[...]
    y_ref[0] = out[0:qb]

    @pl.when(c == pl.num_programs(1) - 1)
    def _():
        hfin_ref[0] = state_scr[...]


def ssd_core(xbc, z, dt, h0, hist, w_conv, b_conv, dt_bias, a_log, d_skip, g_norm):
    b, l, conv_dim = xbc.shape
    inner = z.shape[2]
    nh = inner // SSD_HEAD_DIM
    q = SSD_CHUNK if l >= SSD_CHUNK else -(-l // SUBLANES) * SUBLANES
    qj = max(q, LANES)
    qb = min(l, q)
    assert l % qb == 0 and (qb == q or l == qb)
    nc = l // qb
    hp = nh * SSD_HEAD_DIM
    head_of_lane = jnp.arange(inner) // SSD_HEAD_DIM
    e = (jnp.arange(LANES)[:, None] == head_of_lane[None, :]).astype(BF16)
    pad = lambda v: jnp.pad(v.astype(F32), (0, LANES - nh)).reshape(1, LANES)
    row = lambda v: v.astype(F32).reshape(1, -1)
    blk = lambda w: pl.BlockSpec((1, qb, w), lambda bi, ci: (bi, ci, 0))
    per_b = lambda r, w: pl.BlockSpec((1, r, w), lambda bi, ci: (bi, 0, 0))
    return pl.pallas_call(
        functools.partial(_ssd_kernel, q=q, qj=qj, qb=qb, nh=nh, hd=SSD_HEAD_DIM, ng=SSD_GROUPS, ns=SSD_STATE),
        grid=(b, nc),
        in_specs=[blk(conv_dim), blk(inner), blk(LANES), per_b(hp, SSD_STATE), per_b(SSD_TAIL, conv_dim),
                  _resident((SSD_CONV, conv_dim)), _resident((1, conv_dim)), _resident((1, LANES)),
                  _resident((1, LANES)), _resident((1, inner)), _resident((1, inner)),
                  _resident((LANES, inner)), _resident((inner, LANES))],
        out_specs=[blk(inner), per_b(hp, SSD_STATE)],
        out_shape=[jax.ShapeDtypeStruct((b, l, inner), F32), jax.ShapeDtypeStruct((b, hp, SSD_STATE), F32)],
        scratch_shapes=[pltpu.VMEM((hp, SSD_STATE), F32), pltpu.VMEM((SSD_TAIL, conv_dim), F32),
                        pltpu.VMEM((SSD_TAIL + q, conv_dim), F32), pltpu.VMEM((q, inner), F32),
                        pltpu.VMEM((q, LANES), F32)],
        compiler_params=_params("parallel", "arbitrary"),
        name="ssd_core",
    )(xbc, z, dt, h0, hist, w_conv.astype(F32), row(b_conv), pad(dt_bias), pad(a_log),
      row(jnp.repeat(d_skip, SSD_HEAD_DIM)), row(g_norm), e, e.T)


def _t5_bucket(dist):
    max_exact = REL_BUCKETS // 2
    d = jnp.maximum(dist, 1).astype(F32)
    large = max_exact + (jnp.log(d / max_exact) / math.log(REL_MAX_DIST / max_exact)
                         * (REL_BUCKETS - max_exact)).astype(jnp.int32)
    return jnp.where(dist < max_exact, dist, jnp.minimum(large, REL_BUCKETS - 1))


def _group_bias(rel_bias, g, r, nk):
    tab = rel_bias[_t5_bucket(r * jnp.arange(nk + 1))]
    return tab[:, g * DIL_HEADS_PER_GROUP:(g + 1) * DIL_HEADS_PER_GROUP].T.astype(F32)


def _dil_prompt_kernel(q_ref, kc_ref, kp_ref, vc_ref, vp_ref, bias_ref, o_ref, lse_ref, *, blk, r, hd):
    i = pl.program_id(1)
    per = q_ref.shape[2] // hd
    qi = lax.broadcasted_iota(jnp.int32, (blk, 2 * blk), 0)
    ki = lax.broadcasted_iota(jnp.int32, (blk, 2 * blk), 1)
    dm = qi + blk - ki
    valid = (dm >= 0) & (dm <= blk) & ((ki >= blk) | (i > 0))
    biases = [bias_ref[hh] for hh in range(per)]
    results = []
    for c in range(r):
        rows = pl.ds(c, blk, stride=r) if r > 1 else slice(None)
        q = q_ref[0, rows, :].astype(BF16)
        k = jnp.concatenate([kp_ref[0, rows, :], kc_ref[0, rows, :]], axis=0).astype(BF16)
        v = jnp.concatenate([vp_ref[0, rows, :], vc_ref[0, rows, :]], axis=0).astype(BF16)
        outs, lses = [], []
        for hh in range(per):
            hs = slice(hh * hd, (hh + 1) * hd)
            s = _dot_nt(q[:, hs], k[:, hs]) * DIL_SCALE + biases[hh]
            s = jnp.where(valid, s, -jnp.inf)
            m = jnp.max(s, axis=-1, keepdims=True)
            p = jnp.exp(s - m)
            l = jnp.sum(p, axis=-1, keepdims=True)
            outs.append(_dot(p.astype(BF16), v[:, hs]) / l)
            lses.append(jnp.broadcast_to(m + jnp.log(l), (blk, hd)))
        results.append((rows, jnp.concatenate(outs, axis=-1), jnp.concatenate(lses, axis=-1)))
    for rows, o, lse in results:
        o_ref[0, rows, :] = o
        lse_ref[0, rows, :] = lse


def dil_attn_prompt(qkv, g, ng, r, bias_mat, blk, half):
    b, s, cols = qkv.shape
    nh = bias_mat.shape[0]
    hd = half // nh
    width = LANES if r > 1 else half
    per = width // hd
    tile = blk * r
    assert s % tile == 0 and cols == 3 * ng * half and half % width == 0
    lane_blocks = half // width

    def spec(kind, prev):
        def index(bi, i, hb):
            return (bi, jnp.maximum(i - 1, 0) if prev else i, (kind * ng + g) * lane_blocks + hb)
        return pl.BlockSpec((1, tile, width), index)

    out = pl.BlockSpec((1, tile, width), lambda bi, i, hb: (bi, i, hb))
    o, lse = pl.pallas_call(
        functools.partial(_dil_prompt_kernel, blk=blk, r=r, hd=hd),
        grid=(b, s // tile, lane_blocks),
        in_specs=[spec(0, False), spec(1, False), spec(1, True), spec(2, False), spec(2, True),
                  pl.BlockSpec((per, blk, 2 * blk), lambda bi, i, hb: (hb, 0, 0))],
        out_specs=[out, out],
        out_shape=[jax.ShapeDtypeStruct((b, s, half), F32)] * 2,
        compiler_params=_params("parallel", "arbitrary", "arbitrary"),
        name="dil_attn_prompt",
    )(qkv, qkv, qkv, qkv, qkv, bias_mat)
    return o.reshape(b * s, half), lse.reshape(b * s, half)


def _transpose_rows_kernel(k_ref, v_ref, o_ref):
    half = k_ref.shape[2]
    for c in range(half // LANES):
        cs = slice(c * LANES, (c + 1) * LANES)
        o_ref[0, 0, cs, :] = k_ref[0, :, cs].T
        o_ref[0, 1, cs, :] = v_ref[0, :, cs].T


def dil_state_prompt(qkv, g, ng, keep, half):
    b, s, cols = qkv.shape
    assert keep % LANES == 0 and (s - keep) % LANES == 0
    first = (s - keep) // LANES
    return pl.pallas_call(
        _transpose_rows_kernel,
        grid=(b, keep // LANES),
        in_specs=[pl.BlockSpec((1, LANES, half), lambda bi, i: (bi, first + i, ng + g)),
                  pl.BlockSpec((1, LANES, half), lambda bi, i: (bi, first + i, 2 * ng + g))],
        out_specs=pl.BlockSpec((1, 2, half, LANES), lambda bi, i: (bi, 0, 0, i)),
        out_shape=jax.ShapeDtypeStruct((b, 2, half, keep), F32),
        compiler_params=_params("parallel", "parallel"),
        name="dil_state_prompt",
    )(qkv, qkv)


def _dil_sample_kernel(buf_ref, q_ref, k_ref, v_ref, bias_ref, nbias_ref, hmask_ref, nbuf_ref, o_ref, lse_ref,
                       new_scr, *, eb, t, nh, hd):
    one = lambda ref, e: ref.at[pl.ds(e, 1)]
    for e in range(eb):
        _dil_shift_copy(one(buf_ref, e), one(k_ref, e), one(v_ref, e), one(nbuf_ref, e), new_scr.at[e], t=t)
    results = [_dil_decode_attn(one(buf_ref, e), one(q_ref, e), bias_ref, nbias_ref, hmask_ref, new_scr.at[e],
                                t=t, nh=nh, hd=hd) for e in range(eb)]
    for e, (o, lse) in enumerate(results):
        o_ref[e] = o
        lse_ref[e] = lse


def _dil_shift_copy(buf_ref, k_ref, v_ref, nbuf_ref, new_scr, *, t):
    w = buf_ref.shape[2]
    half = k_ref.shape[2]
    first_new = LANES - t
    new_scr[...] = jnp.zeros(new_scr.shape, F32)
    new_scr[first_new:, :half] = k_ref[0]
    new_scr[first_new:, half:] = v_ref[0]
    lane = lax.broadcasted_iota(jnp.int32, (LANES, LANES), 1)
    for c in range(2 * half // LANES):
        blk = slice(c * LANES, (c + 1) * LANES)
        rolled = pltpu.roll(buf_ref[0, blk, :], w - t, axis=1)
        nbuf_ref[0, blk, :] = rolled
        nbuf_ref[0, blk, w - LANES:] = jnp.where(lane >= first_new, new_scr[:, blk].T, rolled[:, w - LANES:])


def _dil_decode_attn(buf_ref, q_ref, bias_ref, nbias_ref, hmask_ref, new_scr, *, t, nh, hd):
    half = nh * hd
    q8 = jnp.concatenate([q_ref[0], jnp.zeros((SUBLANES - t, half), F32)], axis=0)
    new8 = new_scr[LANES - SUBLANES:, :]
    hmask = hmask_ref[...]
    q_heads = jnp.concatenate([q8] * nh, axis=0) * hmask
    k_t = buf_ref[0, 0:half, :].astype(BF16)
    v_t = buf_ref[0, half:, :].astype(BF16)
    s = _dot(q_heads.astype(BF16), k_t) * DIL_SCALE + bias_ref[...]
    m = jnp.max(s, axis=-1, keepdims=True)
    s_new = []
    for i in range(t):
        row = SUBLANES - t + i
        s_i = (jnp.sum(q_heads * new8[row:row + 1, :half], axis=-1, keepdims=True) * DIL_SCALE
               + nbias_ref[:, i:i + 1])
        s_new.append(s_i)
        m = jnp.maximum(m, s_i)
    p = jnp.exp(s - m)
    l = jnp.sum(p, axis=-1, keepdims=True)
    o = _dot_nt(p.astype(BF16), v_t)
    for i in range(t):
        row = SUBLANES - t + i
        p_i = jnp.exp(s_new[i] - m)
        l = l + p_i
        o = o + p_i * new8[row:row + 1, half:]
    o = (o / l) * hmask
    lse = (m + jnp.log(l)) * hmask
    blocks = lambda a: functools.reduce(lambda x, y: x + y, [a[h * SUBLANES:(h + 1) * SUBLANES] for h in range(nh)])
    return blocks(o)[0:t], blocks(lse)[0:t]


DIL_SAMPLE_BLOCK_BYTES = 4 * 1024 * 1024


def _sample_bias_tables(bias, r, nk, t):
    nh = bias.shape[0]
    w = r * nk
    neg = lambda *shape: jnp.full(shape, -jnp.inf, F32)
    rev = bias[:, nk:0:-1]
    up = rev if r == 1 else jnp.concatenate([rev[:, :, None], neg(nh, nk, r - 1)], axis=2).reshape(nh, w)
    rows = []
    for tt in range(SUBLANES):
        if tt >= t:
            rows.append(neg(nh, w))
        elif tt == 0:
            rows.append(up)
        else:
            rows.append(jnp.concatenate([neg(nh, tt), up[:, :w - tt]], axis=1))
    old = jnp.stack(rows, axis=1)
    cols = []
    for i in range(t):
        col = []
        for tt in range(SUBLANES):
            if tt >= t:
                col.append(jnp.zeros((nh, 1), F32))
            elif i <= tt and (tt - i) % r == 0:
                d = (tt - i) // r
                col.append(bias[:, d:d + 1])
            else:
                col.append(neg(nh, 1))
        cols.append(jnp.concatenate(col, axis=1))
    new = jnp.stack(cols, axis=2)
    return old, jnp.pad(new, ((0, 0), (0, 0), (0, LANES - t)))


def dil_attn_sample(buf_t, qkv, g, bias, r, nk):
    nb, kvw, w = buf_t.shape
    half = kvw // 2
    t = qkv.shape[1]
    n_groups = qkv.shape[2] // (3 * half)
    nh = bias.shape[0]
    assert w == r * nk, "the buffer holds exactly one window"
    assert t <= SUBLANES and w % LANES == 0
    bias_old, bias_new = _sample_bias_tables(bias, r, nk, t)
    bias_old = bias_old.reshape(nh * SUBLANES, w)
    bias_new = bias_new.reshape(nh * SUBLANES, LANES)
    hd = half // nh
    hmask = (jnp.arange(nh * SUBLANES)[:, None] // SUBLANES == jnp.arange(half)[None, :] // hd).astype(F32)
    eb = max(1, min(nb, DIL_SAMPLE_BLOCK_BYTES // (kvw * w * 4)))
    while nb % eb:
        eb -= 1
    col = lambda c: pl.BlockSpec((eb, t, half), lambda bi: (bi, 0, c))
    whole = pl.BlockSpec((eb, kvw, w), lambda bi: (bi, 0, 0))
    return pl.pallas_call(
        functools.partial(_dil_sample_kernel, eb=eb, t=t, nh=nh, hd=hd),
        grid=(nb // eb,),
        in_specs=[whole, col(g), col(n_groups + g), col(2 * n_groups + g),
                  _resident(bias_old.shape), _resident(bias_new.shape), _resident(hmask.shape)],
        out_specs=[whole, pl.BlockSpec((eb, t, half), lambda bi: (bi, 0, 0)),
                   pl.BlockSpec((eb, t, half), lambda bi: (bi, 0, 0))],
        out_shape=[jax.ShapeDtypeStruct((nb, kvw, w), F32), jax.ShapeDtypeStruct((nb, t, half), F32),
                   jax.ShapeDtypeStruct((nb, t, half), F32)],
        scratch_shapes=[pltpu.VMEM((eb, LANES, kvw), F32)],
        compiler_params=_params("parallel"),
        name="dil_attn_sample",
    )(buf_t, qkv, qkv, qkv, bias_old, bias_new, hmask)


def _band_bias(bias, nk):
    nh = bias.shape[0]
    period = 3 * nk
    v = jnp.concatenate([bias[:, ::-1], jnp.broadcast_to(bias[:, :1], (nh, nk - 1)),
                         jnp.broadcast_to(bias[:, nk:], (nh, nk))], axis=1)
    assert v.shape[1] == period
    skew = jnp.broadcast_to(v[:, None, :], (nh, nk, period)).reshape(nh, nk * period)
    return skew[:, :nk * (period - 1)].reshape(nh, nk, period - 1)[:, :, :2 * nk]


def _dil_combine_kernel(*refs, ng):
    o_refs, l_refs = refs[:ng], refs[ng:2 * ng]
    wo_ref, h_ref, y_ref = refs[2 * ng:]
    lses = [l_ref[...] for l_ref in l_refs]
    m = functools.reduce(jnp.maximum, lses)
    es = [jnp.exp(l - m) for l in lses]
    tot = functools.reduce(lambda a, b: a + b, es)
    o = functools.reduce(lambda a, b: a + b, [(e / tot) * o_ref[...] for e, o_ref in zip(es, o_refs)])
    y_ref[...] = h_ref[...] + _dot(o.astype(BF16), wo_ref[...])


def dil_combine_out(outs, lses, w_o, h, tm=512):
    m, w = outs[0].shape
    d = h.shape[1]
    tm = _row_tile(m, tm)
    ng = len(outs)
    rows = lambda width: pl.BlockSpec((tm, width), lambda i: (i, 0))
    return pl.pallas_call(
        functools.partial(_dil_combine_kernel, ng=ng),
        grid=(m // tm,),
        in_specs=[rows(w)] * (2 * ng) + [_resident(w_o.shape), rows(d)],
        out_specs=rows(d),
        out_shape=jax.ShapeDtypeStruct((m, d), F32),
        compiler_params=_params("parallel"),
        name="dil_combine_out",
    )(*outs, *lses, w_o, h)


def _conv_layer(hp, hs, g_mix, state, w_in, w_dw, b_dw, ln_g, ln_b, w_out, bp, s):
    d = hp.shape[1]
    w_in, w_out = w_in.astype(BF16), w_out.astype(BF16)
    hist = w_dw.shape[0] - 1
    glu_p = conv_in(hp, g_mix, w_in)
    glu_s = conv_in(hs, g_mix, w_in)
    glu_p3 = glu_p.reshape(bp, s, d)
    new_hp = conv_core_prompt(glu_p3, hp.reshape(bp, s, d), w_dw, b_dw, ln_g, ln_b, w_out).reshape(bp * s, d)
    new_hs, st_s = conv_core_sample(glu_s, state, hs, w_dw, b_dw, ln_g, ln_b, w_out)
    st_p = jnp.concatenate([jnp.zeros((bp, hist, d), F32), glu_p3], axis=1)[:, s:]
    return new_hp, new_hs, st_p, st_s


def _rope_tables(pos):
    inv = ROPE_THETA ** (-jnp.arange(0, MLA_ROPE, 2, dtype=F32) / MLA_ROPE)
    ang = pos.astype(F32)[:, None] * inv[None, :]
    cos, sin = jnp.cos(ang), jnp.sin(ang)
    reps = LANES // MLA_ROPE
    return jnp.tile(jnp.concatenate([cos, cos], axis=1), (1, reps)), jnp.tile(jnp.concatenate([-sin, sin], axis=1), (1, reps))


def _swap_halves(w):
    k = w.shape[0]
    w4 = w.reshape(k, -1, 2, MLA_ROPE // 2)
    return w4[:, :, ::-1, :].reshape(k, -1)


def _mla_layer(hp, hs, g_mix, ckv_pool, kpe_pool, page_table, w_dq, g_q, w_uq, w_dkv, g_kv, w_uk, w_uv, w_o, bp, s):
    nb = page_table.shape[0]
    t = hs.shape[0] // nb
    past = page_table.shape[1] * ckv_pool.shape[1]
    qr = w_dq.shape[1]
    uq = w_uq.reshape(qr, MLA_HEADS, MLA_NOPE + MLA_ROPE)
    uq_nope = uq[:, :, :MLA_NOPE].reshape(qr, -1)
    uq_rope = uq[:, :, MLA_NOPE:].reshape(qr, -1)
    dkv_rope = w_dkv[:, MLA_KV_RANK:]
    w = {"dq": w_dq.astype(BF16), "gq": g_q,
         "uq": jnp.concatenate([uq_nope, uq_rope, _swap_halves(uq_rope)], axis=1).astype(BF16),
         "dkv": jnp.concatenate([w_dkv[:, :MLA_KV_RANK], dkv_rope, _swap_halves(dkv_rope)], axis=1).astype(BF16),
         "gkv": g_kv, "uk": w_uk.astype(BF16)}
    w_uv, w_o = w_uv.astype(BF16), w_o.astype(BF16)

    tm_p = _row_tile(s, 256)
    cos_p, sin_p = _rope_tables(jnp.arange(s))
    q_p, kcat_p, ckv_p, kpe_p = mla_in(hp, g_mix, w, cos_p, sin_p, lambda i: i % (s // tm_p), tm=tm_p)
    o_p = mla_attn_prompt(q_p, kcat_p, bp, s)
    new_hp = mla_out(o_p, w_uv, w_o, hp)

    ms = hs.shape[0]
    tm_s = _row_tile(ms, 256)
    assert tm_s % t == 0
    cos_s, sin_s = _rope_tables(past + jnp.arange(t))
    cos_s, sin_s = jnp.tile(cos_s, (tm_s // t, 1)), jnp.tile(sin_s, (tm_s // t, 1))
    q_s, kcat_s, ckv_s, kpe_s = mla_in(hs, g_mix, w, cos_s, sin_s, lambda i: 0, tm=tm_s)
    q_rows = q_s.reshape(MLA_HEADS, nb, t, MLA_QK).transpose(1, 0, 2, 3).reshape(nb, MLA_HEADS * t, MLA_QK)
    o_s = mla_attn_sample(q_rows, kcat_s.reshape(nb, t, MLA_QK), ckv_pool, jnp.swapaxes(kpe_pool, 1, 2), page_table)
    o_s = o_s.reshape(nb, MLA_HEADS, t, MLA_KV_RANK).transpose(1, 0, 2, 3).reshape(MLA_HEADS, ms, MLA_KV_RANK)
    new_hs = mla_out(o_s, w_uv, w_o, hs)
    return (new_hp, new_hs, ckv_p.reshape(bp, s, -1), kpe_p.reshape(bp, s, -1),
            ckv_s.reshape(nb, t, -1), kpe_s.reshape(nb, t, -1))


def _ssd_layer(hp, hs, g_mix, conv_state, ssm_state, w_in, w_conv, b_conv, dt_bias, a_log, d_skip, g_norm, w_out,
               bp, s):
    nb = conv_state.shape[0]
    t = hs.shape[0] // nb
    inner = g_norm.shape[0]
    conv_dim = w_conv.shape[1]
    nh = dt_bias.shape[0]
    hist = SSD_CONV - 1
    ws = [w_in[:, :inner].astype(BF16), w_in[:, inner:inner + conv_dim].astype(BF16),
          jnp.pad(w_in[:, inner + conv_dim:], ((0, 0), (0, LANES - nh))).astype(BF16)]
    w_out = w_out.astype(BF16)
    prm = (w_conv, b_conv, dt_bias, a_log, d_skip, g_norm)

    def run(h, b, l, h0, hist_rows):
        z, xbc, dt = norm_matmul(h, g_mix, ws)
        xbc3 = xbc.reshape(b, l, conv_dim)
        hist8 = jnp.pad(hist_rows, ((0, 0), (SSD_TAIL - hist, 0), (0, 0)))
        y, h_fin = ssd_core(xbc3, z.reshape(b, l, inner), dt.reshape(b, l, LANES), h0, hist8, *prm)
        new_h = matmul_residual(y.reshape(b * l, inner), w_out, h)
        new_hist = jnp.concatenate([hist_rows, xbc3], axis=1)[:, l:]
        return new_h, new_hist, h_fin.reshape(b, nh, SSD_HEAD_DIM, SSD_STATE)

    zero_state = jnp.zeros((bp, nh * SSD_HEAD_DIM, SSD_STATE), F32)
    new_hp, hc_p, hh_p = run(hp, bp, s, zero_state, jnp.zeros((bp, hist, conv_dim), F32))
    new_hs, hc_s, hh_s = run(hs, nb, t, ssm_state.reshape(nb, nh * SSD_HEAD_DIM, SSD_STATE), conv_state)
    return new_hp, new_hs, hc_p, hh_p, hc_s, hh_s


def _dil_layer(hp, hs, g_mix, bufs_in, w_qkv, w_o, rel_bias, bp, s):
    nb = bufs_in[0].shape[0]
    t = hs.shape[0] // nb
    ng = len(DIL_PATTERNS)
    half = DIL_HEADS_PER_GROUP * DIL_HEAD_DIM
    w_qkv, w_o = w_qkv.astype(BF16), w_o.astype(BF16)
    (qkv_p,) = norm_matmul(hp, g_mix, [w_qkv])
    (qkv_s,) = norm_matmul(hs, g_mix, [w_qkv])
    qkv_p3 = qkv_p.reshape(bp, s, 3 * ng * half)
    qkv_s3 = qkv_s.reshape(nb, t, 3 * ng * half)
    nh, hd = DIL_HEADS_PER_GROUP, DIL_HEAD_DIM

    outs_p, lses_p, bufs_p, outs_s, lses_s, bufs_s = [], [], [], [], [], []
    for g, (win, r) in enumerate(DIL_PATTERNS):
        nk = win // r
        bias = _group_bias(rel_bias, g, r, nk)
        o, lse = dil_attn_prompt(qkv_p3, g, ng, r, _band_bias(bias, nk), nk, half)
        outs_p.append(o)
        lses_p.append(lse)
        keep = min(win, s)
        st = dil_state_prompt(qkv_p3, g, ng, keep, half)
        bufs_p.append(st.reshape(bp, 2, nh, hd, keep).transpose(0, 4, 1, 2, 3))
        buf = bufs_in[g]
        wb = buf.shape[1]
        buf_t = buf.transpose(0, 2, 3, 4, 1).reshape(nb, 2 * half, wb)
        nbuf_t, o, lse = dil_attn_sample(buf_t, qkv_s3, g, bias, r, nk)
        outs_s.append(o.reshape(nb * t, half))
        lses_s.append(lse.reshape(nb * t, half))
        bufs_s.append(nbuf_t.reshape(nb, 2, nh, hd, wb).transpose(0, 4, 1, 2, 3))
    new_hp = dil_combine_out(outs_p, lses_p, w_o, hp)
    new_hs = dil_combine_out(outs_s, lses_s, w_o, hs)
    return new_hp, new_hs, bufs_p, bufs_s


def kernel(x_prompt, x_sample, state_conv, cache_mla_ckv, cache_mla_kpe, state_ssd_conv, state_ssd, state_dil0_kv, state_dil1_kv, state_dil2_kv, page_table, p_prompt, p_sample, norm_mix, norm_ffn, norm_ple, norm_final, conv_w_in, conv_w_dw, conv_b_dw, conv_ln_g, conv_ln_b, conv_w_out, mla_w_dq, mla_g_q, mla_w_uq, mla_w_dkv, mla_g_kv, mla_w_uk, mla_w_uv, mla_w_o, ssd_w_in, ssd_w_conv, ssd_b_conv, ssd_dt_bias, ssd_a_log, ssd_d, ssd_g_norm, ssd_w_out, dil_w_qkv, dil_w_o, rel_bias, ffn_w1, ffn_w2, ple_w_gate, ple_w_proj):
    bp, s, d = x_prompt.shape
    nb, t, _ = x_sample.shape
    depth = norm_mix.shape[0]
    hp = x_prompt.reshape(bp * s, d)
    hs = x_sample.reshape(nb * t, d)
    w1_all, w2_all = ffn_w1.astype(BF16), ffn_w2.astype(BF16)
    wg_all, wp_all = ple_w_gate.astype(BF16), ple_w_proj.astype(BF16)
    pp_all = p_prompt.reshape(depth, bp * s, -1)
    ps_all = p_sample.reshape(depth, nb * t, -1)
    conv_p, conv_s = [], []
    ckv_p, kpe_p, ckv_s, kpe_s = [], [], [], []
    ssdc_p, ssdh_p, ssdc_s, ssdh_s = [], [], [], []
    dil_p, dil_s = [[], [], []], [[], [], []]
    dil_in = (state_dil0_kv, state_dil1_kv, state_dil2_kv)
    for i in range(depth):
        kind, j = i % 4, i // 4
        if kind == 0:
            hp, hs, st_p, st_s = _conv_layer(hp, hs, norm_mix[i], state_conv[j], conv_w_in[j], conv_w_dw[j],
                                             conv_b_dw[j], conv_ln_g[j], conv_ln_b[j], conv_w_out[j], bp, s)
            conv_p.append(st_p)
            conv_s.append(st_s)
        elif kind == 1:
            hp, hs, c_p, r_p, c_s, r_s = _mla_layer(hp, hs, norm_mix[i], cache_mla_ckv[j], cache_mla_kpe[j], page_table,
                                                    mla_w_dq[j], mla_g_q[j], mla_w_uq[j], mla_w_dkv[j], mla_g_kv[j],
                                                    mla_w_uk[j], mla_w_uv[j], mla_w_o[j], bp, s)
            ckv_p.append(c_p)
            kpe_p.append(r_p)
            ckv_s.append(c_s)
            kpe_s.append(r_s)
        elif kind == 2:
            hp, hs, hc_p, hh_p, hc_s, hh_s = _ssd_layer(hp, hs, norm_mix[i], state_ssd_conv[j], state_ssd[j], ssd_w_in[j],
                                                        ssd_w_conv[j], ssd_b_conv[j], ssd_dt_bias[j], ssd_a_log[j],
                                                        ssd_d[j], ssd_g_norm[j], ssd_w_out[j], bp, s)
            ssdc_p.append(hc_p)
            ssdh_p.append(hh_p)
            ssdc_s.append(hc_s)
            ssdh_s.append(hh_s)
        else:
            hp, hs, bufs_p, bufs_s = _dil_layer(hp, hs, norm_mix[i], [b[j] for b in dil_in], dil_w_qkv[j], dil_w_o[j],
                                                rel_bias, bp, s)
            for g in range(len(DIL_PATTERNS)):
                dil_p[g].append(bufs_p[g])
                dil_s[g].append(bufs_s[g])
        final = i == depth - 1
        hp = ffn(hp, norm_ffn[i], w1_all, w2_all, i)
        hs = ffn(hs, norm_ffn[i], w1_all, w2_all, i)
        hp = ple(hp, pp_all, norm_ple[i], wg_all, wp_all, norm_final, final, i)
        hs = ple(hs, ps_all, norm_ple[i], wg_all, wp_all, norm_final, final, i)
    return (hp.reshape(bp, s, d), hs.reshape(nb, t, d),
            jnp.stack(conv_p), jnp.stack(conv_s),
            jnp.stack(ckv_p), jnp.stack(kpe_p), jnp.stack(ckv_s), jnp.stack(kpe_s),
            jnp.stack(ssdc_p), jnp.stack(ssdh_p), jnp.stack(ssdc_s), jnp.stack(ssdh_s),
            jnp.stack(dil_p[0]), jnp.stack(dil_p[1]), jnp.stack(dil_p[2]),
            jnp.stack(dil_s[0]), jnp.stack(dil_s[1]), jnp.stack(dil_s[2]))
```

```python
import functools
import math

import jax
import jax.numpy as jnp
from jax import lax
from jax.experimental import pallas as pl
from jax.experimental.pallas import tpu as pltpu

F32 = jnp.float32
BF16 = jnp.bfloat16
EPS = 1e-6

LANES = 128
SUBLANES = 8
VMEM_LIMIT_BYTES = 56 * 1024 * 1024

PAGE_SIZE = 128
CONV_WIDTH = 31
MLA_HEADS = 8
MLA_NOPE = 128
MLA_ROPE = 64
MLA_KV_RANK = 256
MLA_SCALE = (MLA_NOPE + MLA_ROPE) ** -0.5
ROPE_THETA = 10000.0
SSD_HEAD_DIM = 64
SSD_GROUPS = 4
SSD_STATE = 128
SSD_CONV = 4
SSD_CHUNK = 128
DIL_PATTERNS = ((128, 1), (512, 4), (2048, 16))
DIL_HEADS_PER_GROUP = 8
DIL_HEAD_DIM = 64
DIL_SCALE = DIL_HEAD_DIM ** -0.5
REL_BUCKETS = 32
REL_MAX_DIST = 2048


def _params(*sem):
    return pltpu.CompilerParams(dimension_semantics=sem, vmem_limit_bytes=VMEM_LIMIT_BYTES)


def _resident(shape):
    zeros = (0,) * len(shape)
    return pl.BlockSpec(shape, lambda *_: zeros)


def _rms(x, g):
    return x * lax.rsqrt(jnp.mean(x * x, axis=-1, keepdims=True) + EPS) * g


def _silu(x):
    return x * jax.nn.sigmoid(x)


def _dot(a, b):
    return jnp.dot(a, b, preferred_element_type=F32)


def _dot_nt(a, b):
    return lax.dot_general(a, b, (((1,), (1,)), ((), ())), preferred_element_type=F32)


def _split3(x):
    p1 = x.astype(BF16)
    r1 = x - p1.astype(F32)
    p2 = r1.astype(BF16)
    p3 = (r1 - p2.astype(F32)).astype(BF16)
    return p1, p2, p3


def _dot_sel_lhs(sel, x, stack=True):
    parts = _split3(x)
    if stack:
        return _dot(jnp.concatenate([sel] * 3, axis=1), jnp.concatenate(parts, axis=0))
    return _dot(sel, parts[0]) + _dot(sel, parts[1]) + _dot(sel, parts[2])


def _dot_sel_rhs(x, sel, stack=True):
    parts = _split3(x)
    if stack:
        return _dot(jnp.concatenate(parts, axis=1), jnp.concatenate([sel] * 3, axis=0))
    return _dot(parts[0], sel) + _dot(parts[1], sel) + _dot(parts[2], sel)


def _row_tile(m, want):
    t = min(m, want)
    assert m % t == 0, (m, t)
    return t


def _norm_matmul_kernel(x_ref, g_ref, *refs, n_w, chunk):
    w_refs, o_refs = refs[:n_w], refs[n_w:]
    xn = _rms(x_ref[...], g_ref[...]).astype(BF16)
    for w_ref, o_ref in zip(w_refs, o_refs):
        n = w_ref.shape[1]
        for c0 in range(0, n, chunk):
            c1 = min(c0 + chunk, n)
            o_ref[:, c0:c1] = _dot(xn, w_ref[:, c0:c1]).astype(o_ref.dtype)


def norm_matmul(x, g, ws, tm=256, chunk=512):
    m, k = x.shape
    tm = _row_tile(m, tm)
    return pl.pallas_call(
        functools.partial(_norm_matmul_kernel, n_w=len(ws), chunk=chunk),
        grid=(m // tm,),
        in_specs=[pl.BlockSpec((tm, k), lambda i: (i, 0)), _resident((1, k))]
        + [_resident(w.shape) for w in ws],
        out_specs=[pl.BlockSpec((tm, w.shape[1]), lambda i: (i, 0)) for w in ws],
        out_shape=[jax.ShapeDtypeStruct((m, w.shape[1]), F32) for w in ws],
        compiler_params=_params("parallel"),
        name="norm_matmul",
    )(x, g.reshape(1, k), *ws)


def _matmul_residual_kernel(a_ref, w_ref, h_ref, o_ref):
    o_ref[...] = h_ref[...] + _dot(a_ref[...].astype(BF16), w_ref[...])


def matmul_residual(a, w, h, tm=1024):
    m, k = a.shape
    n = w.shape[1]
    tm = _row_tile(m, tm)
    return pl.pallas_call(
        _matmul_residual_kernel,
        grid=(m // tm,),
        in_specs=[pl.BlockSpec((tm, k), lambda i: (i, 0)), _resident(w.shape),
                  pl.BlockSpec((tm, n), lambda i: (i, 0))],
        out_specs=pl.BlockSpec((tm, n), lambda i: (i, 0)),
        out_shape=jax.ShapeDtypeStruct((m, n), F32),
        compiler_params=_params("parallel"),
        name="matmul_residual",
    )(a, w, h)


def _ffn_kernel(x_ref, g_ref, w1_ref, w2_ref, o_ref, xn_scr, acc_scr):
    j = pl.program_id(1)

    @pl.when(j == 0)
    def _():
        xn_scr[...] = _rms(x_ref[...], g_ref[...]).astype(BF16)
        acc_scr[...] = jnp.zeros_like(acc_scr)

    a = _dot(xn_scr[...], w1_ref[...])
    a = jnp.square(jnp.maximum(a, 0.0)).astype(BF16)
    acc_scr[...] += _dot(a, w2_ref[...])

    @pl.when(j == pl.num_programs(1) - 1)
    def _():
        o_ref[...] = x_ref[...] + acc_scr[...]


def ffn(x, g, w1, w2, layer, tm=1024, tf=1024):
    m, d = x.shape
    f = w1.shape[2]
    tm = _row_tile(m, tm)
    return pl.pallas_call(
        _ffn_kernel,
        grid=(m // tm, f // tf),
        in_specs=[pl.BlockSpec((tm, d), lambda i, j: (i, 0)), _resident((1, d)),
                  pl.BlockSpec((None, d, tf), lambda i, j: (layer, 0, j)),
                  pl.BlockSpec((None, tf, d), lambda i, j: (layer, j, 0))],
        out_specs=pl.BlockSpec((tm, d), lambda i, j: (i, 0)),
        out_shape=jax.ShapeDtypeStruct((m, d), F32),
        scratch_shapes=[pltpu.VMEM((tm, d), BF16), pltpu.VMEM((tm, d), F32)],
        compiler_params=_params("parallel", "arbitrary"),
        name="ffn",
    )(x, g.reshape(1, d), w1, w2)


def _ple_kernel(x_ref, p_ref, g_ref, wg_ref, wp_ref, gf_ref, o_ref, *, final):
    x = x_ref[...]
    xn = _rms(x, g_ref[...]).astype(BF16)
    gate = jax.nn.sigmoid(_dot(xn, wg_ref[...]))
    y = x + gate * _dot(p_ref[...].astype(BF16), wp_ref[...])
    if final:
        y = _rms(y, gf_ref[...])
    o_ref[...] = y


def ple(x, p, g, wg, wp, g_final, final, layer, tm=1024):
    m, d = x.shape
    pd = p.shape[2]
    tm = _row_tile(m, tm)
    of_layer = lambda shape: pl.BlockSpec((None,) + shape, lambda i: (layer, 0, 0))
    return pl.pallas_call(
        functools.partial(_ple_kernel, final=final),
        grid=(m // tm,),
        in_specs=[pl.BlockSpec((tm, d), lambda i: (i, 0)), pl.BlockSpec((None, tm, pd), lambda i: (layer, i, 0)),
                  _resident((1, d)), of_layer(wg.shape[1:]), of_layer(wp.shape[1:]), _resident((1, d))],
        out_specs=pl.BlockSpec((tm, d), lambda i: (i, 0)),
        out_shape=jax.ShapeDtypeStruct((m, d), F32),
        compiler_params=_params("parallel"),
        name="ple",
    )(x, p, g.reshape(1, d), wg, wp, g_final.reshape(1, d))


def _conv_in_kernel(x_ref, g_ref, w_ref, o_ref):
    xn = _rms(x_ref[...], g_ref[...]).astype(BF16)
    d = o_ref.shape[1]
    o_ref[...] = _dot(xn, w_ref[:, :d]) * jax.nn.sigmoid(_dot(xn, w_ref[:, d:]))


def conv_in(x, g, w_in, tm=1024):
    m, d = x.shape
    tm = _row_tile(m, tm)
    return pl.pallas_call(
        _conv_in_kernel,
        grid=(m // tm,),
        in_specs=[pl.BlockSpec((tm, d), lambda i: (i, 0)), _resident((1, d)), _resident(w_in.shape)],
        out_specs=pl.BlockSpec((tm, d), lambda i: (i, 0)),
        out_shape=jax.ShapeDtypeStruct((m, d), F32),
        compiler_params=_params("parallel"),
        name="conv_in",
    )(x, g.reshape(1, d), w_in)


def _ln_silu_out(c, h, lng_ref, lnb_ref, wo_ref):
    xc = c - jnp.mean(c, axis=-1, keepdims=True)
    y = xc * lax.rsqrt(jnp.mean(xc * xc, axis=-1, keepdims=True) + EPS) * lng_ref[...] + lnb_ref[...]
    return h + _dot(_silu(y).astype(BF16), wo_ref[...])


CONV_HALO = 32


def _conv_prompt_kernel(cur_ref, halo_ref, h_ref, wdw_ref, bdw_ref, lng_ref, lnb_ref, wo_ref, o_ref,
                        full_scr, *, ts, width):
    i = pl.program_id(1)
    d = cur_ref.shape[2]
    full_scr[0:CONV_HALO] = jnp.where(i > 0, halo_ref[0], 0.0)
    full_scr[CONV_HALO:CONV_HALO + ts] = cur_ref[0]
    off = CONV_HALO - (width - 1)
    cols = []
    for c in range(d // LANES):
        cs = slice(c * LANES, (c + 1) * LANES)
        fc = full_scr[:, cs]
        acc = jnp.broadcast_to(bdw_ref[:, cs], (ts, LANES))
        for phase in range(SUBLANES):
            taps = [k for k in range(width) if (off + k) % SUBLANES == phase]
            if not taps:
                continue
            shifted = pltpu.roll(fc, fc.shape[0] - phase, axis=0) if phase else fc
            for k in taps:
                base = off + k - phase
                acc = acc + shifted[base:base + ts] * wdw_ref[k:k + 1, cs]
        cols.append(acc)
    o_ref[0] = _ln_silu_out(jnp.concatenate(cols, axis=-1), h_ref[0], lng_ref, lnb_ref, wo_ref)


def conv_core_prompt(glu, h, w_dw, b_dw, ln_g, ln_b, w_out, ts=256):
    b, s, d = glu.shape
    ts = _row_tile(s, ts)
    width = w_dw.shape[0]
    assert width - 1 <= CONV_HALO and ts % CONV_HALO == 0
    per = ts // CONV_HALO
    vec = lambda v: v.reshape(1, d)
    return pl.pallas_call(
        functools.partial(_conv_prompt_kernel, ts=ts, width=width),
        grid=(b, s // ts),
        in_specs=[pl.BlockSpec((1, ts, d), lambda bi, i: (bi, i, 0)),
                  pl.BlockSpec((1, CONV_HALO, d), lambda bi, i: (bi, jnp.maximum(i * per - 1, 0), 0)),
                  pl.BlockSpec((1, ts, d), lambda bi, i: (bi, i, 0)),
                  _resident(w_dw.shape), _resident((1, d)), _resident((1, d)), _resident((1, d)),
                  _resident(w_out.shape)],
        out_specs=pl.BlockSpec((1, ts, d), lambda bi, i: (bi, i, 0)),
        out_shape=jax.ShapeDtypeStruct((b, s, d), F32),
        scratch_shapes=[pltpu.VMEM((CONV_HALO + ts, d), F32)],
        compiler_params=_params("parallel", "arbitrary"),
        name="conv_core_prompt",
    )(glu, glu, h, w_dw, vec(b_dw), vec(ln_g), vec(ln_b), w_out)


def _conv_sample_kernel(glu_ref, st_ref, h_ref, wdw_ref, bdw_ref, lng_ref, lnb_ref, wo_ref, o_ref, nst_ref,
                        full_scr, c_scr, *, bb, t, width):
    hist = width - 1
    d = glu_ref.shape[1]
    for bi in range(bb):
        full_scr[0:hist] = st_ref[bi]
        full_scr[hist:hist + t] = glu_ref[bi * t:(bi + 1) * t, :]
        acc = jnp.broadcast_to(bdw_ref[...], (t, d))
        for k in range(width):
            acc = acc + full_scr[k:k + t, :] * wdw_ref[k:k + 1, :]
        c_scr[bi * t:(bi + 1) * t, :] = acc
        nst_ref[bi] = full_scr[t:t + hist]
    o_ref[...] = _ln_silu_out(c_scr[...], h_ref[...], lng_ref, lnb_ref, wo_ref)


def conv_core_sample(glu, state, h, w_dw, b_dw, ln_g, ln_b, w_out, bb=8):
    nb, hist, d = state.shape
    t = glu.shape[0] // nb
    width = w_dw.shape[0]
    assert hist == width - 1 and nb % bb == 0
    vec = lambda v: v.reshape(1, d)
    return pl.pallas_call(
        functools.partial(_conv_sample_kernel, bb=bb, t=t, width=width),
        grid=(nb // bb,),
        in_specs=[pl.BlockSpec((bb * t, d), lambda i: (i, 0)),
                  pl.BlockSpec((bb, hist, d), lambda i: (i, 0, 0)),
                  pl.BlockSpec((bb * t, d), lambda i: (i, 0)),
                  _resident(w_dw.shape), _resident((1, d)), _resident((1, d)), _resident((1, d)),
                  _resident(w_out.shape)],
        out_specs=[pl.BlockSpec((bb * t, d), lambda i: (i, 0)),
                   pl.BlockSpec((bb, hist, d), lambda i: (i, 0, 0))],
        out_shape=[jax.ShapeDtypeStruct((nb * t, d), F32), jax.ShapeDtypeStruct((nb, hist, d), F32)],
        scratch_shapes=[pltpu.VMEM((hist + t + SUBLANES, d), F32), pltpu.VMEM((bb * t, d), F32)],
        compiler_params=_params("parallel"),
        name="conv_core_sample",
    )(glu, state, h, w_dw, vec(b_dw), vec(ln_g), vec(ln_b), w_out)


MLA_QK = MLA_KV_RANK + MLA_ROPE


def _mla_in_kernel(x_ref, g_ref, wdq_ref, gq_ref, wuq_ref, wdkv_ref, gkv_ref, wuk_ref, cos_ref, sin_ref,
                   q_ref, kcat_ref, ckv_ref, kpe_ref):
    nope_w = MLA_HEADS * MLA_NOPE
    rope_w = MLA_HEADS * MLA_ROPE
    xn = _rms(x_ref[...], g_ref[...]).astype(BF16)
    cq = _rms(_dot(xn, wdq_ref[...]), gq_ref[...]).astype(BF16)
    kv = _dot(xn, wdkv_ref[...])
    ckv = _rms(kv[:, :MLA_KV_RANK], gkv_ref[...])
    cos, sin = cos_ref[...], sin_ref[...]
    kpe = (kv[:, MLA_KV_RANK:MLA_QK] * cos[:, :MLA_ROPE]
           + kv[:, MLA_QK:MLA_QK + MLA_ROPE] * sin[:, :MLA_ROPE])
    qp = _dot(cq, wuq_ref[:, nope_w:nope_w + rope_w])
    qps = _dot(cq, wuq_ref[:, nope_w + rope_w:])
    qn = _dot(cq, wuq_ref[:, :nope_w]).astype(BF16)
    per = LANES // MLA_ROPE
    q_rope, q_lat = [], []
    for c in range(rope_w // LANES):
        cs = slice(c * LANES, (c + 1) * LANES)
        roped = ((qp[:, cs] * cos + qps[:, cs] * sin) * MLA_SCALE).astype(BF16)
        q_rope += [roped[:, hh * MLA_ROPE:(hh + 1) * MLA_ROPE] for hh in range(per)]
    for h in range(MLA_HEADS):
        q_lat.append((_dot(qn[:, h * MLA_NOPE:(h + 1) * MLA_NOPE], wuk_ref[h]) * MLA_SCALE).astype(BF16))
    ckv_ref[...] = ckv
    kpe_ref[...] = kpe
    kcat_ref[:, :MLA_KV_RANK] = ckv.astype(BF16)
    kcat_ref[:, MLA_KV_RANK:] = kpe.astype(BF16)
    for h in range(MLA_HEADS):
        q_ref[h, :, :MLA_KV_RANK] = q_lat[h]
        q_ref[h, :, MLA_KV_RANK:] = q_rope[h]


def mla_in(x, g, w, cos_tab, sin_tab, tab_index, tm=256):
    m, d = x.shape
    tm = _row_tile(m, tm)
    assert cos_tab.shape[0] % tm == 0 or cos_tab.shape[0] == tm
    return pl.pallas_call(
        _mla_in_kernel,
        grid=(m // tm,),
        in_specs=[pl.BlockSpec((tm, d), lambda i: (i, 0)), _resident((1, d)),
                  _resident(w["dq"].shape), _resident((1, w["dq"].shape[1])), _resident(w["uq"].shape),
                  _resident(w["dkv"].shape), _resident((1, MLA_KV_RANK)), _resident(w["uk"].shape),
                  pl.BlockSpec((tm, LANES), lambda i: (tab_index(i), 0)),
                  pl.BlockSpec((tm, LANES), lambda i: (tab_index(i), 0))],
        out_specs=[pl.BlockSpec((MLA_HEADS, tm, MLA_QK), lambda i: (0, i, 0)),
                   pl.BlockSpec((tm, MLA_QK), lambda i: (i, 0)),
                   pl.BlockSpec((tm, MLA_KV_RANK), lambda i: (i, 0)),
                   pl.BlockSpec((tm, MLA_ROPE), lambda i: (i, 0))],
        out_shape=[jax.ShapeDtypeStruct((MLA_HEADS, m, MLA_QK), BF16),
                   jax.ShapeDtypeStruct((m, MLA_QK), BF16),
                   jax.ShapeDtypeStruct((m, MLA_KV_RANK), F32),
                   jax.ShapeDtypeStruct((m, MLA_ROPE), F32)],
        compiler_params=_params("parallel"),
        name="mla_in",
    )(x, g.reshape(1, d), w["dq"], w["gq"].reshape(1, -1), w["uq"], w["dkv"], w["gkv"].reshape(1, -1),
      w["uk"], cos_tab, sin_tab)


def _mla_attn_kernel(q_ref, k_ref, o_ref, m_scr, l_scr, acc_scr, *, tq, tk):
    i, j = pl.program_id(1), pl.program_id(2)
    half = tk // 2
    nh = q_ref.shape[0]
    last_j = ((i + 1) * tq - 1) // tk

    @pl.when(j == 0)
    def _():
        m_scr[...] = jnp.full_like(m_scr, -jnp.inf)
        l_scr[...] = jnp.zeros_like(l_scr)
        acc_scr[...] = jnp.zeros_like(acc_scr)

    def step(masked, cols):
        k = k_ref[0:cols, :]
        kv = k[:, :MLA_KV_RANK]
        state = [(m_scr[h], l_scr[h], acc_scr[h]) for h in range(nh)]
        scores = [_dot_nt(q_ref[h], k) for h in range(nh)]
        if masked:
            row = lax.broadcasted_iota(jnp.int32, (tq, cols), 0)
            col = lax.broadcasted_iota(jnp.int32, (tq, cols), 1)
            keep = col + j * tk <= row + i * tq
        new_state = []
        for h in range(nh):
            s = jnp.where(keep, scores[h], -jnp.inf) if masked else scores[h]
            m_prev, l_prev, acc_prev = state[h]
            m_new = jnp.maximum(m_prev, jnp.max(s, axis=-1, keepdims=True))
            alpha = jnp.exp(m_prev - m_new)
            p = jnp.exp(s - m_new)
            l_new = alpha * l_prev + jnp.sum(p, axis=-1, keepdims=True)
            new_state.append((m_new, l_new, alpha * acc_prev + _dot(p.astype(BF16), kv)))
        for h in range(nh):
            m_scr[h], l_scr[h], acc_scr[h] = new_state[h]

    crosses_diagonal = j * tk + tk - 1 > i * tq
    first_half_only = (i + 1) * tq <= j * tk + half

    @pl.when((j <= last_j) & jnp.logical_not(crosses_diagonal))
    def _():
        step(False, tk)

    @pl.when((j <= last_j) & crosses_diagonal & jnp.logical_not(first_half_only))
    def _():
        step(True, tk)

    @pl.when((j <= last_j) & crosses_diagonal & first_half_only)
    def _():
        step(True, half)

    @pl.when(j == last_j)
    def _():
        o_ref[...] = (acc_scr[...] / l_scr[...]).astype(o_ref.dtype)


def mla_attn_prompt(q, kcat, b, s, tq=512, tk=1024):
    nh = q.shape[0]
    tq, tk = _row_tile(s, tq), _row_tile(s, tk)
    assert tq & (tq - 1) == 0
    nq, nk = s // tq, s // tk

    def k_index(bi, i, j):
        return (bi * nk + jnp.minimum(j, ((i + 1) * tq - 1) // tk), 0)

    return pl.pallas_call(
        functools.partial(_mla_attn_kernel, tq=tq, tk=tk),
        grid=(b, nq, nk),
        in_specs=[pl.BlockSpec((nh, tq, MLA_QK), lambda bi, i, j: (0, bi * nq + i, 0)),
                  pl.BlockSpec((tk, MLA_QK), k_index)],
        out_specs=pl.BlockSpec((nh, tq, MLA_KV_RANK), lambda bi, i, j: (0, bi * nq + i, 0)),
        out_shape=jax.ShapeDtypeStruct((nh, b * s, MLA_KV_RANK), BF16),
        scratch_shapes=[pltpu.VMEM((nh, tq, 1), F32), pltpu.VMEM((nh, tq, 1), F32),
                        pltpu.VMEM((nh, tq, MLA_KV_RANK), F32)],
        compiler_params=_params("parallel", "parallel", "arbitrary"),
        name="mla_attn_prompt",
    )(q, kcat)


def _mla_sample_kernel(pt_ref, q_ref, knew_ref, ckv_hbm, kpe_hbm, o_ref, ckv_buf, kpe_buf, sems, *, n_pages, t):
    b = pl.program_id(0)
    nb = pl.num_programs(0)
    page = ckv_hbm.shape[1]
    slot = b % 2

    def page_copies(elem, sl, p):
        idx = pt_ref[elem * n_pages + p]
        return (pltpu.make_async_copy(ckv_hbm.at[idx], ckv_buf.at[sl, pl.ds(p * page, page), :], sems.at[sl]),
                pltpu.make_async_copy(kpe_hbm.at[idx], kpe_buf.at[sl, :, pl.ds(p * page, page)], sems.at[sl]))

    def start_fetch(elem, sl):
        for p in range(n_pages):
            for cp in page_copies(elem, sl, p):
                cp.start()

    def wait_fetch(elem, sl):
        for p in range(n_pages):
            for cp in page_copies(elem, sl, p):
                cp.wait()

    @pl.when(b == 0)
    def _():
        start_fetch(0, 0)

    wait_fetch(b, slot)
    nxt = jnp.minimum(b + 1, nb - 1)
    start_fetch(nxt, 1 - slot)

    q = q_ref[0]
    kc = ckv_buf[slot].astype(BF16)
    kr = kpe_buf[slot].astype(BF16)
    s = _dot_nt(q[:, :MLA_KV_RANK], kc) + _dot(q[:, MLA_KV_RANK:], kr)
    m = jnp.max(s, axis=-1, keepdims=True)
    p = jnp.exp(s - m)
    l = jnp.sum(p, axis=-1, keepdims=True)
    acc = _dot(p.astype(BF16), kc)
    qf = q.astype(F32)
    kn = knew_ref[0].astype(F32)
    row_t = lax.broadcasted_iota(jnp.int32, (q.shape[0], 1), 0) % t
    for tk in range(t):
        krow = kn[tk:tk + 1, :]
        s_t = jnp.where(row_t >= tk, jnp.sum(qf * krow, axis=-1, keepdims=True), -jnp.inf)
        m_new = jnp.maximum(m, s_t)
        alpha = jnp.exp(m - m_new)
        p_t = jnp.exp(s_t - m_new)
        l = alpha * l + p_t
        acc = alpha * acc + p_t * krow[:, :MLA_KV_RANK]
        m = m_new
    o_ref[0] = (acc / l).astype(o_ref.dtype)

    @pl.when(b == nb - 1)
    def _():
        wait_fetch(nxt, 1 - slot)


def mla_attn_sample(q, knew, ckv_pool, kpe_pool_t, page_table):
    nb, rows, _ = q.shape
    t = knew.shape[1]
    n_pages = page_table.shape[1]
    page = ckv_pool.shape[1]
    grid_spec = pltpu.PrefetchScalarGridSpec(
        num_scalar_prefetch=1,
        grid=(nb,),
        in_specs=[pl.BlockSpec((1, rows, MLA_QK), lambda bi, pt: (bi, 0, 0)),
                  pl.BlockSpec((1, t, MLA_QK), lambda bi, pt: (bi, 0, 0)),
                  pl.BlockSpec(memory_space=pl.ANY), pl.BlockSpec(memory_space=pl.ANY)],
        out_specs=pl.BlockSpec((1, rows, MLA_KV_RANK), lambda bi, pt: (bi, 0, 0)),
        scratch_shapes=[pltpu.VMEM((2, n_pages * page, MLA_KV_RANK), F32),
                        pltpu.VMEM((2, MLA_ROPE, n_pages * page), F32),
                        pltpu.SemaphoreType.DMA((2,))],
    )
    return pl.pallas_call(
        functools.partial(_mla_sample_kernel, n_pages=n_pages, t=t),
        grid_spec=grid_spec,
        out_shape=jax.ShapeDtypeStruct((nb, rows, MLA_KV_RANK), BF16),
        compiler_params=_params("arbitrary"),
        name="mla_attn_sample",
    )(page_table.reshape(-1), q, knew, ckv_pool, kpe_pool_t)


def _mla_out_kernel(o_ref, wuv_ref, wo_ref, h_ref, y_ref):
    parts = [_dot(o_ref[h], wuv_ref[h]).astype(BF16) for h in range(o_ref.shape[0])]
    y_ref[...] = h_ref[...] + _dot(jnp.concatenate(parts, axis=-1), wo_ref[...])


def mla_out(o, w_uv, w_o, h, tm=1024):
    nh, m, c = o.shape
    d = h.shape[1]
    tm = _row_tile(m, tm)
    return pl.pallas_call(
        _mla_out_kernel,
        grid=(m // tm,),
        in_specs=[pl.BlockSpec((nh, tm, c), lambda i: (0, i, 0)), _resident(w_uv.shape), _resident(w_o.shape),
                  pl.BlockSpec((tm, d), lambda i: (i, 0))],
        out_specs=pl.BlockSpec((tm, d), lambda i: (i, 0)),
        out_shape=jax.ShapeDtypeStruct((m, d), F32),
        compiler_params=_params("parallel"),
        name="mla_out",
    )(o, w_uv, w_o, h)


SSD_TAIL = SUBLANES


def _ssd_kernel(xbc_ref, z_ref, dt_ref, h0_ref, hist_ref, wc_ref, bc_ref, dtb_ref, alog_ref, dexp_ref, gn_ref,
                e_ref, et_ref, y_ref, hfin_ref, state_scr, tail_scr, full_scr, z_scr, dt_scr,
                *, q, qj, qb, nh, hd, ng, ns):
    c = pl.program_id(1)
    inner = nh * hd
    hpg = nh // ng
    gw = hpg * hd

    @pl.when(c == 0)
    def _():
        state_scr[...] = h0_ref[0]
        tail_scr[...] = hist_ref[0]

    full_scr[0:SSD_TAIL] = tail_scr[...]
    full_scr[SSD_TAIL:SSD_TAIL + qb] = xbc_ref[0]
    if qb < q:
        full_scr[SSD_TAIL + qb:SSD_TAIL + q] = jnp.zeros((q - qb, full_scr.shape[1]), F32)
        z_scr[0:qb] = z_ref[0]
        z_scr[qb:q] = jnp.zeros((q - qb, inner), F32)
        dt_scr[0:qb] = dt_ref[0]
        dt_scr[qb:q] = jnp.zeros((q - qb, LANES), F32)
        z, dt_raw = z_scr[...], dt_scr[...]
    else:
        tail_scr[...] = full_scr[q:q + SSD_TAIL]
        z, dt_raw = z_ref[0], dt_ref[0]

    off = SSD_TAIL - (SSD_CONV - 1)
    conv = bc_ref[...] + full_scr[off:off + q] * wc_ref[0:1]
    for k in range(1, SSD_CONV):
        conv = conv + full_scr[off + k:off + k + q] * wc_ref[k:k + 1]
    xc = _silu(conv)
    dt = jnp.maximum(dt_raw + dtb_ref[...], 0.0) + jnp.log1p(jnp.exp(-jnp.abs(dt_raw + dtb_ref[...])))
    if qb < q:
        live = lax.broadcasted_iota(jnp.int32, (q, 1), 0) < qb
        xc = jnp.where(live, xc, 0.0)
        dt = jnp.where(live, dt, 0.0)

    def pad_j(a):
        return a if qj == q else jnp.concatenate([a, jnp.zeros((qj - q,) + a.shape[1:], a.dtype)], axis=0)

    la = dt * (-jnp.exp(alog_ref[...]))
    ri = lax.broadcasted_iota(jnp.int32, (q, qj), 0)
    ci = lax.broadcasted_iota(jnp.int32, (q, qj), 1)
    causal = ri >= ci
    full_chunk = qj == q
    cs = _dot_sel_lhs(causal.astype(BF16), pad_j(la), full_chunk)
    cs_t = pad_j(cs).T
    cs_last = cs[q - 1:q, :]
    per_head = jnp.concatenate([dt, jnp.exp(cs), jnp.exp(cs_last - cs)], axis=0)
    spread = _dot_sel_rhs(per_head, e_ref[...], full_chunk)
    dt_x, ecs_x, edec_x = spread[0:q], spread[q:2 * q], spread[2 * q:3 * q]
    chunk_decay = jnp.broadcast_to(jnp.exp(cs_t[:, q - 1:q]), (LANES, LANES))
    decay_rows = _dot_sel_lhs(et_ref[...], chunk_decay, full_chunk)

    xs = xc[:, :inner]
    xdt = xs * dt_x
    xw_j = pad_j(xdt * edec_x)
    xdt_j = pad_j(xdt)
    states = [state_scr[g * gw:(g + 1) * gw, :] for g in range(ng)]
    per = LANES // hd
    lane_head = lax.broadcasted_iota(jnp.int32, (1, LANES), 1) // hd
    new_states, y_groups = [], []
    for g in range(ng):
        bg = pad_j(xc[:, inner + g * ns:inner + (g + 1) * ns]).astype(BF16)
        cg = xc[:, inner + ng * ns + g * ns:inner + ng * ns + (g + 1) * ns].astype(BF16)
        cb = _dot_nt(cg, bg)
        rows = slice(g * gw, (g + 1) * gw)
        st = states[g]
        y_off = _dot_nt(cg, st.astype(BF16)) * ecs_x[:, rows]
        new_states.append(decay_rows[rows, :] * st + _dot(xw_j[:, rows].T.astype(BF16), bg))
        y_diag = []
        for blk in range(gw // LANES):
            lanes = slice(g * gw + blk * LANES, g * gw + (blk + 1) * LANES)
            x_blk = xdt_j[:, lanes]
            acc = None
            for hh in range(per):
                h = (g * gw + blk * LANES) // hd + hh
                seg = cs[:, h:h + 1] - cs_t[h:h + 1, :]
                mat = (cb * jnp.exp(jnp.where(causal, seg, -jnp.inf))).astype(BF16)
                part = _dot(mat, jnp.where(lane_head == hh, x_blk, 0.0).astype(BF16))
                acc = part if acc is None else acc + part
            y_diag.append(acc)
        y_groups.append(y_off + jnp.concatenate(y_diag, axis=-1))
    for g in range(ng):
        state_scr[g * gw:(g + 1) * gw, :] = new_states[g]

    gated = (jnp.concatenate(y_groups, axis=-1) + xs * dexp_ref[...]) * _silu(z)
    normed = []
    for g in range(ng):
        grp = gated[:, g * gw:(g + 1) * gw]
        normed.append(grp * lax.rsqrt(jnp.mean(grp * grp, axis=-1, keepdims=True) + EPS))
    out = jnp.concatenate(normed, axis=-1) * gn_ref[...]
    y_ref[0] = out[0:qb]

    @pl.when(c == pl.num_programs(1) - 1)
    def _():
        hfin_ref[0] = state_scr[...]


def ssd_core(xbc, z, dt, h0, hist, w_conv, b_conv, dt_bias, a_log, d_skip, g_norm):
    b, l, conv_dim = xbc.shape
    inner = z.shape[2]
    nh = inner // SSD_HEAD_DIM
    q = SSD_CHUNK if l >= SSD_CHUNK else -(-l // SUBLANES) * SUBLANES
    qj = max(q, LANES)
    qb = min(l, q)
    assert l % qb == 0 and (qb == q or l == qb)
    nc = l // qb
    hp = nh * SSD_HEAD_DIM
    head_of_lane = jnp.arange(inner) // SSD_HEAD_DIM
    e = (jnp.arange(LANES)[:, None] == head_of_lane[None, :]).astype(BF16)
    pad = lambda v: jnp.pad(v.astype(F32), (0, LANES - nh)).reshape(1, LANES)
    row = lambda v: v.astype(F32).reshape(1, -1)
    blk = lambda w: pl.BlockSpec((1, qb, w), lambda bi, ci: (bi, ci, 0))
    per_b = lambda r, w: pl.BlockSpec((1, r, w), lambda bi, ci: (bi, 0, 0))
    return pl.pallas_call(
        functools.partial(_ssd_kernel, q=q, qj=qj, qb=qb, nh=nh, hd=SSD_HEAD_DIM, ng=SSD_GROUPS, ns=SSD_STATE),
        grid=(b, nc),
        in_specs=[blk(conv_dim), blk(inner), blk(LANES), per_b(hp, SSD_STATE), per_b(SSD_TAIL, conv_dim),
                  _resident((SSD_CONV, conv_dim)), _resident((1, conv_dim)), _resident((1, LANES)),
                  _resident((1, LANES)), _resident((1, inner)), _resident((1, inner)),
                  _resident((LANES, inner)), _resident((inner, LANES))],
        out_specs=[blk(inner), per_b(hp, SSD_STATE)],
        out_shape=[jax.ShapeDtypeStruct((b, l, inner), F32), jax.ShapeDtypeStruct((b, hp, SSD_STATE), F32)],
        scratch_shapes=[pltpu.VMEM((hp, SSD_STATE), F32), pltpu.VMEM((SSD_TAIL, conv_dim), F32),
                        pltpu.VMEM((SSD_TAIL + q, conv_dim), F32), pltpu.VMEM((q, inner), F32),
                        pltpu.VMEM((q, LANES), F32)],
        compiler_params=_params("parallel", "arbitrary"),
        name="ssd_core",
    )(xbc, z, dt, h0, hist, w_conv.astype(F32), row(b_conv), pad(dt_bias), pad(a_log),
      row(jnp.repeat(d_skip, SSD_HEAD_DIM)), row(g_norm), e, e.T)


def _t5_bucket(dist):
    max_exact = REL_BUCKETS // 2
    d = jnp.maximum(dist, 1).astype(F32)
    large = max_exact + (jnp.log(d / max_exact) / math.log(REL_MAX_DIST / max_exact)
                         * (REL_BUCKETS - max_exact)).astype(jnp.int32)
    return jnp.where(dist < max_exact, dist, jnp.minimum(large, REL_BUCKETS - 1))


def _group_bias(rel_bias, g, r, nk):
    tab = rel_bias[_t5_bucket(r * jnp.arange(nk + 1))]
    return tab[:, g * DIL_HEADS_PER_GROUP:(g + 1) * DIL_HEADS_PER_GROUP].T.astype(F32)


def _dil_prompt_kernel(q_ref, kc_ref, kp_ref, vc_ref, vp_ref, bias_ref, o_ref, lse_ref, *, blk, r, hd):
    i = pl.program_id(1)
    per = q_ref.shape[2] // hd
    qi = lax.broadcasted_iota(jnp.int32, (blk, 2 * blk), 0)
    ki = lax.broadcasted_iota(jnp.int32, (blk, 2 * blk), 1)
    dm = qi + blk - ki
    valid = (dm >= 0) & (dm <= blk) & ((ki >= blk) | (i > 0))
    biases = [bias_ref[hh] for hh in range(per)]
    results = []
    for c in range(r):
        rows = pl.ds(c, blk, stride=r) if r > 1 else slice(None)
        q = q_ref[0, rows, :].astype(BF16)
        k = jnp.concatenate([kp_ref[0, rows, :], kc_ref[0, rows, :]], axis=0).astype(BF16)
        v = jnp.concatenate([vp_ref[0, rows, :], vc_ref[0, rows, :]], axis=0).astype(BF16)
        outs, lses = [], []
        for hh in range(per):
            hs = slice(hh * hd, (hh + 1) * hd)
            s = _dot_nt(q[:, hs], k[:, hs]) * DIL_SCALE + biases[hh]
            s = jnp.where(valid, s, -jnp.inf)
            m = jnp.max(s, axis=-1, keepdims=True)
            p = jnp.exp(s - m)
            l = jnp.sum(p, axis=-1, keepdims=True)
            outs.append(_dot(p.astype(BF16), v[:, hs]) / l)
            lses.append(jnp.broadcast_to(m + jnp.log(l), (blk, hd)))
        results.append((rows, jnp.concatenate(outs, axis=-1), jnp.concatenate(lses, axis=-1)))
    for rows, o, lse in results:
        o_ref[0, rows, :] = o
        lse_ref[0, rows, :] = lse


def dil_attn_prompt(qkv, g, ng, r, bias_mat, blk, half):
    b, s, cols = qkv.shape
    nh = bias_mat.shape[0]
    hd = half // nh
    width = LANES if r > 1 else half
    per = width // hd
    tile = blk * r
    assert s % tile == 0 and cols == 3 * ng * half and half % width == 0
    lane_blocks = half // width

    def spec(kind, prev):
        def index(bi, i, hb):
            return (bi, jnp.maximum(i - 1, 0) if prev else i, (kind * ng + g) * lane_blocks + hb)
        return pl.BlockSpec((1, tile, width), index)

    out = pl.BlockSpec((1, tile, width), lambda bi, i, hb: (bi, i, hb))
    o, lse = pl.pallas_call(
        functools.partial(_dil_prompt_kernel, blk=blk, r=r, hd=hd),
        grid=(b, s // tile, lane_blocks),
        in_specs=[spec(0, False), spec(1, False), spec(1, True), spec(2, False), spec(2, True),
                  pl.BlockSpec((per, blk, 2 * blk), lambda bi, i, hb: (hb, 0, 0))],
        out_specs=[out, out],
        out_shape=[jax.ShapeDtypeStruct((b, s, half), F32)] * 2,
        compiler_params=_params("parallel", "arbitrary", "arbitrary"),
        name="dil_attn_prompt",
    )(qkv, qkv, qkv, qkv, qkv, bias_mat)
    return o.reshape(b * s, half), lse.reshape(b * s, half)


def _transpose_rows_kernel(k_ref, v_ref, o_ref):
    half = k_ref.shape[2]
    for c in range(half // LANES):
        cs = slice(c * LANES, (c + 1) * LANES)
        o_ref[0, 0, cs, :] = k_ref[0, :, cs].T
        o_ref[0, 1, cs, :] = v_ref[0, :, cs].T


def dil_state_prompt(qkv, g, ng, keep, half):
    b, s, cols = qkv.shape
    assert keep % LANES == 0 and (s - keep) % LANES == 0
    first = (s - keep) // LANES
    return pl.pallas_call(
        _transpose_rows_kernel,
        grid=(b, keep // LANES),
        in_specs=[pl.BlockSpec((1, LANES, half), lambda bi, i: (bi, first + i, ng + g)),
                  pl.BlockSpec((1, LANES, half), lambda bi, i: (bi, first + i, 2 * ng + g))],
        out_specs=pl.BlockSpec((1, 2, half, LANES), lambda bi, i: (bi, 0, 0, i)),
        out_shape=jax.ShapeDtypeStruct((b, 2, half, keep), F32),
        compiler_params=_params("parallel", "parallel"),
        name="dil_state_prompt",
    )(qkv, qkv)


def _dil_sample_kernel(buf_ref, q_ref, k_ref, v_ref, bias_ref, nbias_ref, hmask_ref, nbuf_ref, o_ref, lse_ref,
                       new_scr, *, eb, t, nh, hd):
    one = lambda ref, e: ref.at[pl.ds(e, 1)]
    for e in range(eb):
        _dil_shift_copy(one(buf_ref, e), one(k_ref, e), one(v_ref, e), one(nbuf_ref, e), new_scr.at[e], t=t)
    results = [_dil_decode_attn(one(buf_ref, e), one(q_ref, e), bias_ref, nbias_ref, hmask_ref, new_scr.at[e],
                                t=t, nh=nh, hd=hd) for e in range(eb)]
    for e, (o, lse) in enumerate(results):
        o_ref[e] = o
        lse_ref[e] = lse


def _dil_shift_copy(buf_ref, k_ref, v_ref, nbuf_ref, new_scr, *, t):
    w = buf_ref.shape[2]
    half = k_ref.shape[2]
    first_new = LANES - t
    new_scr[...] = jnp.zeros(new_scr.shape, F32)
    new_scr[first_new:, :half] = k_ref[0]
    new_scr[first_new:, half:] = v_ref[0]
    lane = lax.broadcasted_iota(jnp.int32, (LANES, LANES), 1)
    for c in range(2 * half // LANES):
        blk = slice(c * LANES, (c + 1) * LANES)
        rolled = pltpu.roll(buf_ref[0, blk, :], w - t, axis=1)
        nbuf_ref[0, blk, :] = rolled
        nbuf_ref[0, blk, w - LANES:] = jnp.where(lane >= first_new, new_scr[:, blk].T, rolled[:, w - LANES:])


def _dil_decode_attn(buf_ref, q_ref, bias_ref, nbias_ref, hmask_ref, new_scr, *, t, nh, hd):
    half = nh * hd
    q8 = jnp.concatenate([q_ref[0], jnp.zeros((SUBLANES - t, half), F32)], axis=0)
    new8 = new_scr[LANES - SUBLANES:, :]
    hmask = hmask_ref[...]
    q_heads = jnp.concatenate([q8] * nh, axis=0) * hmask
    k_t = buf_ref[0, 0:half, :].astype(BF16)
    v_t = buf_ref[0, half:, :].astype(BF16)
    s = _dot(q_heads.astype(BF16), k_t) * DIL_SCALE + bias_ref[...]
    m = jnp.max(s, axis=-1, keepdims=True)
    s_new = []
    for i in range(t):
        row = SUBLANES - t + i
        s_i = (jnp.sum(q_heads * new8[row:row + 1, :half], axis=-1, keepdims=True) * DIL_SCALE
               + nbias_ref[:, i:i + 1])
        s_new.append(s_i)
        m = jnp.maximum(m, s_i)
    p = jnp.exp(s - m)
    l = jnp.sum(p, axis=-1, keepdims=True)
    o = _dot_nt(p.astype(BF16), v_t)
    for i in range(t):
        row = SUBLANES - t + i
        p_i = jnp.exp(s_new[i] - m)
        l = l + p_i
        o = o + p_i * new8[row:row + 1, half:]
    o = (o / l) * hmask
    lse = (m + jnp.log(l)) * hmask
    blocks = lambda a: functools.reduce(lambda x, y: x + y, [a[h * SUBLANES:(h + 1) * SUBLANES] for h in range(nh)])
    return blocks(o)[0:t], blocks(lse)[0:t]


DIL_SAMPLE_BLOCK_BYTES = 4 * 1024 * 1024


def _sample_bias_tables(bias, r, nk, t):
    nh = bias.shape[0]
    w = r * nk
    neg = lambda *shape: jnp.full(shape, -jnp.inf, F32)
    rev = bias[:, nk:0:-1]
    up = rev if r == 1 else jnp.concatenate([rev[:, :, None], neg(nh, nk, r - 1)], axis=2).reshape(nh, w)
    rows = []
    for tt in range(SUBLANES):
        if tt >= t:
            rows.append(neg(nh, w))
        elif tt == 0:
            rows.append(up)
        else:
            rows.append(jnp.concatenate([neg(nh, tt), up[:, :w - tt]], axis=1))
    old = jnp.stack(rows, axis=1)
    cols = []
    for i in range(t):
        col = []
        for tt in range(SUBLANES):
            if tt >= t:
                col.append(jnp.zeros((nh, 1), F32))
            elif i <= tt and (tt - i) % r == 0:
                d = (tt - i) // r
                col.append(bias[:, d:d + 1])
            else:
                col.append(neg(nh, 1))
        cols.append(jnp.concatenate(col, axis=1))
    new = jnp.stack(cols, axis=2)
    return old, jnp.pad(new, ((0, 0), (0, 0), (0, LANES - t)))


def dil_attn_sample(buf_t, qkv, g, bias, r, nk):
    nb, kvw, w = buf_t.shape
    half = kvw // 2
    t = qkv.shape[1]
    n_groups = qkv.shape[2] // (3 * half)
    nh = bias.shape[0]
    assert w == r * nk, "the buffer holds exactly one window"
    assert t <= SUBLANES and w % LANES == 0
    bias_old, bias_new = _sample_bias_tables(bias, r, nk, t)
    bias_old = bias_old.reshape(nh * SUBLANES, w)
    bias_new = bias_new.reshape(nh * SUBLANES, LANES)
    hd = half // nh
    hmask = (jnp.arange(nh * SUBLANES)[:, None] // SUBLANES == jnp.arange(half)[None, :] // hd).astype(F32)
    eb = max(1, min(nb, DIL_SAMPLE_BLOCK_BYTES // (kvw * w * 4)))
    while nb % eb:
        eb -= 1
    col = lambda c: pl.BlockSpec((eb, t, half), lambda bi: (bi, 0, c))
    whole = pl.BlockSpec((eb, kvw, w), lambda bi: (bi, 0, 0))
    return pl.pallas_call(
        functools.partial(_dil_sample_kernel, eb=eb, t=t, nh=nh, hd=hd),
        grid=(nb // eb,),
        in_specs=[whole, col(g), col(n_groups + g), col(2 * n_groups + g),
                  _resident(bias_old.shape), _resident(bias_new.shape), _resident(hmask.shape)],
        out_specs=[whole, pl.BlockSpec((eb, t, half), lambda bi: (bi, 0, 0)),
                   pl.BlockSpec((eb, t, half), lambda bi: (bi, 0, 0))],
        out_shape=[jax.ShapeDtypeStruct((nb, kvw, w), F32), jax.ShapeDtypeStruct((nb, t, half), F32),
                   jax.ShapeDtypeStruct((nb, t, half), F32)],
        scratch_shapes=[pltpu.VMEM((eb, LANES, kvw), F32)],
        compiler_params=_params("parallel"),
        name="dil_attn_sample",
    )(buf_t, qkv, qkv, qkv, bias_old, bias_new, hmask)


def _band_bias(bias, nk):
    nh = bias.shape[0]
    period = 3 * nk
    v = jnp.concatenate([bias[:, ::-1], jnp.broadcast_to(bias[:, :1], (nh, nk - 1)),
                         jnp.broadcast_to(bias[:, nk:], (nh, nk))], axis=1)
    assert v.shape[1] == period
    skew = jnp.broadcast_to(v[:, None, :], (nh, nk, period)).reshape(nh, nk * period)
    return skew[:, :nk * (period - 1)].reshape(nh, nk, period - 1)[:, :, :2 * nk]


def _dil_combine_kernel(*refs, ng):
    o_refs, l_refs = refs[:ng], refs[ng:2 * ng]
    wo_ref, h_ref, y_ref = refs[2 * ng:]
    lses = [l_ref[...] for l_ref in l_refs]
    m = functools.reduce(jnp.maximum, lses)
    es = [jnp.exp(l - m) for l in lses]
    tot = functools.reduce(lambda a, b: a + b, es)
    o = functools.reduce(lambda a, b: a + b, [(e / tot) * o_ref[...] for e, o_ref in zip(es, o_refs)])
    y_ref[...] = h_ref[...] + _dot(o.astype(BF16), wo_ref[...])


def dil_combine_out(outs, lses, w_o, h, tm=1024):
    m, w = outs[0].shape
    d = h.shape[1]
    tm = _row_tile(m, tm)
    ng = len(outs)
    rows = lambda width: pl.BlockSpec((tm, width), lambda i: (i, 0))
    return pl.pallas_call(
        functools.partial(_dil_combine_kernel, ng=ng),
        grid=(m // tm,),
        in_specs=[rows(w)] * (2 * ng) + [_resident(w_o.shape), rows(d)],
        out_specs=rows(d),
        out_shape=jax.ShapeDtypeStruct((m, d), F32),
        compiler_params=_params("parallel"),
        name="dil_combine_out",
    )(*outs, *lses, w_o, h)


def _conv_layer(hp, hs, g_mix, state, w_in, w_dw, b_dw, ln_g, ln_b, w_out, bp, s):
    d = hp.shape[1]
    w_in, w_out = w_in.astype(BF16), w_out.astype(BF16)
    hist = w_dw.shape[0] - 1
    glu_p = conv_in(hp, g_mix, w_in)
    glu_s = conv_in(hs, g_mix, w_in)
    glu_p3 = glu_p.reshape(bp, s, d)
    new_hp = conv_core_prompt(glu_p3, hp.reshape(bp, s, d), w_dw, b_dw, ln_g, ln_b, w_out).reshape(bp * s, d)
    new_hs, st_s = conv_core_sample(glu_s, state, hs, w_dw, b_dw, ln_g, ln_b, w_out)
    st_p = jnp.concatenate([jnp.zeros((bp, hist, d), F32), glu_p3], axis=1)[:, s:]
    return new_hp, new_hs, st_p, st_s


def _rope_tables(pos):
    inv = ROPE_THETA ** (-jnp.arange(0, MLA_ROPE, 2, dtype=F32) / MLA_ROPE)
    ang = pos.astype(F32)[:, None] * inv[None, :]
    cos, sin = jnp.cos(ang), jnp.sin(ang)
    reps = LANES // MLA_ROPE
    return jnp.tile(jnp.concatenate([cos, cos], axis=1), (1, reps)), jnp.tile(jnp.concatenate([-sin, sin], axis=1), (1, reps))


def _swap_halves(w):
    k = w.shape[0]
    w4 = w.reshape(k, -1, 2, MLA_ROPE // 2)
    return w4[:, :, ::-1, :].reshape(k, -1)


def _mla_layer(hp, hs, g_mix, ckv_pool, kpe_pool, page_table, w_dq, g_q, w_uq, w_dkv, g_kv, w_uk, w_uv, w_o, bp, s):
    nb = page_table.shape[0]
    t = hs.shape[0] // nb
    past = page_table.shape[1] * ckv_pool.shape[1]
    qr = w_dq.shape[1]
    uq = w_uq.reshape(qr, MLA_HEADS, MLA_NOPE + MLA_ROPE)
    uq_nope = uq[:, :, :MLA_NOPE].reshape(qr, -1)
    uq_rope = uq[:, :, MLA_NOPE:].reshape(qr, -1)
    dkv_rope = w_dkv[:, MLA_KV_RANK:]
    w = {"dq": w_dq.astype(BF16), "gq": g_q,
         "uq": jnp.concatenate([uq_nope, uq_rope, _swap_halves(uq_rope)], axis=1).astype(BF16),
         "dkv": jnp.concatenate([w_dkv[:, :MLA_KV_RANK], dkv_rope, _swap_halves(dkv_rope)], axis=1).astype(BF16),
         "gkv": g_kv, "uk": w_uk.astype(BF16)}
    w_uv, w_o = w_uv.astype(BF16), w_o.astype(BF16)

    tm_p = _row_tile(s, 256)
    cos_p, sin_p = _rope_tables(jnp.arange(s))
    q_p, kcat_p, ckv_p, kpe_p = mla_in(hp, g_mix, w, cos_p, sin_p, lambda i: i % (s // tm_p), tm=tm_p)
    o_p = mla_attn_prompt(q_p, kcat_p, bp, s)
    new_hp = mla_out(o_p, w_uv, w_o, hp)

    ms = hs.shape[0]
    tm_s = _row_tile(ms, 256)
    assert tm_s % t == 0
    cos_s, sin_s = _rope_tables(past + jnp.arange(t))
    cos_s, sin_s = jnp.tile(cos_s, (tm_s // t, 1)), jnp.tile(sin_s, (tm_s // t, 1))
    q_s, kcat_s, ckv_s, kpe_s = mla_in(hs, g_mix, w, cos_s, sin_s, lambda i: 0, tm=tm_s)
    q_rows = q_s.reshape(MLA_HEADS, nb, t, MLA_QK).transpose(1, 0, 2, 3).reshape(nb, MLA_HEADS * t, MLA_QK)
    o_s = mla_attn_sample(q_rows, kcat_s.reshape(nb, t, MLA_QK), ckv_pool, jnp.swapaxes(kpe_pool, 1, 2), page_table)
    o_s = o_s.reshape(nb, MLA_HEADS, t, MLA_KV_RANK).transpose(1, 0, 2, 3).reshape(MLA_HEADS, ms, MLA_KV_RANK)
    new_hs = mla_out(o_s, w_uv, w_o, hs)
    return (new_hp, new_hs, ckv_p.reshape(bp, s, -1), kpe_p.reshape(bp, s, -1),
            ckv_s.reshape(nb, t, -1), kpe_s.reshape(nb, t, -1))


def _ssd_layer(hp, hs, g_mix, conv_state, ssm_state, w_in, w_conv, b_conv, dt_bias, a_log, d_skip, g_norm, w_out,
               bp, s):
    nb = conv_state.shape[0]
    t = hs.shape[0] // nb
    inner = g_norm.shape[0]
    conv_dim = w_conv.shape[1]
    nh = dt_bias.shape[0]
    hist = SSD_CONV - 1
    ws = [w_in[:, :inner].astype(BF16), w_in[:, inner:inner + conv_dim].astype(BF16),
          jnp.pad(w_in[:, inner + conv_dim:], ((0, 0), (0, LANES - nh))).astype(BF16)]
    w_out = w_out.astype(BF16)
    prm = (w_conv, b_conv, dt_bias, a_log, d_skip, g_norm)

    def run(h, b, l, h0, hist_rows):
        z, xbc, dt = norm_matmul(h, g_mix, ws)
        xbc3 = xbc.reshape(b, l, conv_dim)
        hist8 = jnp.pad(hist_rows, ((0, 0), (SSD_TAIL - hist, 0), (0, 0)))
        y, h_fin = ssd_core(xbc3, z.reshape(b, l, inner), dt.reshape(b, l, LANES), h0, hist8, *prm)
        new_h = matmul_residual(y.reshape(b * l, inner), w_out, h)
        new_hist = jnp.concatenate([hist_rows, xbc3], axis=1)[:, l:]
        return new_h, new_hist, h_fin.reshape(b, nh, SSD_HEAD_DIM, SSD_STATE)

    zero_state = jnp.zeros((bp, nh * SSD_HEAD_DIM, SSD_STATE), F32)
    new_hp, hc_p, hh_p = run(hp, bp, s, zero_state, jnp.zeros((bp, hist, conv_dim), F32))
    new_hs, hc_s, hh_s = run(hs, nb, t, ssm_state.reshape(nb, nh * SSD_HEAD_DIM, SSD_STATE), conv_state)
    return new_hp, new_hs, hc_p, hh_p, hc_s, hh_s


def _dil_layer(hp, hs, g_mix, bufs_in, w_qkv, w_o, rel_bias, bp, s):
    nb = bufs_in[0].shape[0]
    t = hs.shape[0] // nb
    ng = len(DIL_PATTERNS)
    half = DIL_HEADS_PER_GROUP * DIL_HEAD_DIM
    w_qkv, w_o = w_qkv.astype(BF16), w_o.astype(BF16)
    (qkv_p,) = norm_matmul(hp, g_mix, [w_qkv], tm=512)
    (qkv_s,) = norm_matmul(hs, g_mix, [w_qkv])
    qkv_p3 = qkv_p.reshape(bp, s, 3 * ng * half)
    qkv_s3 = qkv_s.reshape(nb, t, 3 * ng * half)
    nh, hd = DIL_HEADS_PER_GROUP, DIL_HEAD_DIM

    outs_p, lses_p, bufs_p, outs_s, lses_s, bufs_s = [], [], [], [], [], []
    for g, (win, r) in enumerate(DIL_PATTERNS):
        nk = win // r
        bias = _group_bias(rel_bias, g, r, nk)
        o, lse = dil_attn_prompt(qkv_p3, g, ng, r, _band_bias(bias, nk), nk, half)
        outs_p.append(o)
        lses_p.append(lse)
        keep = min(win, s)
        st = dil_state_prompt(qkv_p3, g, ng, keep, half)
        bufs_p.append(st.reshape(bp, 2, nh, hd, keep).transpose(0, 4, 1, 2, 3))
        buf = bufs_in[g]
        wb = buf.shape[1]
        buf_t = buf.transpose(0, 2, 3, 4, 1).reshape(nb, 2 * half, wb)
        nbuf_t, o, lse = dil_attn_sample(buf_t, qkv_s3, g, bias, r, nk)
        outs_s.append(o.reshape(nb * t, half))
        lses_s.append(lse.reshape(nb * t, half))
        bufs_s.append(nbuf_t.reshape(nb, 2, nh, hd, wb).transpose(0, 4, 1, 2, 3))
    new_hp = dil_combine_out(outs_p, lses_p, w_o, hp)
    new_hs = dil_combine_out(outs_s, lses_s, w_o, hs)
    return new_hp, new_hs, bufs_p, bufs_s


def kernel(x_prompt, x_sample, state_conv, cache_mla_ckv, cache_mla_kpe, state_ssd_conv, state_ssd, state_dil0_kv, state_dil1_kv, state_dil2_kv, page_table, p_prompt, p_sample, norm_mix, norm_ffn, norm_ple, norm_final, conv_w_in, conv_w_dw, conv_b_dw, conv_ln_g, conv_ln_b, conv_w_out, mla_w_dq, mla_g_q, mla_w_uq, mla_w_dkv, mla_g_kv, mla_w_uk, mla_w_uv, mla_w_o, ssd_w_in, ssd_w_conv, ssd_b_conv, ssd_dt_bias, ssd_a_log, ssd_d, ssd_g_norm, ssd_w_out, dil_w_qkv, dil_w_o, rel_bias, ffn_w1, ffn_w2, ple_w_gate, ple_w_proj):
    bp, s, d = x_prompt.shape
    nb, t, _ = x_sample.shape
    depth = norm_mix.shape[0]
    hp = x_prompt.reshape(bp * s, d)
    hs = x_sample.reshape(nb * t, d)
    w1_all, w2_all = ffn_w1.astype(BF16), ffn_w2.astype(BF16)
    wg_all, wp_all = ple_w_gate.astype(BF16), ple_w_proj.astype(BF16)
    pp_all = p_prompt.reshape(depth, bp * s, -1)
    ps_all = p_sample.reshape(depth, nb * t, -1)
    conv_p, conv_s = [], []
    ckv_p, kpe_p, ckv_s, kpe_s = [], [], [], []
    ssdc_p, ssdh_p, ssdc_s, ssdh_s = [], [], [], []
    dil_p, dil_s = [[], [], []], [[], [], []]
    dil_in = (state_dil0_kv, state_dil1_kv, state_dil2_kv)
    for i in range(depth):
        kind, j = i % 4, i // 4
        if kind == 0:
            hp, hs, st_p, st_s = _conv_layer(hp, hs, norm_mix[i], state_conv[j], conv_w_in[j], conv_w_dw[j],
                                             conv_b_dw[j], conv_ln_g[j], conv_ln_b[j], conv_w_out[j], bp, s)
            conv_p.append(st_p)
            conv_s.append(st_s)
        elif kind == 1:
            hp, hs, c_p, r_p, c_s, r_s = _mla_layer(hp, hs, norm_mix[i], cache_mla_ckv[j], cache_mla_kpe[j], page_table,
                                                    mla_w_dq[j], mla_g_q[j], mla_w_uq[j], mla_w_dkv[j], mla_g_kv[j],
                                                    mla_w_uk[j], mla_w_uv[j], mla_w_o[j], bp, s)
            ckv_p.append(c_p)
            kpe_p.append(r_p)
            ckv_s.append(c_s)
            kpe_s.append(r_s)
        elif kind == 2:
            hp, hs, hc_p, hh_p, hc_s, hh_s = _ssd_layer(hp, hs, norm_mix[i], state_ssd_conv[j], state_ssd[j], ssd_w_in[j],
                                                        ssd_w_conv[j], ssd_b_conv[j], ssd_dt_bias[j], ssd_a_log[j],
                                                        ssd_d[j], ssd_g_norm[j], ssd_w_out[j], bp, s)
            ssdc_p.append(hc_p)
            ssdh_p.append(hh_p)
            ssdc_s.append(hc_s)
            ssdh_s.append(hh_s)
        else:
            hp, hs, bufs_p, bufs_s = _dil_layer(hp, hs, norm_mix[i], [b[j] for b in dil_in], dil_w_qkv[j], dil_w_o[j],
                                                rel_bias, bp, s)
            for g in range(len(DIL_PATTERNS)):
                dil_p[g].append(bufs_p[g])
                dil_s[g].append(bufs_s[g])
        final = i == depth - 1
        hp = ffn(hp, norm_ffn[i], w1_all, w2_all, i)
        hs = ffn(hs, norm_ffn[i], w1_all, w2_all, i)
        hp = ple(hp, pp_all, norm_ple[i], wg_all, wp_all, norm_final, final, i)
        hs = ple(hs, ps_all, norm_ple[i], wg_all, wp_all, norm_final, final, i)
    return (hp.reshape(bp, s, d), hs.reshape(nb, t, d),
            jnp.stack(conv_p), jnp.stack(conv_s),
            jnp.stack(ckv_p), jnp.stack(kpe_p), jnp.stack(ckv_s), jnp.stack(kpe_s),
            jnp.stack(ssdc_p), jnp.stack(ssdh_p), jnp.stack(ssdc_s), jnp.stack(ssdh_s),
            jnp.stack(dil_p[0]), jnp.stack(dil_p[1]), jnp.stack(dil_p[2]),
            jnp.stack(dil_s[0]), jnp.stack(dil_s[1]), jnp.stack(dil_s[2]))
```
